```python
import math
import jax, jax.numpy as jnp
from jax import lax
import numpy as np

D_MODEL = 2048
BATCH = 4
SEQ = 8192
DEPTH = 2
DEC_BATCH = 1
DEC_SEQ = 8192
PAST_LEN = 128

GRID_W = 64
Q_BLOCK = 128
BRANCH_W = 512
N_BRANCH = 4
ROPE_THETA = 10000.0
EPS = 1e-6
A_HEADS = 4
A_Q_LORA = 384
A_KV_LORA = 128
A_NOPE = 128
A_ROPE = 64
A_V = 128
B_HEADS = 4
B_QK = 64
B_V = 2 * B_QK
REL_BUCKETS = 32
REL_MAX_DIST = 128
POOL_WINDOWS = (2, 4, 8, 16)
C_GROUPS = 4
C_GROUP_W = BRANCH_W // C_GROUPS
D_HEADS = 4
D_KV_HEADS = 2
D_HEAD = 128
IN_SPLIT = (A_Q_LORA, A_KV_LORA, A_ROPE, B_HEADS * 2 * B_QK, B_HEADS * 2 * B_QK, B_HEADS * B_V,
            BRANCH_W, D_HEADS * D_HEAD, D_KV_HEADS * D_HEAD, D_KV_HEADS * D_HEAD)
IN_COLS = sum(IN_SPLIT)
N_GROUPS = 4
EXP_PER_GROUP = 8
N_EXPERTS = N_GROUPS * EXP_PER_GROUP
TOP_K = 2
D_EXPERT = 512
MOE_BLOCK = 128

kernel_name = 'hybrid_mla_diff_pool_axialgqa_hiermoe_encoder'


def rms_norm(x, g=None):
    xf = x.astype(jnp.float32)
    y = xf * lax.rsqrt(jnp.mean(xf * xf, axis=-1, keepdims=True) + EPS)
    if g is not None:
        y = y * g.astype(jnp.float32)
    return y.astype(x.dtype)


def rope_angles(pos, dim):
    inv = 1.0 / (ROPE_THETA ** (jnp.arange(0, dim, 2, dtype=jnp.float32) / dim))
    ang = pos.astype(jnp.float32)[:, None] * inv[None, :]
    return jnp.cos(ang), jnp.sin(ang)


def apply_rope(x, cos, sin):
    shape = (cos.shape[0],) + (1,) * (x.ndim - 3) + (cos.shape[1],)
    c = cos.reshape(shape)
    s = sin.reshape(shape)
    x1, x2 = jnp.split(x.astype(jnp.float32), 2, axis=-1)
    return jnp.concatenate([x1 * c - x2 * s, x2 * c + x1 * s], axis=-1).astype(x.dtype)


def t5_bucket(rel):
    half = REL_BUCKETS // 2
    max_exact = half // 2
    ret = (rel > 0).astype(jnp.int32) * half
    n = jnp.abs(rel)
    large = max_exact + (jnp.log(jnp.maximum(n, 1).astype(jnp.float32) / max_exact)
                         / math.log(REL_MAX_DIST / max_exact) * (half - max_exact)).astype(jnp.int32)
    large = jnp.minimum(large, half - 1)
    return ret + jnp.where(n < max_exact, n, large)


def over_query_blocks(fn, *qs):
    B, S = qs[0].shape[:2]
    nb = S // Q_BLOCK
    blocks = tuple(jnp.swapaxes(q.reshape((B, nb, Q_BLOCK) + q.shape[2:]), 0, 1) for q in qs)
    starts = jnp.arange(nb, dtype=jnp.int32) * Q_BLOCK
    out = lax.map(lambda a: fn(a[0], *a[1:]), (starts,) + blocks)
    out = jnp.swapaxes(out, 0, 1)
    return out.reshape((B, S) + out.shape[3:])


def mla_mixer(c_q, c_kv, k_r, q_norm_g, kv_norm_g, w_uq, w_ukv, cos, sin):
    B, S, _ = c_q.shape
    q = (rms_norm(c_q, q_norm_g) @ w_uq).reshape(B, S, A_HEADS, A_NOPE + A_ROPE)
    kv = (rms_norm(c_kv, kv_norm_g) @ w_ukv).reshape(B, S, A_HEADS, A_NOPE + A_V)
    q_nope = q[..., :A_NOPE]
    q_rope = apply_rope(q[..., A_NOPE:], cos, sin)
    k_nope = kv[..., :A_NOPE]
    v = kv[..., A_NOPE:]
    k_rope = apply_rope(k_r, cos, sin)
    scale = (A_NOPE + A_ROPE) ** -0.5

    def block(start, qn, qr):
        s = (jnp.einsum('bqhd,bkhd->bhqk', qn, k_nope, preferred_element_type=jnp.float32)
             + jnp.einsum('bqhr,bkr->bhqk', qr, k_rope, preferred_element_type=jnp.float32)) * scale
        p = jax.nn.softmax(s, axis=-1).astype(v.dtype)
        return jnp.einsum('bhqk,bkhd->bqhd', p, v)

    return over_query_blocks(block, q_nope, q_rope).reshape(B, S, A_HEADS * A_V)


def diff_mixer(q, k, v, lam_q1, lam_k1, lam_q2, lam_k2, rel_bias, lam_init):
    B, S, _ = q.shape
    f32 = jnp.float32
    q = q.reshape(B, S, B_HEADS, 2, B_QK)
    k = k.reshape(B, S, B_HEADS, 2, B_QK)
    v = v.reshape(B, S, B_HEADS, B_V)
    lam = (jnp.exp(jnp.sum(lam_q1.astype(f32) * lam_k1.astype(f32)))
           - jnp.exp(jnp.sum(lam_q2.astype(f32) * lam_k2.astype(f32))) + lam_init)
    key_pos = jnp.arange(S)
    scale = B_QK ** -0.5

    def block(start, qb):
        q_pos = start + jnp.arange(Q_BLOCK)
        bucket = t5_bucket(key_pos[None, :] - q_pos[:, None])
        bias = jnp.transpose(rel_bias[bucket], (2, 0, 1)).astype(f32)
        s = jnp.einsum('bqhcd,bkhcd->bhcqk', qb, k, preferred_element_type=f32) * scale + bias[None, :, None]
        p = jax.nn.softmax(s, axis=-1)
        a = (p[:, :, 0] - lam * p[:, :, 1]).astype(v.dtype)
        return jnp.einsum('bhqk,bkhd->bqhd', a, v)

    o = over_query_blocks(block, q)
    o = rms_norm(o) * (1.0 - lam_init)
    return o.reshape(B, S, B_HEADS * B_V)


def pool_mixer(u, w_pool, pool_scale):
    B, S, _ = u.shape
    ug = u.reshape(B, S, C_GROUPS, C_GROUP_W).astype(jnp.float32)
    cs = jnp.concatenate([jnp.zeros((B, 1, C_GROUPS, C_GROUP_W), jnp.float32), jnp.cumsum(ug, axis=1)], axis=1)
    t = jnp.arange(S)
    outs = []
    for g, w in enumerate(POOL_WINDOWS):
        lo = jnp.clip(t - w // 2, 0, S)
        hi = jnp.clip(t - w // 2 + w, 0, S)
        mean = (cs[:, hi, g] - cs[:, lo, g]) / (hi - lo).astype(jnp.float32)[None, :, None]
        outs.append(mean - ug[:, :, g])
    d = jnp.stack(outs, axis=2).astype(u.dtype)
    y = jnp.einsum('bsgc,gcd->bsgd', d, w_pool).reshape(B, S, BRANCH_W)
    return y * pool_scale


def gqa_axial_mixer(q, k, v, qn_g, kn_g, cos_r, sin_r, cos_c, sin_c):
    B, S, _ = q.shape
    half = D_HEAD // 2
    q = rms_norm(q.reshape(B, S, D_HEADS, D_HEAD), qn_g)
    k = rms_norm(k.reshape(B, S, D_KV_HEADS, D_HEAD), kn_g)
    v = v.reshape(B, S, D_KV_HEADS, D_HEAD)

    def axial(x):
        return jnp.concatenate([apply_rope(x[..., :half], cos_r, sin_r),
                                apply_rope(x[..., half:], cos_c, sin_c)], axis=-1)

    q = axial(q).reshape(B, S, D_KV_HEADS, D_HEADS // D_KV_HEADS, D_HEAD)
    k = axial(k)
    scale = D_HEAD ** -0.5

    def block(start, qb):
        s = jnp.einsum('bqgrd,bkgd->bgrqk', qb, k, preferred_element_type=jnp.float32) * scale
        p = jax.nn.softmax(s, axis=-1).astype(v.dtype)
        return jnp.einsum('bgrqk,bkgd->bqgrd', p, v)

    return over_query_blocks(block, q).reshape(B, S, D_HEADS * D_HEAD)


def hier_moe(h, w_rg, b_rg, w_re, b_re, w_g, w_u, w_d):
    B, S, D = h.shape
    T = B * S
    f32 = jnp.float32
    xf = h.reshape(T, D)
    lg = jnp.einsum('td,dg->tg', xf, w_rg, preferred_element_type=f32) + b_rg.astype(f32)
    pg = jax.nn.softmax(lg, axis=-1)
    g_top = jnp.argmax(lg, axis=-1)
    pg_top = jnp.take_along_axis(pg, g_top[:, None], axis=1)
    le = (jnp.einsum('td,de->te', xf, w_re, preferred_element_type=f32) + b_re.astype(f32))
    le = jnp.take_along_axis(le.reshape(T, N_GROUPS, EXP_PER_GROUP), g_top[:, None, None], axis=1)[:, 0]
    pe = jax.nn.softmax(le, axis=-1)
    top_p, top_i = lax.top_k(pe, TOP_K)
    gates = pg_top * top_p / jnp.sum(top_p, axis=-1, keepdims=True)
    expert = g_top[:, None] * EXP_PER_GROUP + top_i
    A = T * TOP_K
    flat_e = expert.reshape(A)
    flat_w = gates.reshape(A)
    flat_t = jnp.arange(A) // TOP_K
    order = jnp.argsort(flat_e)
    se = flat_e[order]
    counts = jnp.zeros((N_EXPERTS,), jnp.int32).at[flat_e].add(1)
    start = jnp.cumsum(counts) - counts
    padded = (counts + MOE_BLOCK - 1) // MOE_BLOCK * MOE_BLOCK
    pend = jnp.cumsum(padded)
    pstart = pend - padded
    dest = pstart[se] + (jnp.arange(A) - start[se])
    P = (-(-A // MOE_BLOCK) + N_EXPERTS) * MOE_BLOCK
    n_blocks = P // MOE_BLOCK
    slot_tok = jnp.full((P,), T, jnp.int32).at[dest].set(flat_t[order].astype(jnp.int32))
    slot_w = jnp.zeros((P,), h.dtype).at[dest].set(flat_w[order].astype(h.dtype))
    blk_exp = jnp.minimum(jnp.searchsorted(pend, jnp.arange(n_blocks) * MOE_BLOCK, side='right'), N_EXPERTS - 1)
    x_pad = jnp.concatenate([xf, jnp.zeros((1, D), xf.dtype)], axis=0)

    def run(args):
        tok, e = args
        xb = x_pad[tok]
        a = jax.nn.silu(xb @ w_g[e]) * (xb @ w_u[e])
        return a @ w_d[e]

    out = lax.map(run, (slot_tok.reshape(n_blocks, MOE_BLOCK), blk_exp)).reshape(P, D)
    out = out * slot_w[:, None]
    y = jnp.zeros((T + 1, D), h.dtype).at[slot_tok].add(out)[:T]
    return y.reshape(B, S, D)


def encoder(x, p):
    B, S, _ = x.shape
    pos = jnp.arange(S)
    cos_1d, sin_1d = rope_angles(pos, A_ROPE)
    rows = S // GRID_W
    row_ids = jnp.repeat(jnp.arange(rows), GRID_W)
    col_ids = jnp.tile(jnp.arange(GRID_W), rows)
    cos_r, sin_r = rope_angles(row_ids, D_HEAD // 2)
    cos_c, sin_c = rope_angles(col_ids, D_HEAD // 2)
    offsets = []
    acc = 0
    for w in IN_SPLIT[:-1]:
        acc += w
        offsets.append(acc)
    for l in range(DEPTH):
        lam_init = 0.8 - 0.6 * math.exp(-0.3 * l)
        h = rms_norm(x, p['norm1_g'][l])
        z = h @ p['w_in'][l]
        c_q, c_kv, k_r, qb, kb, vb, uc, qd, kd, vd = jnp.split(z, offsets, axis=-1)
        o_a = mla_mixer(c_q, c_kv, k_r, p['q_norm_g'][l], p['kv_norm_g'][l], p['w_uq'][l], p['w_ukv'][l],
                        cos_1d, sin_1d)
        o_b = diff_mixer(qb, kb, vb, p['lam_q1'][l], p['lam_k1'][l], p['lam_q2'][l], p['lam_k2'][l],
                         p['rel_bias'], lam_init)
        o_c = pool_mixer(uc, p['w_pool'][l], p['pool_scale'][l])
        o_d = gqa_axial_mixer(qd, kd, vd, p['qk_norm_q'][l], p['qk_norm_k'][l], cos_r, sin_r, cos_c, sin_c)
        terms = []
        for b, o in enumerate((o_a, o_b, o_c, o_d)):
            gate = jax.nn.sigmoid(h @ p['w_gate'][l, b] + p['b_gate'][l, b])
            terms.append(gate * (o @ p['w_lift'][l, b]))
        merged = terms[0] + terms[1] + terms[2] + terms[3]
        x = x + merged @ p['w_out'][l]
        h2 = rms_norm(x, p['norm2_g'][l])
        x = x + hier_moe(h2, p['w_route_group'][l], p['b_route_group'][l], p['w_route_expert'][l],
                         p['b_route_expert'][l], p['w_exp_gate'][l], p['w_exp_up'][l], p['w_exp_down'][l])
    return rms_norm(x, p['final_g'])


def setup_inputs(seed: int = 0) -> dict:
    key = jax.random.key(seed)
    ks = jax.random.split(key, 32)
    L, D = DEPTH, D_MODEL

    def nrm(k, shape, scale):
        return jax.random.normal(k, shape, jnp.float32) * scale

    return {
        'x_prompt': nrm(ks[0], (BATCH, SEQ, D), 1.0),
        'x_sample': nrm(ks[1], (DEC_BATCH, DEC_SEQ, D), 1.0),
        'norm1_g': 1.0 + nrm(ks[2], (L, D), 0.02),
        'w_in': nrm(ks[3], (L, D, IN_COLS), D ** -0.5),
        'q_norm_g': 1.0 + nrm(ks[4], (L, A_Q_LORA), 0.02),
        'kv_norm_g': 1.0 + nrm(ks[5], (L, A_KV_LORA), 0.02),
        'w_uq': nrm(ks[6], (L, A_Q_LORA, A_HEADS * (A_NOPE + A_ROPE)), A_Q_LORA ** -0.5),
        'w_ukv': nrm(ks[7], (L, A_KV_LORA, A_HEADS * (A_NOPE + A_V)), A_KV_LORA ** -0.5),
        'lam_q1': nrm(ks[8], (L, B_QK), 0.1),
        'lam_k1': nrm(ks[9], (L, B_QK), 0.1),
        'lam_q2': nrm(ks[10], (L, B_QK), 0.1),
        'lam_k2': nrm(ks[11], (L, B_QK), 0.1),
        'rel_bias': nrm(ks[12], (REL_BUCKETS, B_HEADS), 0.5),
        'w_pool': nrm(ks[13], (L, C_GROUPS, C_GROUP_W, C_GROUP_W), C_GROUP_W ** -0.5),
        'pool_scale': 1.0 + nrm(ks[14], (L, BRANCH_W), 0.1),
        'qk_norm_q': 1.0 + nrm(ks[15], (L, D_HEAD), 0.02),
        'qk_norm_k': 1.0 + nrm(ks[16], (L, D_HEAD), 0.02),
        'w_lift': nrm(ks[17], (L, N_BRANCH, BRANCH_W, D), BRANCH_W ** -0.5),
        'w_gate': nrm(ks[18], (L, N_BRANCH, D, D), D ** -0.5),
        'b_gate': nrm(ks[19], (L, N_BRANCH, D), 0.1),
        'w_out': nrm(ks[20], (L, D, D), D ** -0.5),
        'norm2_g': 1.0 + nrm(ks[21], (L, D), 0.02),
        'w_route_group': nrm(ks[22], (L, D, N_GROUPS), D ** -0.5),
        'b_route_group': nrm(ks[23], (L, N_GROUPS), 0.01),
        'w_route_expert': nrm(ks[24], (L, D, N_EXPERTS), D ** -0.5),
        'b_route_expert': nrm(ks[25], (L, N_EXPERTS), 0.01),
        'w_exp_gate': nrm(ks[26], (L, N_EXPERTS, D, D_EXPERT), D ** -0.5),
        'w_exp_up': nrm(ks[27], (L, N_EXPERTS, D, D_EXPERT), D ** -0.5),
        'w_exp_down': nrm(ks[28], (L, N_EXPERTS, D_EXPERT, D), D_EXPERT ** -0.5),
        'final_g': 1.0 + nrm(ks[29], (D,), 0.02),
    }


def reference(x_prompt, x_sample, norm1_g, w_in, q_norm_g, kv_norm_g, w_uq, w_ukv, lam_q1, lam_k1, lam_q2,
              lam_k2, rel_bias, w_pool, pool_scale, qk_norm_q, qk_norm_k, w_lift, w_gate, b_gate, w_out,
              norm2_g, w_route_group, b_route_group, w_route_expert, b_route_expert, w_exp_gate, w_exp_up,
              w_exp_down, final_g):
    p = dict(norm1_g=norm1_g, w_in=w_in, q_norm_g=q_norm_g, kv_norm_g=kv_norm_g, w_uq=w_uq, w_ukv=w_ukv,
             lam_q1=lam_q1, lam_k1=lam_k1, lam_q2=lam_q2, lam_k2=lam_k2, rel_bias=rel_bias, w_pool=w_pool,
             pool_scale=pool_scale, qk_norm_q=qk_norm_q, qk_norm_k=qk_norm_k, w_lift=w_lift, w_gate=w_gate,
             b_gate=b_gate, w_out=w_out, norm2_g=norm2_g, w_route_group=w_route_group,
             b_route_group=b_route_group, w_route_expert=w_route_expert, b_route_expert=b_route_expert,
             w_exp_gate=w_exp_gate, w_exp_up=w_exp_up, w_exp_down=w_exp_down, final_g=final_g)
    y_prompt = encoder(x_prompt, p)
    y_sample = encoder(x_sample, p)
    return (y_prompt, y_sample)
```

```python
import functools
import math

import jax
import jax.numpy as jnp
from jax import lax
from jax.experimental import pallas as pl
from jax.experimental.pallas import tpu as pltpu

F32 = jnp.float32
BF16 = jnp.bfloat16
LOG2E = 1.4426950408889634

D_MODEL = 2048
GRID_W = 64
BRANCH_W = 512
N_BRANCH = 4
ROPE_THETA = 10000.0
EPS = 1e-6
A_HEADS, A_Q_LORA, A_KV_LORA, A_NOPE, A_ROPE, A_V = 4, 384, 128, 128, 64, 128
B_HEADS, B_QK, B_V = 4, 64, 128
REL_BUCKETS, REL_MAX_DIST = 32, 128
POOL_WINDOWS = (2, 4, 8, 16)
C_GROUPS, C_GROUP_W = 4, 128
D_HEADS, D_KV_HEADS, D_HEAD = 4, 2, 128
N_GROUPS, EXP_PER_GROUP, N_EXPERTS, TOP_K, D_EXPERT = 4, 8, 32, 2, 512

LANES = 128
ROPE_PAD = LANES - A_ROPE
ZA_W = A_Q_LORA + A_KV_LORA + LANES
ZB_W = 3 * B_HEADS * B_V
ZC_W = BRANCH_W
ZD_W = (D_HEADS + 2 * D_KV_HEADS) * D_HEAD
A_QK_W = 2 * LANES
ROUTER_W = LANES
HALO = 16
VMEM_LIMIT = 56 * 2 ** 20


def _tiles(seq, tokens):
    def pick(n, pref):
        t = min(pref, n)
        while n % t:
            t //= 2
        return t
    return dict(
        tm=pick(seq, 512),
        tq=pick(seq, 512),
        tn=256,
        tb=256,
        pool_chunk=pick(seq, 1024),
    )


def _cparams(*sem):
    return pltpu.CompilerParams(dimension_semantics=sem, vmem_limit_bytes=VMEM_LIMIT)


def _resident(shape):
    zeros = (0,) * len(shape)
    return pl.BlockSpec(shape, lambda *_: zeros, pipeline_mode=pl.Buffered(1))


def _rms(x, g=None):
    y = x * lax.rsqrt(jnp.mean(x * x, axis=-1, keepdims=True) + EPS)
    return y if g is None else y * g


def _rope_lanes(x, c, s):
    lane = lax.broadcasted_iota(jnp.int32, x.shape, 1)
    first_half = (lane & 32) == 0
    partner = jnp.where(first_half, pltpu.roll(x, 96, 1), pltpu.roll(x, 32, 1))
    return x * c + partner * s


def _norm_inproj_body(x_ref, g_ref, w_ref, h_ref, za_ref, zb_ref, zc_ref, zd_ref):
    h = _rms(x_ref[...], g_ref[...]).astype(BF16)
    h_ref[...] = h
    off = 0
    for z_ref in (za_ref, zb_ref, zc_ref, zd_ref):
        w = z_ref.shape[1]
        z_ref[...] = jnp.dot(h, w_ref[:, off:off + w], preferred_element_type=F32).astype(BF16)
        off += w


def _norm_inproj(x, g, w, tm):
    T, D = x.shape
    widths = (ZA_W, ZB_W, ZC_W, ZD_W)
    row = lambda w_: pl.BlockSpec((tm, w_), lambda i: (i, 0))
    return pl.pallas_call(
        _norm_inproj_body,
        grid=(T // tm,),
        in_specs=[row(D), _resident((1, D)), _resident((D, sum(widths)))],
        out_specs=[row(D)] + [row(w_) for w_ in widths],
        out_shape=[jax.ShapeDtypeStruct((T, D), BF16)] + [jax.ShapeDtypeStruct((T, w_), BF16) for w_ in widths],
        compiler_params=_cparams("parallel"),
        name="norm_inproj",
    )(x, g, w)


def _prep_body(za_ref, zd_ref, gq_ref, gkv_ref, wuq_ref, wukv_ref, gdq_ref, gdk_ref,
               ca_ref, sa_ref, cd_ref, sd_ref, qa_ref, ka_ref, va_ref, qd_ref, kd_ref):
    a_scale = (A_NOPE + A_ROPE) ** -0.5 * LOG2E
    d_scale = D_HEAD ** -0.5 * LOG2E
    ca, sa, cd, sd = ca_ref[...], sa_ref[...], cd_ref[...], sd_ref[...]
    cq = _rms(za_ref[:, :A_Q_LORA].astype(F32), gq_ref[...]).astype(BF16)
    q = jnp.dot(cq, wuq_ref[...], preferred_element_type=F32)
    ckv = _rms(za_ref[:, A_Q_LORA:A_Q_LORA + A_KV_LORA].astype(F32), gkv_ref[...]).astype(BF16)
    kv = jnp.dot(ckv, wukv_ref[...], preferred_element_type=F32)
    k_rope = _rope_lanes(za_ref[:, A_Q_LORA + A_KV_LORA:].astype(F32), ca, sa).astype(BF16)
    nope_w = A_HEADS * A_NOPE
    for h in range(A_HEADS):
        lo = h * A_QK_W
        qa_ref[:, lo:lo + LANES] = (q[:, h * LANES:(h + 1) * LANES] * a_scale).astype(BF16)
        q_rope = _rope_lanes(q[:, nope_w + h * LANES:nope_w + (h + 1) * LANES], ca, sa)
        qa_ref[:, lo + LANES:lo + 2 * LANES] = (q_rope * a_scale).astype(BF16)
        ka_ref[:, lo:lo + LANES] = kv[:, h * LANES:(h + 1) * LANES].astype(BF16)
        ka_ref[:, lo + LANES:lo + 2 * LANES] = k_rope
    va_ref[...] = kv[:, nope_w:].astype(BF16)
    for h in range(D_HEADS):
        xh = _rms(zd_ref[:, h * D_HEAD:(h + 1) * D_HEAD].astype(F32), gdq_ref[...])
        qd_ref[:, h * D_HEAD:(h + 1) * D_HEAD] = (_rope_lanes(xh, cd, sd) * d_scale).astype(BF16)
    k_off = D_HEADS * D_HEAD
    for h in range(D_KV_HEADS):
        xh = _rms(zd_ref[:, k_off + h * D_HEAD:k_off + (h + 1) * D_HEAD].astype(F32), gdk_ref[...])
        kd_ref[:, h * D_HEAD:(h + 1) * D_HEAD] = _rope_lanes(xh, cd, sd).astype(BF16)


def _prep(za, zd, gq, gkv, wuq, wukv, gdq, gdk, tabs, seq, tm):
    T = za.shape[0]
    per_seq = seq // tm
    row = lambda w_: pl.BlockSpec((tm, w_), lambda i: (i, 0))
    tab = pl.BlockSpec((tm, LANES), lambda i: (i % per_seq, 0))
    out_w = (A_HEADS * A_QK_W, A_HEADS * A_QK_W, A_HEADS * A_V, D_HEADS * D_HEAD, D_KV_HEADS * D_HEAD)
    return pl.pallas_call(
        _prep_body,
        grid=(T // tm,),
        in_specs=[row(ZA_W), row(ZD_W), _resident(gq.shape), _resident(gkv.shape), _resident(wuq.shape),
                  _resident(wukv.shape), _resident(gdq.shape), _resident(gdk.shape), tab, tab, tab, tab],
        out_specs=[row(w_) for w_ in out_w],
        out_shape=[jax.ShapeDtypeStruct((T, w_), BF16) for w_ in out_w],
        compiler_params=_cparams("parallel"),
        name="mixer_prep",
    )(za, zd, gq, gkv, wuq, wukv, gdq, gdk, *tabs)


def _qk(q, k):
    return lax.dot_general(q, k, (((1,), (1,)), ((), ())), preferred_element_type=F32)


def _online_update(s, c, m, l, acc, v):
    m_new = jnp.maximum(m, jnp.max(s, axis=-1, keepdims=True) + c)
    alpha = jnp.exp2(m - m_new)
    p = jnp.exp2(s - (m_new - c))
    l = alpha * l + jnp.sum(p, axis=-1, keepdims=True)
    acc = alpha * acc + jnp.dot(p.astype(BF16), v, preferred_element_type=F32)
    return m_new, l, acc


def _softmax_state(tq, dv):
    return (jnp.full((tq, 1), -jnp.inf, F32), jnp.zeros((tq, 1), F32), jnp.zeros((tq, dv), F32))


def _flash_body(q_ref, k_ref, v_ref, o_ref, *, tk):
    q = q_ref[0]
    tq = q.shape[0]
    nk = k_ref.shape[1] // tk

    def step(j, carry):
        ks = pl.multiple_of(j * tk, tk)
        return _online_update(_qk(q, k_ref[0, pl.ds(ks, tk), :]), 0.0, *carry, v_ref[0, pl.ds(ks, tk), :])

    _, l, acc = lax.fori_loop(0, nk, step, _softmax_state(tq, v_ref.shape[2]))
    o_ref[0] = (acc / l).astype(o_ref.dtype)


def _flash(q, k, v, *, heads, q_w, kv_of_head, k_col0, v_col0, tq):
    B, S, _ = q.shape
    dv = LANES
    return pl.pallas_call(
        functools.partial(_flash_body, tk=tq),
        grid=(B, heads, S // tq),
        in_specs=[pl.BlockSpec((1, tq, q_w), lambda b, h, i: (b, i, h)),
                  pl.BlockSpec((1, S, q_w), lambda b, h, i: (b, 0, k_col0 + kv_of_head(h))),
                  pl.BlockSpec((1, S, dv), lambda b, h, i: (b, 0, v_col0 + kv_of_head(h)))],
        out_specs=pl.BlockSpec((1, tq, dv), lambda b, h, i: (b, i, h)),
        out_shape=jax.ShapeDtypeStruct((B, S, heads * dv), BF16),
        compiler_params=_cparams("parallel", "parallel", "arbitrary"),
        name="flash_attn",
    )(q, k, v)


def _diff_body(lam_ref, cb_ref, q_ref, k_ref, v_ref, bias_ref, o_ref, *, tk, out_scale):
    h, qi = pl.program_id(1), pl.program_id(2)
    tq = q_ref.shape[1]
    nk = k_ref.shape[1] // tk
    qf = q_ref[0].astype(F32) * (B_QK ** -0.5 * LOG2E)
    lane = lax.broadcasted_iota(jnp.int32, qf.shape, 1)
    q1 = jnp.where(lane < B_QK, qf, 0.0).astype(BF16)
    q2 = jnp.where(lane >= B_QK, qf, 0.0).astype(BF16)

    def chunk(j):
        ks = pl.multiple_of(j * tk, tk)
        return k_ref[0, pl.ds(ks, tk), :], v_ref[0, pl.ds(ks, tk), :]

    def far_step(c):
        def step(j, carry):
            k, v = chunk(j)
            return _online_update(_qk(q1, k), c, *carry[:3], v) + _online_update(_qk(q2, k), c, *carry[3:], v)
        return step

    def near_step(j, d, carry):
        k, v = chunk(j)
        bias = bias_ref[0, d]
        return (_online_update(_qk(q1, k) + bias, 0.0, *carry[:3], v)
                + _online_update(_qk(q2, k) + bias, 0.0, *carry[3:], v))

    carry = _softmax_state(tq, B_V) + _softmax_state(tq, B_V)
    carry = lax.fori_loop(0, jnp.maximum(qi - 1, 0), far_step(cb_ref[0, h]), carry)
    for d in range(3):
        j = qi + (d - 1)
        valid = jnp.logical_and(j >= 0, j < nk)
        carry = lax.cond(valid, functools.partial(near_step, jnp.clip(j, 0, nk - 1), d), lambda c_: c_, carry)
    carry = lax.fori_loop(jnp.minimum(qi + 2, nk), nk, far_step(cb_ref[1, h]), carry)
    _, l1, a1, _, l2, a2 = carry
    o = a1 / l1 - lam_ref[0] * (a2 / l2)
    o_ref[0] = (_rms(o) * out_scale).astype(o_ref.dtype)


def _rel_bucket(rel):
    half = REL_BUCKETS // 2
    max_exact = half // 2
    ret = (rel > 0).astype(jnp.int32) * half
    n = jnp.abs(rel)
    large = max_exact + (jnp.log(jnp.maximum(n, 1).astype(F32) / max_exact)
                         / math.log(REL_MAX_DIST / max_exact) * (half - max_exact)).astype(jnp.int32)
    large = jnp.minimum(large, half - 1)
    return ret + jnp.where(n < max_exact, n, large)


def _diff_bias_tables(rel_bias, tq):
    a = jnp.arange(tq)
    rel = (jnp.arange(-1, 2)[:, None, None] * tq + a[None, None, :]) - a[None, :, None]
    near = jnp.transpose(rel_bias[_rel_bucket(rel)], (3, 0, 1, 2)).astype(F32) * LOG2E
    far = jnp.stack([rel_bias[REL_BUCKETS // 2 - 1], rel_bias[REL_BUCKETS - 1]]).astype(F32) * LOG2E
    return near, far


def _diff_attn(zb, lam, near, far, *, tq, out_scale):
    B, S, _ = zb.shape
    assert tq >= REL_MAX_DIST, "far key chunks must lie beyond the last distinct relative bucket"
    H = B_HEADS
    smem = pl.BlockSpec(memory_space=pltpu.SMEM)
    return pl.pallas_call(
        functools.partial(_diff_body, tk=tq, out_scale=out_scale),
        grid=(B, H, S // tq),
        in_specs=[smem, smem,
                  pl.BlockSpec((1, tq, B_V), lambda b, h, i: (b, i, h)),
                  pl.BlockSpec((1, S, B_V), lambda b, h, i: (b, 0, H + h)),
                  pl.BlockSpec((1, S, B_V), lambda b, h, i: (b, 0, 2 * H + h)),
                  pl.BlockSpec((1, 3, tq, tq), lambda b, h, i: (h, 0, 0, 0))],
        out_specs=pl.BlockSpec((1, tq, B_V), lambda b, h, i: (b, i, h)),
        out_shape=jax.ShapeDtypeStruct((B, S, H * B_V), BF16),
        compiler_params=_cparams("parallel", "parallel", "arbitrary"),
        name="diff_attn",
    )(lam, far, zb, zb, zb, near)


def _pool_body(u_ref, w_ref, sc_ref, o_ref, pad_ref, *, chunk):
    g = pl.program_id(1)
    S = u_ref.shape[1]
    pad_ref[0:HALO, :] = jnp.zeros((HALO, LANES), F32)
    pad_ref[HALO + S:, :] = jnp.zeros((HALO, LANES), F32)
    pad_ref[HALO:HALO + S, :] = u_ref[0].astype(F32)
    w_mat, sc = w_ref[0], sc_ref[...]

    def pooled(win):
        def body(c, _):
            r0 = pl.multiple_of(c * chunk, chunk)
            tot = pad_ref[pl.ds(r0 + HALO - win // 2, chunk), :]
            for j in range(1 - win // 2, win // 2):
                tot = tot + pad_ref[pl.ds(r0 + HALO + j, chunk), :]
            t = r0 + lax.broadcasted_iota(jnp.int32, (chunk, 1), 0)
            cnt = jnp.clip(t - win // 2 + win, 0, S) - jnp.clip(t - win // 2, 0, S)
            d = tot / cnt.astype(F32) - pad_ref[pl.ds(r0 + HALO, chunk), :]
            y = jnp.dot(d.astype(BF16), w_mat, preferred_element_type=F32) * sc
            o_ref[0, pl.ds(r0, chunk), :] = y.astype(o_ref.dtype)
            return 0
        lax.fori_loop(0, S // chunk, body, 0)

    for gi, win in enumerate(POOL_WINDOWS):
        pl.when(g == gi)(functools.partial(pooled, win))


def _pool(zc, w_pool, pool_scale, chunk):
    B, S, _ = zc.shape
    blk = pl.BlockSpec((1, S, LANES), lambda b, g: (b, 0, g))
    return pl.pallas_call(
        functools.partial(_pool_body, chunk=chunk),
        grid=(B, C_GROUPS),
        in_specs=[blk, pl.BlockSpec((1, C_GROUP_W, C_GROUP_W), lambda b, g: (g, 0, 0)),
                  pl.BlockSpec((1, LANES), lambda b, g: (0, g))],
        out_specs=blk,
        out_shape=jax.ShapeDtypeStruct((B, S, BRANCH_W), BF16),
        scratch_shapes=[pltpu.VMEM((S + 2 * HALO, LANES), F32)],
        compiler_params=_cparams("parallel", "arbitrary"),
        name="pool_mixer",
    )(zc, w_pool, pool_scale)


def _merge_body(h_ref, oa_ref, ob_ref, oc_ref, od_ref, x_ref, wg_ref, bg_ref, wl_ref, wo_ref, g2_ref,
                wr_hi_ref, wr_lo_ref, xn_ref, h2_ref, lg_ref, acc_ref):
    j = pl.program_id(1)

    @pl.when(j == 0)
    def _():
        acc_ref[...] = x_ref[...]

    h = h_ref[...]
    merged = None
    for b, o_ref in enumerate((oa_ref, ob_ref, oc_ref, od_ref)):
        gate = jax.nn.sigmoid(jnp.dot(h, wg_ref[b], preferred_element_type=F32) + bg_ref[b:b + 1, :])
        term = gate * jnp.dot(o_ref[...], wl_ref[b], preferred_element_type=F32)
        merged = term if merged is None else merged + term
    acc_ref[...] += jnp.dot(merged.astype(BF16), wo_ref[...], preferred_element_type=F32)

    @pl.when(j == pl.num_programs(1) - 1)
    def _():
        xn = acc_ref[...]
        xn_ref[...] = xn
        h2 = _rms(xn, g2_ref[...])
        h2_ref[...] = h2.astype(BF16)
        h_hi = h2.astype(BF16)
        h_lo = (h2 - h_hi.astype(F32)).astype(BF16)
        lg_ref[...] = (jnp.dot(h_hi, wr_hi_ref[...], preferred_element_type=F32)
                       + jnp.dot(h_hi, wr_lo_ref[...], preferred_element_type=F32)
                       + jnp.dot(h_lo, wr_hi_ref[...], preferred_element_type=F32))


def _merge(h, outs, x, wg, bg, wl, wo, g2, wr_hi, wr_lo, tm, tn):
    T, D = x.shape
    row = lambda w_: pl.BlockSpec((tm, w_), lambda i, j: (i, 0))
    return pl.pallas_call(
        _merge_body,
        grid=(T // tm, D // tn),
        in_specs=[row(D)] + [row(BRANCH_W)] * N_BRANCH + [
            row(D),
            pl.BlockSpec((N_BRANCH, D, tn), lambda i, j: (0, 0, j)),
            pl.BlockSpec((N_BRANCH, tn), lambda i, j: (0, j)),
            pl.BlockSpec((N_BRANCH, BRANCH_W, tn), lambda i, j: (0, 0, j)),
            pl.BlockSpec((tn, D), lambda i, j: (j, 0)),
            _resident((1, D)), _resident((D, ROUTER_W)), _resident((D, ROUTER_W))],
        out_specs=[row(D), row(D), row(ROUTER_W)],
        out_shape=[jax.ShapeDtypeStruct((T, D), F32), jax.ShapeDtypeStruct((T, D), BF16),
                   jax.ShapeDtypeStruct((T, ROUTER_W), F32)],
        scratch_shapes=[pltpu.VMEM((tm, D), F32)],
        compiler_params=_cparams("parallel", "arbitrary"),
        name="gated_merge",
    )(h, *outs, x, wg, bg, wl, wo, g2, wr_hi, wr_lo)


def _route(logits, b_rg, b_re):
    T = logits.shape[0]
    lg = logits[:, :N_GROUPS] + b_rg.astype(F32)
    pg = jax.nn.softmax(lg, axis=-1)
    g_top = jnp.argmax(lg, axis=-1)
    pg_top = jnp.take_along_axis(pg, g_top[:, None], axis=1)
    le = (logits[:, N_GROUPS:N_GROUPS + N_EXPERTS] + b_re.astype(F32)).reshape(T, N_GROUPS, EXP_PER_GROUP)
    le = jnp.take_along_axis(le, g_top[:, None, None], axis=1)[:, 0]
    pe = jax.nn.softmax(le, axis=-1)
    top_p, top_i = lax.top_k(pe, TOP_K)
    gates = pg_top * top_p / jnp.sum(top_p, axis=-1, keepdims=True)
    expert = g_top[:, None] * EXP_PER_GROUP + top_i
    return expert.astype(jnp.int32), gates


def _dispatch(expert, gates, tb):
    T = expert.shape[0]
    A = T * TOP_K
    flat_e = expert.reshape(A)
    order = jnp.argsort(flat_e)
    se = flat_e[order]
    counts = jnp.zeros((N_EXPERTS,), jnp.int32).at[flat_e].add(1)
    start = jnp.cumsum(counts) - counts
    padded = (counts + tb - 1) // tb * tb
    pend = jnp.cumsum(padded)
    pstart = pend - padded
    dest = (pstart[se] + (jnp.arange(A, dtype=jnp.int32) - start[se])).astype(jnp.int32)
    n_blocks = -(-A // tb) + N_EXPERTS
    P = n_blocks * tb
    slot_tok = jnp.zeros((P,), jnp.int32).at[dest].set((order // TOP_K).astype(jnp.int32))
    slot_w = jnp.zeros((P,), F32).at[dest].set(gates.reshape(A)[order])
    blk_exp = jnp.minimum(jnp.searchsorted(pend, jnp.arange(n_blocks) * tb, side='right'),
                          N_EXPERTS - 1).astype(jnp.int32)
    n_valid = (pend[-1] // tb).astype(jnp.int32).reshape(1)
    slot_of = jnp.zeros((A,), jnp.int32).at[order].set(dest).reshape(T, TOP_K)
    return slot_tok, slot_w, blk_exp, n_valid, slot_of


def _expert_body(be_ref, nv_ref, x_ref, sw_ref, wg_ref, wu_ref, wd_ref, o_ref):
    del be_ref
    i = pl.program_id(0)

    @pl.when(i < nv_ref[0])
    def _():
        x = x_ref[...]
        g = jnp.dot(x, wg_ref[0], preferred_element_type=F32)
        u = jnp.dot(x, wu_ref[0], preferred_element_type=F32)
        a = (g * jax.nn.sigmoid(g) * u).astype(BF16)
        y = jnp.dot(a, wd_ref[0], preferred_element_type=F32)
        o_ref[...] = (y * sw_ref[...]).astype(o_ref.dtype)

    @pl.when(i >= nv_ref[0])
    def _():
        o_ref[...] = jnp.zeros(o_ref.shape, o_ref.dtype)


def _expert_ffn(xs, slot_w, blk_exp, n_valid, wg, wu, wd, tb):
    P, D = xs.shape
    grid_spec = pltpu.PrefetchScalarGridSpec(
        num_scalar_prefetch=2,
        grid=(P // tb,),
        in_specs=[pl.BlockSpec((tb, D), lambda i, be, nv: (i, 0)),
                  pl.BlockSpec((tb, 1), lambda i, be, nv: (i, 0)),
                  pl.BlockSpec((1, D, D_EXPERT), lambda i, be, nv: (be[i], 0, 0)),
                  pl.BlockSpec((1, D, D_EXPERT), lambda i, be, nv: (be[i], 0, 0)),
                  pl.BlockSpec((1, D_EXPERT, D), lambda i, be, nv: (be[i], 0, 0))],
        out_specs=pl.BlockSpec((tb, D), lambda i, be, nv: (i, 0)),
    )
    return pl.pallas_call(
        _expert_body,
        grid_spec=grid_spec,
        out_shape=jax.ShapeDtypeStruct((P, D), BF16),
        compiler_params=_cparams("arbitrary"),
        name="expert_ffn",
    )(blk_exp, n_valid, xs, slot_w.reshape(P, 1), wg, wu, wd)


def _final_norm_body(x_ref, g_ref, o_ref):
    o_ref[...] = _rms(x_ref[...], g_ref[...])


def _final_norm(x, g, tm):
    T, D = x.shape
    row = pl.BlockSpec((tm, D), lambda i: (i, 0))
    return pl.pallas_call(
        _final_norm_body, grid=(T // tm,), in_specs=[row, _resident((1, D))], out_specs=row,
        out_shape=jax.ShapeDtypeStruct((T, D), F32), compiler_params=_cparams("parallel"), name="final_norm",
    )(x, g)


def _rope_tables(seq):
    def angles(pos, dim):
        inv = 1.0 / (ROPE_THETA ** (jnp.arange(0, dim, 2, dtype=F32) / dim))
        ang = pos.astype(F32)[:, None] * inv[None, :]
        return jnp.cos(ang), jnp.sin(ang)
    pos = jnp.arange(seq)
    c1, s1 = angles(pos, A_ROPE)
    cr, sr = angles(pos // GRID_W, D_HEAD // 2)
    cc, sc = angles(pos % GRID_W, D_HEAD // 2)
    pad1, pad0 = jnp.ones((seq, ROPE_PAD), F32), jnp.zeros((seq, ROPE_PAD), F32)
    return (jnp.concatenate([c1, c1, pad1], 1), jnp.concatenate([-s1, s1, pad0], 1),
            jnp.concatenate([cr, cr, cc, cc], 1), jnp.concatenate([-sr, sr, -sc, sc], 1))


def _layer_weights(l, w_in, w_uq, w_ukv, w_route_group, w_route_expert):
    D = D_MODEL
    k_r_end = A_Q_LORA + A_KV_LORA + A_ROPE
    win = jnp.concatenate([w_in[l][:, :k_r_end], jnp.zeros((D, ROPE_PAD), F32), w_in[l][:, k_r_end:]], 1)
    uq = w_uq[l].reshape(A_Q_LORA, A_HEADS, A_NOPE + A_ROPE)
    uq_rope = jnp.pad(uq[:, :, A_NOPE:], ((0, 0), (0, 0), (0, ROPE_PAD)))
    wuq = jnp.concatenate([uq[:, :, :A_NOPE].reshape(A_Q_LORA, -1), uq_rope.reshape(A_Q_LORA, -1)], 1)
    ukv = w_ukv[l].reshape(A_KV_LORA, A_HEADS, A_NOPE + A_V)
    wukv = jnp.concatenate([ukv[:, :, :A_NOPE].reshape(A_KV_LORA, -1), ukv[:, :, A_NOPE:].reshape(A_KV_LORA, -1)], 1)
    wr = jnp.concatenate([w_route_group[l], w_route_expert[l],
                          jnp.zeros((D, ROUTER_W - N_GROUPS - N_EXPERTS), F32)], 1)
    wr_hi = wr.astype(BF16)
    wr_lo = (wr - wr_hi.astype(F32)).astype(BF16)
    return win.astype(BF16), wuq.astype(BF16), wukv.astype(BF16), wr_hi, wr_lo


def kernel(x_prompt, x_sample, norm1_g, w_in, q_norm_g, kv_norm_g, w_uq, w_ukv, lam_q1, lam_k1, lam_q2, lam_k2,
           rel_bias, w_pool, pool_scale, qk_norm_q, qk_norm_k, w_lift, w_gate, b_gate, w_out, norm2_g,
           w_route_group, b_route_group, w_route_expert, b_route_expert, w_exp_gate, w_exp_up, w_exp_down,
           final_g):
    assert x_prompt.shape[1:] == x_sample.shape[1:], "both request groups must share (seq, d_model)"
    n_prompt = x_prompt.shape[0]
    x = jnp.concatenate([x_prompt, x_sample], axis=0)
    B, S, D = x.shape
    T = B * S
    t = _tiles(S, T)
    x = x.reshape(T, D)
    depth = w_in.shape[0]
    tabs = _rope_tables(S)
    near, far = _diff_bias_tables(rel_bias, t["tq"])
    row2 = lambda v: v.reshape(1, -1).astype(F32)

    for l in range(depth):
        lam_init = 0.8 - 0.6 * math.exp(-0.3 * l)
        lam = (jnp.exp(jnp.sum(lam_q1[l] * lam_k1[l])) - jnp.exp(jnp.sum(lam_q2[l] * lam_k2[l])) + lam_init)
        win, wuq, wukv, wr_hi, wr_lo = _layer_weights(l, w_in, w_uq, w_ukv, w_route_group, w_route_expert)

        h, za, zb, zc, zd = _norm_inproj(x, row2(norm1_g[l]), win, t["tm"])
        qa, ka, va, qd, kd = _prep(za, zd, row2(q_norm_g[l]), row2(kv_norm_g[l]), wuq, wukv,
                                   row2(qk_norm_q[l]), row2(qk_norm_k[l]), tabs, S, t["tm"])
        seq3 = lambda a: a.reshape(B, S, a.shape[-1])
        o_a = _flash(seq3(qa), seq3(ka), seq3(va), heads=A_HEADS, q_w=A_QK_W, kv_of_head=lambda hh: hh,
                     k_col0=0, v_col0=0, tq=t["tq"])
        o_b = _diff_attn(seq3(zb), lam.reshape(1).astype(F32), near, far, tq=t["tq"], out_scale=1.0 - lam_init)
        o_c = _pool(seq3(zc), w_pool[l].astype(BF16), row2(pool_scale[l]), t["pool_chunk"])
        rep = D_HEADS // D_KV_HEADS
        o_d = _flash(seq3(qd), seq3(kd), seq3(zd), heads=D_HEADS, q_w=D_HEAD, kv_of_head=lambda hh: hh // rep,
                     k_col0=0, v_col0=D_HEADS + D_KV_HEADS, tq=t["tq"])
        outs = [o.reshape(T, BRANCH_W) for o in (o_a, o_b, o_c, o_d)]
        xn, h2, logits = _merge(h, outs, x, w_gate[l].astype(BF16), b_gate[l].astype(F32), w_lift[l].astype(BF16),
                                w_out[l].astype(BF16), row2(norm2_g[l]), wr_hi, wr_lo, t["tm"], t["tn"])

        expert, gates = _route(logits, b_route_group[l], b_route_expert[l])
        slot_tok, slot_w, blk_exp, n_valid, slot_of = _dispatch(expert, gates, t["tb"])
        ys = _expert_ffn(h2[slot_tok], slot_w, blk_exp, n_valid, w_exp_gate[l].astype(BF16),
                         w_exp_up[l].astype(BF16), w_exp_down[l].astype(BF16), t["tb"])
        x = xn + ys[slot_of[:, 0]].astype(F32) + ys[slot_of[:, 1]].astype(F32)

    y = _final_norm(x, row2(final_g), t["tm"]).reshape(B, S, D)
    return y[:n_prompt], y[n_prompt:]
```

```python
import functools
import math

import jax
import jax.numpy as jnp
from jax import lax
from jax.experimental import pallas as pl
from jax.experimental.pallas import tpu as pltpu

F32 = jnp.float32
BF16 = jnp.bfloat16
LOG2E = 1.4426950408889634

D_MODEL = 2048
GRID_W = 64
BRANCH_W = 512
N_BRANCH = 4
ROPE_THETA = 10000.0
EPS = 1e-6
A_HEADS, A_Q_LORA, A_KV_LORA, A_NOPE, A_ROPE, A_V = 4, 384, 128, 128, 64, 128
B_HEADS, B_QK, B_V = 4, 64, 128
REL_BUCKETS, REL_MAX_DIST = 32, 128
POOL_WINDOWS = (2, 4, 8, 16)
C_GROUPS, C_GROUP_W = 4, 128
D_HEADS, D_KV_HEADS, D_HEAD = 4, 2, 128
N_GROUPS, EXP_PER_GROUP, N_EXPERTS, TOP_K, D_EXPERT = 4, 8, 32, 2, 512

LANES = 128
ROPE_PAD = LANES - A_ROPE
ZA_W = A_Q_LORA + A_KV_LORA + LANES
ZB_W = 3 * B_HEADS * B_V
ZC_W = BRANCH_W
ZD_W = (D_HEADS + 2 * D_KV_HEADS) * D_HEAD
A_QK_W = 2 * LANES
ROUTER_W = LANES
HALO = 16
VMEM_LIMIT = 56 * 2 ** 20


def _tiles(seq, tokens):
    def pick(n, pref):
        t = min(pref, n)
        while n % t:
            t //= 2
        return t
    return dict(
        tm=pick(seq, 512),
        tq=pick(seq, 512),
        tk=pick(seq, 4096),
        tn=256,
        tb=256,
        pool_chunk=pick(seq, 1024),
    )


def _cparams(*sem):
    return pltpu.CompilerParams(dimension_semantics=sem, vmem_limit_bytes=VMEM_LIMIT)


def _resident(shape):
    zeros = (0,) * len(shape)
    return pl.BlockSpec(shape, lambda *_: zeros, pipeline_mode=pl.Buffered(1))


def _rms(x, g=None):
    y = x * lax.rsqrt(jnp.mean(x * x, axis=-1, keepdims=True) + EPS)
    return y if g is None else y * g


def _rope_lanes(x, c, s):
    lane = lax.broadcasted_iota(jnp.int32, x.shape, 1)
    first_half = (lane & 32) == 0
    partner = jnp.where(first_half, pltpu.roll(x, 96, 1), pltpu.roll(x, 32, 1))
    return x * c + partner * s


def _norm_inproj_body(x_ref, g_ref, w_ref, h_ref, za_ref, zb_ref, zc_ref, zd_ref):
    h = _rms(x_ref[...], g_ref[...]).astype(BF16)
    h_ref[...] = h
    off = 0
    for z_ref in (za_ref, zb_ref, zc_ref, zd_ref):
        w = z_ref.shape[1]
        z_ref[...] = jnp.dot(h, w_ref[:, off:off + w], preferred_element_type=F32).astype(BF16)
        off += w


def _norm_inproj(x, g, w, tm):
    T, D = x.shape
    widths = (ZA_W, ZB_W, ZC_W, ZD_W)
    row = lambda w_: pl.BlockSpec((tm, w_), lambda i: (i, 0))
    return pl.pallas_call(
        _norm_inproj_body,
        grid=(T // tm,),
        in_specs=[row(D), _resident((1, D)), _resident((D, sum(widths)))],
        out_specs=[row(D)] + [row(w_) for w_ in widths],
        out_shape=[jax.ShapeDtypeStruct((T, D), BF16)] + [jax.ShapeDtypeStruct((T, w_), BF16) for w_ in widths],
        compiler_params=_cparams("parallel"),
        name="norm_inproj",
    )(x, g, w)


def _prep_body(za_ref, zd_ref, gq_ref, gkv_ref, wuq_ref, wukv_ref, gdq_ref, gdk_ref,
               ca_ref, sa_ref, cd_ref, sd_ref, qa_ref, ka_ref, va_ref, qd_ref, kd_ref):
    a_scale = (A_NOPE + A_ROPE) ** -0.5 * LOG2E
    d_scale = D_HEAD ** -0.5 * LOG2E
    ca, sa, cd, sd = ca_ref[...], sa_ref[...], cd_ref[...], sd_ref[...]
    cq = _rms(za_ref[:, :A_Q_LORA].astype(F32), gq_ref[...]).astype(BF16)
    q = jnp.dot(cq, wuq_ref[...], preferred_element_type=F32)
    ckv = _rms(za_ref[:, A_Q_LORA:A_Q_LORA + A_KV_LORA].astype(F32), gkv_ref[...]).astype(BF16)
    kv = jnp.dot(ckv, wukv_ref[...], preferred_element_type=F32)
    k_rope = _rope_lanes(za_ref[:, A_Q_LORA + A_KV_LORA:].astype(F32), ca, sa).astype(BF16)
    nope_w = A_HEADS * A_NOPE
    for h in range(A_HEADS):
        lo = h * A_QK_W
        qa_ref[:, lo:lo + LANES] = (q[:, h * LANES:(h + 1) * LANES] * a_scale).astype(BF16)
        q_rope = _rope_lanes(q[:, nope_w + h * LANES:nope_w + (h + 1) * LANES], ca, sa)
        qa_ref[:, lo + LANES:lo + 2 * LANES] = (q_rope * a_scale).astype(BF16)
        ka_ref[:, lo:lo + LANES] = kv[:, h * LANES:(h + 1) * LANES].astype(BF16)
        ka_ref[:, lo + LANES:lo + 2 * LANES] = k_rope
    va_ref[...] = kv[:, nope_w:].astype(BF16)
    for h in range(D_HEADS):
        xh = _rms(zd_ref[:, h * D_HEAD:(h + 1) * D_HEAD].astype(F32), gdq_ref[...])
        qd_ref[:, h * D_HEAD:(h + 1) * D_HEAD] = (_rope_lanes(xh, cd, sd) * d_scale).astype(BF16)
    k_off = D_HEADS * D_HEAD
    for h in range(D_KV_HEADS):
        xh = _rms(zd_ref[:, k_off + h * D_HEAD:k_off + (h + 1) * D_HEAD].astype(F32), gdk_ref[...])
        kd_ref[:, h * D_HEAD:(h + 1) * D_HEAD] = _rope_lanes(xh, cd, sd).astype(BF16)


def _prep(za, zd, gq, gkv, wuq, wukv, gdq, gdk, tabs, seq, tm):
    T = za.shape[0]
    per_seq = seq // tm
    row = lambda w_: pl.BlockSpec((tm, w_), lambda i: (i, 0))
    tab = pl.BlockSpec((tm, LANES), lambda i: (i % per_seq, 0))
    out_w = (A_HEADS * A_QK_W, A_HEADS * A_QK_W, A_HEADS * A_V, D_HEADS * D_HEAD, D_KV_HEADS * D_HEAD)
    return pl.pallas_call(
        _prep_body,
        grid=(T // tm,),
        in_specs=[row(ZA_W), row(ZD_W), _resident(gq.shape), _resident(gkv.shape), _resident(wuq.shape),
                  _resident(wukv.shape), _resident(gdq.shape), _resident(gdk.shape), tab, tab, tab, tab],
        out_specs=[row(w_) for w_ in out_w],
        out_shape=[jax.ShapeDtypeStruct((T, w_), BF16) for w_ in out_w],
        compiler_params=_cparams("parallel"),
        name="mixer_prep",
    )(za, zd, gq, gkv, wuq, wukv, gdq, gdk, *tabs)


def _qk(q, k):
    return lax.dot_general(q, k, (((1,), (1,)), ((), ())), preferred_element_type=F32)


def _lane_fold(fn, acc, x):
    for c in range(x.shape[1] // LANES):
        acc = fn(acc, x[:, c * LANES:(c + 1) * LANES])
    return acc


def _score_pass(q, k_ref, s_ref, tk, max_shift=None):
    tq, ts = s_ref.shape[1:]
    sub = tk // ts

    def step(j, mrun):
        ks = pl.multiple_of(j * tk, tk)
        s = _qk(q, k_ref[0, pl.ds(ks, tk), :])
        for c in range(sub):
            jj = j * sub + c
            sc = s[:, c * ts:(c + 1) * ts]
            s_ref[jj] = sc
            m_sub = _lane_fold(jnp.maximum, sc[:, :LANES], sc[:, LANES:])
            mrun = jnp.maximum(mrun, m_sub if max_shift is None else m_sub + max_shift(jj))
        return mrun

    return lax.fori_loop(0, k_ref.shape[1] // tk, step, jnp.full((tq, LANES), -jnp.inf, F32))


def _value_pass(v_ref, s_ref, tk, m_of):
    tq, ts = s_ref.shape[1:]
    sub = tk // ts

    def step(j, carry):
        lsum, acc = carry
        ks = pl.multiple_of(j * tk, tk)
        p = []
        for c in range(sub):
            jj = j * sub + c
            m = m_of(jj)
            s = s_ref[jj]
            p += [jnp.exp2(s[:, i * LANES:(i + 1) * LANES] - m) for i in range(ts // LANES)]
        for pc in p:
            lsum = lsum + pc
        p = jnp.concatenate([pc.astype(BF16) for pc in p], axis=1)
        return lsum, acc + jnp.dot(p, v_ref[0, pl.ds(ks, tk), :], preferred_element_type=F32)

    init = (jnp.zeros((tq, LANES), F32), jnp.zeros((tq, v_ref.shape[2]), F32))
    lsum, acc = lax.fori_loop(0, v_ref.shape[1] // tk, step, init)
    return acc / jnp.sum(lsum, axis=-1, keepdims=True)


def _row_max(mrun):
    return jnp.broadcast_to(jnp.max(mrun, axis=-1, keepdims=True), mrun.shape)


def _flash_body(q_ref, k_ref, v_ref, o_ref, s_ref, *, tk):
    m = _row_max(_score_pass(q_ref[0], k_ref, s_ref, tk))
    o_ref[0] = _value_pass(v_ref, s_ref, tk, lambda jj: m).astype(o_ref.dtype)


def _flash(q, k, v, *, heads, q_w, kv_of_head, k_col0, v_col0, tq, tk):
    B, S, _ = q.shape
    dv = LANES
    return pl.pallas_call(
        functools.partial(_flash_body, tk=tk),
        grid=(B, heads, S // tq),
        in_specs=[pl.BlockSpec((1, tq, q_w), lambda b, h, i: (b, i, h)),
                  pl.BlockSpec((1, S, q_w), lambda b, h, i: (b, 0, k_col0 + kv_of_head(h))),
                  pl.BlockSpec((1, S, dv), lambda b, h, i: (b, 0, v_col0 + kv_of_head(h)))],
        out_specs=pl.BlockSpec((1, tq, dv), lambda b, h, i: (b, i, h)),
        out_shape=jax.ShapeDtypeStruct((B, S, heads * dv), BF16),
        scratch_shapes=[pltpu.VMEM((S // tq, tq, tq), F32)],
        compiler_params=_cparams("parallel", "parallel", "arbitrary"),
        name="flash_attn",
    )(q, k, v)


def _diff_body(lam_ref, cb_ref, q_ref, k_ref, v_ref, bias_ref, o_ref, s_ref, *, tk, out_scale):
    h, qi = pl.program_id(1), pl.program_id(2)
    n_sub = s_ref.shape[0]
    qf = q_ref[0].astype(F32) * (B_QK ** -0.5 * LOG2E)
    lane = lax.broadcasted_iota(jnp.int32, qf.shape, 1)
    c_lo, c_hi = cb_ref[0, h], cb_ref[1, h]
    far_shift = lambda jj, near: jnp.where(jj < qi - 1, c_lo, jnp.where(jj > qi + 1, c_hi, near))

    def one_map(q):
        mrun = _score_pass(q, k_ref, s_ref, tk, max_shift=lambda jj: far_shift(jj, -jnp.inf))
        for d in range(3):
            jj = qi + (d - 1)

            def add_bias(mrun, jj=jj, d=d):
                s = s_ref[jj] + bias_ref[0, d]
                s_ref[jj] = s
                return jnp.maximum(mrun, _lane_fold(jnp.maximum, s[:, :LANES], s[:, LANES:]))

            mrun = lax.cond(jnp.logical_and(jj >= 0, jj < n_sub), add_bias, lambda m_: m_, mrun)
        m = _row_max(mrun)
        return _value_pass(v_ref, s_ref, tk, lambda jj: m - far_shift(jj, 0.0))

    o1 = one_map(jnp.where(lane < B_QK, qf, 0.0).astype(BF16))
    o2 = one_map(jnp.where(lane >= B_QK, qf, 0.0).astype(BF16))
    o_ref[0] = (_rms(o1 - lam_ref[0] * o2) * out_scale).astype(o_ref.dtype)


def _rel_bucket(rel):
    half = REL_BUCKETS // 2
    max_exact = half // 2
    ret = (rel > 0).astype(jnp.int32) * half
    n = jnp.abs(rel)
    large = max_exact + (jnp.log(jnp.maximum(n, 1).astype(F32) / max_exact)
                         / math.log(REL_MAX_DIST / max_exact) * (half - max_exact)).astype(jnp.int32)
    large = jnp.minimum(large, half - 1)
    return ret + jnp.where(n < max_exact, n, large)


def _diff_bias_tables(rel_bias, tq):
    a = jnp.arange(tq)
    rel = (jnp.arange(-1, 2)[:, None, None] * tq + a[None, None, :]) - a[None, :, None]
    bucket = _rel_bucket(rel)
    rb = rel_bias.astype(F32) * LOG2E
    near = jnp.zeros((B_HEADS,) + rel.shape, F32)
    for b in range(REL_BUCKETS):
        near = jnp.where(bucket[None] == b, rb[b][:, None, None, None], near)
    far =jnp.stack([rel_bias[REL_BUCKETS // 2 - 1], rel_bias[REL_BUCKETS - 1]]).astype(F32) * LOG2E
    return near, far


def _diff_attn(zb, lam, near, far, *, tq, tk, out_scale):
    B, S, _ = zb.shape
    assert tq >= REL_MAX_DIST, "far key chunks must lie beyond the last distinct relative bucket"
    H = B_HEADS
    smem = pl.BlockSpec(memory_space=pltpu.SMEM)
    return pl.pallas_call(
        functools.partial(_diff_body, tk=tk, out_scale=out_scale),
        grid=(B, H, S // tq),
        in_specs=[smem, smem,
                  pl.BlockSpec((1, tq, B_V), lambda b, h, i: (b, i, h)),
                  pl.BlockSpec((1, S, B_V), lambda b, h, i: (b, 0, H + h)),
                  pl.BlockSpec((1, S, B_V), lambda b, h, i: (b, 0, 2 * H + h)),
                  pl.BlockSpec((1, 3, tq, tq), lambda b, h, i: (h, 0, 0, 0))],
        out_specs=pl.BlockSpec((1, tq, B_V), lambda b, h, i: (b, i, h)),
        out_shape=jax.ShapeDtypeStruct((B, S, H * B_V), BF16),
        scratch_shapes=[pltpu.VMEM((S // tq, tq, tq), F32)],
        compiler_params=_cparams("parallel", "parallel", "arbitrary"),
        name="diff_attn",
    )(lam, far, zb, zb, zb, near)


def _pool_body(u_ref, w_ref, sc_ref, o_ref, pad_ref, *, chunk):
    g = pl.program_id(1)
    S = u_ref.shape[1]
    pad_ref[0:HALO, :] = jnp.zeros((HALO, LANES), F32)
    pad_ref[HALO + S:, :] = jnp.zeros((HALO, LANES), F32)
    pad_ref[HALO:HALO + S, :] = u_ref[0].astype(F32)
    w_mat, sc = w_ref[0], sc_ref[...]

    def pooled(win):
        def body(c, _):
            r0 = pl.multiple_of(c * chunk, chunk)
            tot = pad_ref[pl.ds(r0 + HALO - win // 2, chunk), :]
            for j in range(1 - win // 2, win // 2):
                tot = tot + pad_ref[pl.ds(r0 + HALO + j, chunk), :]
            t = r0 + lax.broadcasted_iota(jnp.int32, (chunk, 1), 0)
            cnt = jnp.clip(t - win // 2 + win, 0, S) - jnp.clip(t - win // 2, 0, S)
            d = tot / cnt.astype(F32) - pad_ref[pl.ds(r0 + HALO, chunk), :]
            y = jnp.dot(d.astype(BF16), w_mat, preferred_element_type=F32) * sc
            o_ref[0, pl.ds(r0, chunk), :] = y.astype(o_ref.dtype)
            return 0
        lax.fori_loop(0, S // chunk, body, 0)

    for gi, win in enumerate(POOL_WINDOWS):
        pl.when(g == gi)(functools.partial(pooled, win))


def _pool(zc, w_pool, pool_scale, chunk):
    B, S, _ = zc.shape
    blk = pl.BlockSpec((1, S, LANES), lambda b, g: (b, 0, g))
    return pl.pallas_call(
        functools.partial(_pool_body, chunk=chunk),
        grid=(B, C_GROUPS),
        in_specs=[blk, pl.BlockSpec((1, C_GROUP_W, C_GROUP_W), lambda b, g: (g, 0, 0)),
                  pl.BlockSpec((1, LANES), lambda b, g: (0, g))],
        out_specs=blk,
        out_shape=jax.ShapeDtypeStruct((B, S, BRANCH_W), BF16),
        scratch_shapes=[pltpu.VMEM((S + 2 * HALO, LANES), F32)],
        compiler_params=_cparams("parallel", "arbitrary"),
        name="pool_mixer",
    )(zc, w_pool, pool_scale)


def _merge_body(h_ref, oa_ref, ob_ref, oc_ref, od_ref, x_ref, wg_ref, bg_ref, wl_ref, wo_ref, g2_ref,
                wr_hi_ref, wr_lo_ref, xn_ref, h2_ref, lg_ref, acc_ref):
    j = pl.program_id(1)

    @pl.when(j == 0)
    def _():
        acc_ref[...] = x_ref[...]

    h = h_ref[...]
    merged = None
    for b, o_ref in enumerate((oa_ref, ob_ref, oc_ref, od_ref)):
        gate = jax.nn.sigmoid(jnp.dot(h, wg_ref[b], preferred_element_type=F32) + bg_ref[b:b + 1, :])
        term = gate * jnp.dot(o_ref[...], wl_ref[b], preferred_element_type=F32)
        merged = term if merged is None else merged + term
    acc_ref[...] += jnp.dot(merged.astype(BF16), wo_ref[...], preferred_element_type=F32)

    @pl.when(j == pl.num_programs(1) - 1)
    def _():
        xn = acc_ref[...]
        xn_ref[...] = xn
        h2 = _rms(xn, g2_ref[...])
        h2_ref[...] = h2.astype(BF16)
        h_hi = h2.astype(BF16)
        h_lo = (h2 - h_hi.astype(F32)).astype(BF16)
        lg_ref[...] = (jnp.dot(h_hi, wr_hi_ref[...], preferred_element_type=F32)
                       + jnp.dot(h_hi, wr_lo_ref[...], preferred_element_type=F32)
                       + jnp.dot(h_lo, wr_hi_ref[...], preferred_element_type=F32))


def _merge(h, outs, x, wg, bg, wl, wo, g2, wr_hi, wr_lo, tm, tn):
    T, D = x.shape
    row = lambda w_: pl.BlockSpec((tm, w_), lambda i, j: (i, 0))
    return pl.pallas_call(
        _merge_body,
        grid=(T // tm, D // tn),
        in_specs=[row(D)] + [row(BRANCH_W)] * N_BRANCH + [
            row(D),
            pl.BlockSpec((N_BRANCH, D, tn), lambda i, j: (0, 0, j)),
            pl.BlockSpec((N_BRANCH, tn), lambda i, j: (0, j)),
            pl.BlockSpec((N_BRANCH, BRANCH_W, tn), lambda i, j: (0, 0, j)),
            pl.BlockSpec((tn, D), lambda i, j: (j, 0)),
            _resident((1, D)), _resident((D, ROUTER_W)), _resident((D, ROUTER_W))],
        out_specs=[row(D), row(D), row(ROUTER_W)],
        out_shape=[jax.ShapeDtypeStruct((T, D), F32), jax.ShapeDtypeStruct((T, D), BF16),
                   jax.ShapeDtypeStruct((T, ROUTER_W), F32)],
        scratch_shapes=[pltpu.VMEM((tm, D), F32)],
        compiler_params=_cparams("parallel", "arbitrary"),
        name="gated_merge",
    )(h, *outs, x, wg, bg, wl, wo, g2, wr_hi, wr_lo)


def _route(logits, b_rg, b_re):
    T = logits.shape[0]
    lg = logits[:, :N_GROUPS] + b_rg.astype(F32)
    pg = jax.nn.softmax(lg, axis=-1)
    g_top = jnp.argmax(lg, axis=-1)
    pg_top = jnp.take_along_axis(pg, g_top[:, None], axis=1)
    le = (logits[:, N_GROUPS:N_GROUPS + N_EXPERTS] + b_re.astype(F32)).reshape(T, N_GROUPS, EXP_PER_GROUP)
    le = jnp.take_along_axis(le, g_top[:, None, None], axis=1)[:, 0]
    pe = jax.nn.softmax(le, axis=-1)
    top_p, top_i = lax.top_k(pe, TOP_K)
    gates = pg_top * top_p / jnp.sum(top_p, axis=-1, keepdims=True)
    expert = g_top[:, None] * EXP_PER_GROUP + top_i
    return expert.astype(jnp.int32), gates


def _dispatch(expert, gates, tb):
    T = expert.shape[0]
    A = T * TOP_K
    flat_e = expert.reshape(A)
    iota = jnp.arange(A, dtype=jnp.int32)
    se, order, sw = lax.sort((flat_e, iota, gates.reshape(A)), num_keys=1, is_stable=True)
    counts = jnp.sum((flat_e[:, None] == jnp.arange(N_EXPERTS, dtype=jnp.int32)[None, :]).astype(jnp.int32), axis=0)
    start = jnp.cumsum(counts) - counts
    padded = (counts + tb - 1) // tb * tb
    pend = jnp.cumsum(padded)
    pstart = pend - padded
    n_blocks = -(-A // tb) + N_EXPERTS
    P = n_blocks * tb
    blk_first = jnp.arange(n_blocks, dtype=jnp.int32) * tb
    blk_exp = jnp.minimum(jnp.sum((pend[None, :] <= blk_first[:, None]).astype(jnp.int32), axis=1), N_EXPERTS - 1)
    n_valid = (pend[-1] // tb).astype(jnp.int32).reshape(1)
    per_slot = lambda v: jnp.repeat(v[blk_exp], tb)
    in_expert = jnp.arange(P, dtype=jnp.int32) - per_slot(pstart)
    valid = in_expert < per_slot(counts)
    src = jnp.where(valid, in_expert + per_slot(start), 0)
    slot_tok = jnp.where(valid, order[src] // TOP_K, 0)
    slot_w = jnp.where(valid, sw[src], 0.0)
    dest = (pstart - start)[se] + iota
    _, slot_of = lax.sort((order, dest), num_keys=1)
    return slot_tok, slot_w, blk_exp, n_valid, slot_of.reshape(T, TOP_K)


def _expert_body(be_ref, nv_ref, x_ref, sw_ref, wg_ref, wu_ref, wd_ref, o_ref, wg_s, wu_s, wd_s):
    i = pl.program_id(0)

    @pl.when(jnp.logical_or(i == 0, be_ref[i] != be_ref[jnp.maximum(i - 1, 0)]))
    def _():
        wg_s[...] = wg_ref[0, 0].astype(BF16)
        wu_s[...] = wu_ref[0, 0].astype(BF16)
        wd_s[...] = wd_ref[0, 0].astype(BF16)

    @pl.when(i < nv_ref[0])
    def _():
        x = x_ref[...]
        g = jnp.dot(x, wg_s[...], preferred_element_type=F32)
        u = jnp.dot(x, wu_s[...], preferred_element_type=F32)
        a = (g * jax.nn.sigmoid(g) * u).astype(BF16)
        y = jnp.dot(a, wd_s[...], preferred_element_type=F32)
        o_ref[...] = (y * sw_ref[...]).astype(o_ref.dtype)

    @pl.when(i >= nv_ref[0])
    def _():
        o_ref[...] = jnp.zeros(o_ref.shape, o_ref.dtype)


def _expert_ffn(xs, slot_w, blk_exp, n_valid, wg, wu, wd, layer, tb):
    P, D = xs.shape
    grid_spec = pltpu.PrefetchScalarGridSpec(
        num_scalar_prefetch=2,
        grid=(P // tb,),
        in_specs=[pl.BlockSpec((tb, D), lambda i, be, nv: (i, 0)),
                  pl.BlockSpec((tb, 1), lambda i, be, nv: (i, 0)),
                  pl.BlockSpec((1, 1, D, D_EXPERT), lambda i, be, nv: (layer, be[i], 0, 0)),
                  pl.BlockSpec((1, 1, D, D_EXPERT), lambda i, be, nv: (layer, be[i], 0, 0)),
                  pl.BlockSpec((1, 1, D_EXPERT, D), lambda i, be, nv: (layer, be[i], 0, 0))],
        out_specs=pl.BlockSpec((tb, D), lambda i, be, nv: (i, 0)),
        scratch_shapes=[pltpu.VMEM((D, D_EXPERT), BF16), pltpu.VMEM((D, D_EXPERT), BF16),
                        pltpu.VMEM((D_EXPERT, D), BF16)],
    )
    return pl.pallas_call(
        _expert_body,
        grid_spec=grid_spec,
        out_shape=jax.ShapeDtypeStruct((P, D), BF16),
        compiler_params=_cparams("arbitrary"),
        name="expert_ffn",
    )(blk_exp, n_valid, xs, slot_w.reshape(P, 1), wg, wu, wd)


def _final_norm_body(x_ref, g_ref, o_ref):
    o_ref[...] = _rms(x_ref[...], g_ref[...])


def _final_norm(x, g, row0, rows, tm):
    D = x.shape[1]
    first = row0 // tm
    return pl.pallas_call(
        _final_norm_body, grid=(rows // tm,),
        in_specs=[pl.BlockSpec((tm, D), lambda i: (first + i, 0)), _resident((1, D))],
        out_specs=pl.BlockSpec((tm, D), lambda i: (i, 0)),
        out_shape=jax.ShapeDtypeStruct((rows, D), F32), compiler_params=_cparams("parallel"), name="final_norm",
    )(x, g)


def _rope_tables(seq):
    def angles(pos, dim):
        inv = 1.0 / (ROPE_THETA ** (jnp.arange(0, dim, 2, dtype=F32) / dim))
        ang = pos.astype(F32)[:, None] * inv[None, :]
        return jnp.cos(ang), jnp.sin(ang)
    pos = jnp.arange(seq)
    c1, s1 = angles(pos, A_ROPE)
    cr, sr = angles(pos // GRID_W, D_HEAD // 2)
    cc, sc = angles(pos % GRID_W, D_HEAD // 2)
    pad1, pad0 = jnp.ones((seq, ROPE_PAD), F32), jnp.zeros((seq, ROPE_PAD), F32)
    return (jnp.concatenate([c1, c1, pad1], 1), jnp.concatenate([-s1, s1, pad0], 1),
            jnp.concatenate([cr, cr, cc, cc], 1), jnp.concatenate([-sr, sr, -sc, sc], 1))


def _layer_weights(l, w_in, w_uq, w_ukv, w_route_group, w_route_expert):
    D = D_MODEL
    k_r_end = A_Q_LORA + A_KV_LORA + A_ROPE
    win = jnp.concatenate([w_in[l][:, :k_r_end], jnp.zeros((D, ROPE_PAD), F32), w_in[l][:, k_r_end:]], 1)
    uq = w_uq[l].reshape(A_Q_LORA, A_HEADS, A_NOPE + A_ROPE)
    uq_rope = jnp.pad(uq[:, :, A_NOPE:], ((0, 0), (0, 0), (0, ROPE_PAD)))
    wuq = jnp.concatenate([uq[:, :, :A_NOPE].reshape(A_Q_LORA, -1), uq_rope.reshape(A_Q_LORA, -1)], 1)
    ukv = w_ukv[l].reshape(A_KV_LORA, A_HEADS, A_NOPE + A_V)
    wukv = jnp.concatenate([ukv[:, :, :A_NOPE].reshape(A_KV_LORA, -1), ukv[:, :, A_NOPE:].reshape(A_KV_LORA, -1)], 1)
    wr = jnp.concatenate([w_route_group[l], w_route_expert[l],
                          jnp.zeros((D, ROUTER_W - N_GROUPS - N_EXPERTS), F32)], 1)
    wr_hi = wr.astype(BF16)
    wr_lo = (wr - wr_hi.astype(F32)).astype(BF16)
    return win.astype(BF16), wuq.astype(BF16), wukv.astype(BF16), wr_hi, wr_lo


def kernel(x_prompt, x_sample, norm1_g, w_in, q_norm_g, kv_norm_g, w_uq, w_ukv, lam_q1, lam_k1, lam_q2, lam_k2,
           rel_bias, w_pool, pool_scale, qk_norm_q, qk_norm_k, w_lift, w_gate, b_gate, w_out, norm2_g,
           w_route_group, b_route_group, w_route_expert, b_route_expert, w_exp_gate, w_exp_up, w_exp_down,
           final_g):
    assert x_prompt.shape[1:] == x_sample.shape[1:], "both request groups must share (seq, d_model)"
    n_prompt = x_prompt.shape[0]
    x = jnp.concatenate([x_prompt, x_sample], axis=0)
    B, S, D = x.shape
    T = B * S
    t = _tiles(S, T)
    x = x.reshape(T, D)
    depth = w_in.shape[0]
    tabs = _rope_tables(S)
    near, far = _diff_bias_tables(rel_bias, t["tq"])
    row2 = lambda v: v.reshape(1, -1).astype(F32)

    for l in range(depth):
        lam_init = 0.8 - 0.6 * math.exp(-0.3 * l)
        lam = (jnp.exp(jnp.sum(lam_q1[l] * lam_k1[l])) - jnp.exp(jnp.sum(lam_q2[l] * lam_k2[l])) + lam_init)
        win, wuq, wukv, wr_hi, wr_lo = _layer_weights(l, w_in, w_uq, w_ukv, w_route_group, w_route_expert)

        h, za, zb, zc, zd = _norm_inproj(x, row2(norm1_g[l]), win, t["tm"])
        qa, ka, va, qd, kd = _prep(za, zd, row2(q_norm_g[l]), row2(kv_norm_g[l]), wuq, wukv,
                                   row2(qk_norm_q[l]), row2(qk_norm_k[l]), tabs, S, t["tm"])
        seq3 = lambda a: a.reshape(B, S, a.shape[-1])
        o_a = _flash(seq3(qa), seq3(ka), seq3(va), heads=A_HEADS, q_w=A_QK_W, kv_of_head=lambda hh: hh,
                     k_col0=0, v_col0=0, tq=t["tq"], tk=t["tk"])
        o_b =_diff_attn(seq3(zb), lam.reshape(1).astype(F32), near, far, tq=t["tq"], tk=t["tk"],
                         out_scale=1.0 - lam_init)
        o_c = _pool(seq3(zc), w_pool[l].astype(BF16), row2(pool_scale[l]), t["pool_chunk"])
        rep = D_HEADS // D_KV_HEADS
        o_d = _flash(seq3(qd), seq3(kd), seq3(zd), heads=D_HEADS, q_w=D_HEAD, kv_of_head=lambda hh: hh // rep,
                     k_col0=0, v_col0=D_HEADS + D_KV_HEADS, tq=t["tq"], tk=t["tk"])
        outs = [o.reshape(T, BRANCH_W) for o in (o_a, o_b, o_c, o_d)]
        xn, h2, logits = _merge(h, outs, x, w_gate[l].astype(BF16), b_gate[l].astype(F32), w_lift[l].astype(BF16),
                                w_out[l].astype(BF16), row2(norm2_g[l]), wr_hi, wr_lo, t["tm"], t["tn"])

        expert, gates = _route(logits, b_route_group[l], b_route_expert[l])
        slot_tok, slot_w, blk_exp, n_valid, slot_of = _dispatch(expert, gates, t["tb"])
        ys = _expert_ffn(h2[slot_tok], slot_w, blk_exp, n_valid, w_exp_gate, w_exp_up, w_exp_down, l, t["tb"])
        x = xn + ys[slot_of[:, 0]].astype(F32) + ys[slot_of[:, 1]].astype(F32)

    t_prompt = n_prompt * S
    y_prompt = _final_norm(x, row2(final_g), 0, t_prompt, t["tm"]).reshape(n_prompt, S, D)
    y_sample = _final_norm(x, row2(final_g), t_prompt, T - t_prompt, t["tm"]).reshape(B - n_prompt, S, D)
    return y_prompt, y_sample
```

```python
import functools
import math

import jax
import jax.numpy as jnp
from jax import lax
from jax.experimental import pallas as pl
from jax.experimental.pallas import tpu as pltpu

F32 = jnp.float32
BF16 = jnp.bfloat16
LOG2E = 1.4426950408889634

D_MODEL = 2048
GRID_W = 64
BRANCH_W = 512
N_BRANCH = 4
ROPE_THETA = 10000.0
EPS = 1e-6
A_HEADS, A_Q_LORA, A_KV_LORA, A_NOPE, A_ROPE, A_V = 4, 384, 128, 128, 64, 128
B_HEADS, B_QK, B_V = 4, 64, 128
REL_BUCKETS, REL_MAX_DIST = 32, 128
POOL_WINDOWS = (2, 4, 8, 16)
C_GROUPS, C_GROUP_W = 4, 128
D_HEADS, D_KV_HEADS, D_HEAD = 4, 2, 128
N_GROUPS, EXP_PER_GROUP, N_EXPERTS, TOP_K, D_EXPERT = 4, 8, 32, 2, 512

LANES = 128
ROPE_PAD = LANES - A_ROPE
ZA_W = A_Q_LORA + A_KV_LORA + LANES
ZB_W = 3 * B_HEADS * B_V
ZC_W = BRANCH_W
ZD_W = (D_HEADS + 2 * D_KV_HEADS) * D_HEAD
A_QK_W = 2 * LANES
ROUTER_W = LANES
HALO = 16
VMEM_LIMIT = 56 * 2 ** 20


def _tiles(seq, tokens):
    def pick(n, pref):
        t = min(pref, n)
        while n % t:
            t //= 2
        return t
    return dict(
        tm=pick(seq, 512),
        tq=pick(seq, 512),
        rows=pick(seq, 1024),
        tk=pick(seq, 4096),
        tn=256,
        tb=256,
        pool_chunk=pick(seq, 1024),
    )


def _cparams(*sem):
    return pltpu.CompilerParams(dimension_semantics=sem, vmem_limit_bytes=VMEM_LIMIT)


def _resident(shape):
    zeros = (0,) * len(shape)
    return pl.BlockSpec(shape, lambda *_: zeros, pipeline_mode=pl.Buffered(1))


def _rms(x, g=None):
    y = x * lax.rsqrt(jnp.mean(x * x, axis=-1, keepdims=True) + EPS)
    return y if g is None else y * g


def _rope_lanes(x, c, s):
    lane = lax.broadcasted_iota(jnp.int32, x.shape, 1)
    first_half = (lane & 32) == 0
    partner = jnp.where(first_half, pltpu.roll(x, 96, 1), pltpu.roll(x, 32, 1))
    return x * c + partner * s


def _residual_sum(x_ref, y_refs):
    x = x_ref[...]
    for y_ref in y_refs:
        x = x + y_ref[...].astype(F32)
    return x


def _norm_inproj_body(*refs, n_add):
    x_ref, y_refs = refs[0], refs[1:1 + n_add]
    g_ref, w_ref = refs[1 + n_add:3 + n_add]
    outs = refs[3 + n_add:]
    x = _residual_sum(x_ref, y_refs)
    if n_add:
        outs[0][...] = x
        outs = outs[1:]
    h_ref, z_refs = outs[0], outs[1:]
    h = _rms(x, g_ref[...]).astype(BF16)
    h_ref[...] = h
    off = 0
    for z_ref in z_refs:
        w = z_ref.shape[1]
        z_ref[...] = jnp.dot(h, w_ref[:, off:off + w], preferred_element_type=F32).astype(BF16)
        off += w


def _norm_inproj(x, adds, g, w, tm):
    T, D = x.shape
    widths = (ZA_W, ZB_W, ZC_W, ZD_W)
    row = lambda w_: pl.BlockSpec((tm, w_), lambda i: (i, 0))
    sum_spec, sum_shape = ([row(D)], [jax.ShapeDtypeStruct((T, D), F32)]) if adds else ([], [])
    outs = pl.pallas_call(
        functools.partial(_norm_inproj_body, n_add=len(adds)),
        grid=(T // tm,),
        in_specs=[row(D)] * (1 + len(adds)) + [_resident((1, D)), _resident((D, sum(widths)))],
        out_specs=sum_spec + [row(D)] + [row(w_) for w_ in widths],
        out_shape=sum_shape + [jax.ShapeDtypeStruct((T, D), BF16)]
        + [jax.ShapeDtypeStruct((T, w_), BF16) for w_ in widths],
        compiler_params=_cparams("parallel"),
        name="norm_inproj",
    )(x, *adds, g, w)
    return outs if adds else [x] + list(outs)


def _prep_body(za_ref, zd_ref, gq_ref, gkv_ref, wuq_ref, wukv_ref, gdq_ref, gdk_ref,
               ca_ref, sa_ref, cd_ref, sd_ref, qa_ref, ka_ref, va_ref, qd_ref, kd_ref):
    a_scale = (A_NOPE + A_ROPE) ** -0.5 * LOG2E
    d_scale = D_HEAD ** -0.5 * LOG2E
    ca, sa, cd, sd = ca_ref[...], sa_ref[...], cd_ref[...], sd_ref[...]
    cq = _rms(za_ref[:, :A_Q_LORA].astype(F32), gq_ref[...]).astype(BF16)
    q = jnp.dot(cq, wuq_ref[...], preferred_element_type=F32)
    ckv = _rms(za_ref[:, A_Q_LORA:A_Q_LORA + A_KV_LORA].astype(F32), gkv_ref[...]).astype(BF16)
    kv = jnp.dot(ckv, wukv_ref[...], preferred_element_type=F32)
    k_rope = _rope_lanes(za_ref[:, A_Q_LORA + A_KV_LORA:].astype(F32), ca, sa).astype(BF16)
    nope_w = A_HEADS * A_NOPE
    for h in range(A_HEADS):
        lo = h * A_QK_W
        qa_ref[:, lo:lo + LANES] = (q[:, h * LANES:(h + 1) * LANES] * a_scale).astype(BF16)
        q_rope = _rope_lanes(q[:, nope_w + h * LANES:nope_w + (h + 1) * LANES], ca, sa)
        qa_ref[:, lo + LANES:lo + 2 * LANES] = (q_rope * a_scale).astype(BF16)
        ka_ref[:, lo:lo + LANES] = kv[:, h * LANES:(h + 1) * LANES].astype(BF16)
        ka_ref[:, lo + LANES:lo + 2 * LANES] = k_rope
    va_ref[...] = kv[:, nope_w:].astype(BF16)
    for h in range(D_HEADS):
        xh = _rms(zd_ref[:, h * D_HEAD:(h + 1) * D_HEAD].astype(F32), gdq_ref[...])
        qd_ref[:, h * D_HEAD:(h + 1) * D_HEAD] = (_rope_lanes(xh, cd, sd) * d_scale).astype(BF16)
    k_off = D_HEADS * D_HEAD
    for h in range(D_KV_HEADS):
        xh = _rms(zd_ref[:, k_off + h * D_HEAD:k_off + (h + 1) * D_HEAD].astype(F32), gdk_ref[...])
        kd_ref[:, h * D_HEAD:(h + 1) * D_HEAD] = _rope_lanes(xh, cd, sd).astype(BF16)


def _prep(za, zd, gq, gkv, wuq, wukv, gdq, gdk, tabs, seq, tm):
    T = za.shape[0]
    per_seq = seq // tm
    row = lambda w_: pl.BlockSpec((tm, w_), lambda i: (i, 0))
    tab = pl.BlockSpec((tm, LANES), lambda i: (i % per_seq, 0))
    out_w = (A_HEADS * A_QK_W, A_HEADS * A_QK_W, A_HEADS * A_V, D_HEADS * D_HEAD, D_KV_HEADS * D_HEAD)
    return pl.pallas_call(
        _prep_body,
        grid=(T // tm,),
        in_specs=[row(ZA_W), row(ZD_W), _resident(gq.shape), _resident(gkv.shape), _resident(wuq.shape),
                  _resident(wukv.shape), _resident(gdq.shape), _resident(gdk.shape), tab, tab, tab, tab],
        out_specs=[row(w_) for w_ in out_w],
        out_shape=[jax.ShapeDtypeStruct((T, w_), BF16) for w_ in out_w],
        compiler_params=_cparams("parallel"),
        name="mixer_prep",
    )(za, zd, gq, gkv, wuq, wukv, gdq, gdk, *tabs)


def _qk(q, k):
    return lax.dot_general(q, k, (((1,), (1,)), ((), ())), preferred_element_type=F32)


def _lane_fold(fn, acc, x):
    for c in range(x.shape[1] // LANES):
        acc = fn(acc, x[:, c * LANES:(c + 1) * LANES])
    return acc


def _score_pass(q, k_ref, s_ref, tk, max_shift=None):
    tq, ts = s_ref.shape[1:]
    sub = tk // ts

    def step(j, mrun):
        ks = pl.multiple_of(j * tk, tk)
        s = _qk(q, k_ref[0, pl.ds(ks, tk), :])
        for c in range(sub):
            jj = j * sub + c
            sc = s[:, c * ts:(c + 1) * ts]
            s_ref[jj] = sc
            m_sub = _lane_fold(jnp.maximum, sc[:, :LANES], sc[:, LANES:])
            mrun = jnp.maximum(mrun, m_sub if max_shift is None else m_sub + max_shift(jj))
        return mrun

    return lax.fori_loop(0, k_ref.shape[1] // tk, step, jnp.full((tq, LANES), -jnp.inf, F32))


def _value_pass(v_ref, s_ref, tk, m_of):
    tq, ts = s_ref.shape[1:]
    sub = tk // ts
    dv = v_ref.shape[2]
    ones = jnp.ones((tk, LANES), BF16)

    def step(j, acc):
        ks = pl.multiple_of(j * tk, tk)
        p = []
        for c in range(sub):
            jj = j * sub + c
            m = m_of(jj)
            s = s_ref[jj]
            p += [jnp.exp2((s[:, i * LANES:(i + 1) * LANES] - m).astype(BF16)) for i in range(ts // LANES)]
        v_ext = jnp.concatenate([v_ref[0, pl.ds(ks, tk), :], ones], axis=1)
        return acc + jnp.dot(jnp.concatenate(p, axis=1), v_ext, preferred_element_type=F32)

    acc = lax.fori_loop(0, v_ref.shape[1] // tk, step, jnp.zeros((tq, dv + LANES), F32))
    return acc[:, :dv] / acc[:, dv:]


def _row_max(mrun):
    return jnp.broadcast_to(jnp.max(mrun, axis=-1, keepdims=True), mrun.shape)


def _flash_body(q_ref, k_ref, v_ref, o_ref, s_ref, *, tk, stack):
    tq = q_ref.shape[1]
    q_w = q_ref.shape[2] // stack
    q = jnp.concatenate([q_ref[0, :, c * q_w:(c + 1) * q_w] for c in range(stack)], axis=0)
    m = _row_max(_score_pass(q, k_ref, s_ref, tk))
    o = _value_pass(v_ref, s_ref, tk, lambda jj: m).astype(o_ref.dtype)
    o_ref[0] = jnp.concatenate([o[c * tq:(c + 1) * tq] for c in range(stack)], axis=1)


def _flash(q, k, v, *, kv_heads, stack, q_w, k_col0, v_col0, tq, ts, tk):
    B, S, _ = q.shape
    dv = LANES
    return pl.pallas_call(
        functools.partial(_flash_body, tk=tk, stack=stack),
        grid=(B, kv_heads, S // tq),
        in_specs=[pl.BlockSpec((1, tq, stack * q_w), lambda b, h, i: (b, i, h)),
                  pl.BlockSpec((1, S, q_w), lambda b, h, i: (b, 0, k_col0 + h)),
                  pl.BlockSpec((1, S, dv), lambda b, h, i: (b, 0, v_col0 + h))],
        out_specs=pl.BlockSpec((1, tq, stack * dv), lambda b, h, i: (b, i, h)),
        out_shape=jax.ShapeDtypeStruct((B, S, kv_heads * stack * dv), BF16),
        scratch_shapes=[pltpu.VMEM((S // ts, stack * tq, ts), F32)],
        compiler_params=_cparams("parallel", "parallel", "arbitrary"),
        name="flash_attn",
    )(q, k, v)


def _diff_body(lam_ref, cb_ref, q_ref, k_ref, v_ref, bias_ref, o_ref, s_ref, *, tk, out_scale):
    h, qi = pl.program_id(1), pl.program_id(2)
    n_sub = s_ref.shape[0]
    tq = q_ref.shape[1]
    qf = q_ref[0].astype(F32) * (B_QK ** -0.5 * LOG2E)
    lane = lax.broadcasted_iota(jnp.int32, qf.shape, 1)
    c_lo, c_hi = cb_ref[0, h], cb_ref[1, h]
    far_shift = lambda jj, near: jnp.where(jj < qi - 1, c_lo, jnp.where(jj > qi + 1, c_hi, near))

    q = jnp.concatenate([jnp.where(lane < B_QK, qf, 0.0), jnp.where(lane >= B_QK, qf, 0.0)], axis=0).astype(BF16)
    mrun = _score_pass(q, k_ref, s_ref, tk, max_shift=lambda jj: far_shift(jj, -jnp.inf))
    for d in range(3):
        jj = qi + (d - 1)

        def add_bias(mrun, jj=jj, d=d):
            bias = bias_ref[0, d]
            s = s_ref[jj] + jnp.concatenate([bias, bias], axis=0)
            s_ref[jj] = s
            return jnp.maximum(mrun, _lane_fold(jnp.maximum, s[:, :LANES], s[:, LANES:]))

        mrun = lax.cond(jnp.logical_and(jj >= 0, jj < n_sub), add_bias, lambda m_: m_, mrun)
    m = _row_max(mrun)
    o = _value_pass(v_ref, s_ref, tk, lambda jj: m - far_shift(jj, 0.0))
    o = o[:tq] - lam_ref[0] * o[tq:]
    o_ref[0] = (_rms(o) * out_scale).astype(o_ref.dtype)


def _rel_bucket(rel):
    half = REL_BUCKETS // 2
    max_exact = half // 2
    ret = (rel > 0).astype(jnp.int32) * half
    n = jnp.abs(rel)
    large = max_exact + (jnp.log(jnp.maximum(n, 1).astype(F32) / max_exact)
                         / math.log(REL_MAX_DIST / max_exact) * (half - max_exact)).astype(jnp.int32)
    large = jnp.minimum(large, half - 1)
    return ret + jnp.where(n < max_exact, n, large)


def _diff_bias_tables(rel_bias, tq):
    a = jnp.arange(tq)
    rel = (jnp.arange(-1, 2)[:, None, None] * tq + a[None, None, :]) - a[None, :, None]
    bucket = _rel_bucket(rel)
    rb = rel_bias.astype(F32) * LOG2E
    near = jnp.zeros((B_HEADS,) + rel.shape, F32)
    for b in range(REL_BUCKETS):
        near = jnp.where(bucket[None] == b, rb[b][:, None, None, None], near)
    far =jnp.stack([rel_bias[REL_BUCKETS // 2 - 1], rel_bias[REL_BUCKETS - 1]]).astype(F32) * LOG2E
    return near, far


def _diff_attn(zb, lam, near, far, *, tq, tk, out_scale):
    B, S, _ = zb.shape
    assert tq >= REL_MAX_DIST, "far key chunks must lie beyond the last distinct relative bucket"
    H = B_HEADS
    smem = pl.BlockSpec(memory_space=pltpu.SMEM)
    return pl.pallas_call(
        functools.partial(_diff_body, tk=tk, out_scale=out_scale),
        grid=(B, H, S // tq),
        in_specs=[smem, smem,
                  pl.BlockSpec((1, tq, B_V), lambda b, h, i: (b, i, h)),
                  pl.BlockSpec((1, S, B_V), lambda b, h, i: (b, 0, H + h)),
                  pl.BlockSpec((1, S, B_V), lambda b, h, i: (b, 0, 2 * H + h)),
                  pl.BlockSpec((1, 3, tq, tq), lambda b, h, i: (h, 0, 0, 0))],
        out_specs=pl.BlockSpec((1, tq, B_V), lambda b, h, i: (b, i, h)),
        out_shape=jax.ShapeDtypeStruct((B, S, H * B_V), BF16),
        scratch_shapes=[pltpu.VMEM((S // tq, 2 * tq, tq), F32)],
        compiler_params=_cparams("parallel", "parallel", "arbitrary"),
        name="diff_attn",
    )(lam, far, zb, zb, zb, near)


def _pool_body(u_ref, w_ref, sc_ref, o_ref, pad_ref, *, chunk):
    g = pl.program_id(1)
    S = u_ref.shape[1]
    pad_ref[0:HALO, :] = jnp.zeros((HALO, LANES), F32)
    pad_ref[HALO + S:, :] = jnp.zeros((HALO, LANES), F32)
    pad_ref[HALO:HALO + S, :] = u_ref[0].astype(F32)
    w_mat, sc = w_ref[0], sc_ref[...]

    def pooled(win):
        def body(c, _):
            r0 = pl.multiple_of(c * chunk, chunk)
            tot = pad_ref[pl.ds(r0 + HALO - win // 2, chunk), :]
            for j in range(1 - win // 2, win // 2):
                tot = tot + pad_ref[pl.ds(r0 + HALO + j, chunk), :]
            t = r0 + lax.broadcasted_iota(jnp.int32, (chunk, 1), 0)
            cnt = jnp.clip(t - win // 2 + win, 0, S) - jnp.clip(t - win // 2, 0, S)
            d = tot / cnt.astype(F32) - pad_ref[pl.ds(r0 + HALO, chunk), :]
            y = jnp.dot(d.astype(BF16), w_mat, preferred_element_type=F32) * sc
            o_ref[0, pl.ds(r0, chunk), :] = y.astype(o_ref.dtype)
            return 0
        lax.fori_loop(0, S // chunk, body, 0)

    for gi, win in enumerate(POOL_WINDOWS):
        pl.when(g == gi)(functools.partial(pooled, win))


def _pool(zc, w_pool, pool_scale, chunk):
    B, S, _ = zc.shape
    blk = pl.BlockSpec((1, S, LANES), lambda b, g: (b, 0, g))
    return pl.pallas_call(
        functools.partial(_pool_body, chunk=chunk),
        grid=(B, C_GROUPS),
        in_specs=[blk, pl.BlockSpec((1, C_GROUP_W, C_GROUP_W), lambda b, g: (g, 0, 0)),
                  pl.BlockSpec((1, LANES), lambda b, g: (0, g))],
        out_specs=blk,
        out_shape=jax.ShapeDtypeStruct((B, S, BRANCH_W), BF16),
        scratch_shapes=[pltpu.VMEM((S + 2 * HALO, LANES), F32)],
        compiler_params=_cparams("parallel", "arbitrary"),
        name="pool_mixer",
    )(zc, w_pool, pool_scale)


def _merge_body(h_ref, oa_ref, ob_ref, oc_ref, od_ref, x_ref, wg_ref, bg_ref, wl_ref, wo_ref, g2_ref,
                wr_hi_ref, wr_lo_ref, xn_ref, h2_ref, lg_ref, acc_ref):
    j = pl.program_id(1)

    @pl.when(j == 0)
    def _():
        acc_ref[...] = x_ref[...]

    h = h_ref[...]
    merged = None
    for b, o_ref in enumerate((oa_ref, ob_ref, oc_ref, od_ref)):
        gate = jax.nn.sigmoid(jnp.dot(h, wg_ref[b], preferred_element_type=F32) + bg_ref[b:b + 1, :])
        term = gate * jnp.dot(o_ref[...], wl_ref[b], preferred_element_type=F32)
        merged = term if merged is None else merged + term
    acc_ref[...] += jnp.dot(merged.astype(BF16), wo_ref[...], preferred_element_type=F32)

    @pl.when(j == pl.num_programs(1) - 1)
    def _():
        xn = acc_ref[...]
        xn_ref[...] = xn
        h2 = _rms(xn, g2_ref[...])
        h2_ref[...] = h2.astype(BF16)
        h_hi = h2.astype(BF16)
        h_lo = (h2 - h_hi.astype(F32)).astype(BF16)
        lg_ref[...] = (jnp.dot(h_hi, wr_hi_ref[...], preferred_element_type=F32)
                       + jnp.dot(h_hi, wr_lo_ref[...], preferred_element_type=F32)
                       + jnp.dot(h_lo, wr_hi_ref[...], preferred_element_type=F32))


def _merge(h, outs, x, wg, bg, wl, wo, g2, wr_hi, wr_lo, tm, tn):
    T, D = x.shape
    row = lambda w_: pl.BlockSpec((tm, w_), lambda i, j: (i, 0))
    return pl.pallas_call(
        _merge_body,
        grid=(T // tm, D // tn),
        in_specs=[row(D)] + [row(BRANCH_W)] * N_BRANCH + [
            row(D),
            pl.BlockSpec((N_BRANCH, D, tn), lambda i, j: (0, 0, j)),
            pl.BlockSpec((N_BRANCH, tn), lambda i, j: (0, j)),
            pl.BlockSpec((N_BRANCH, BRANCH_W, tn), lambda i, j: (0, 0, j)),
            pl.BlockSpec((tn, D), lambda i, j: (j, 0)),
            _resident((1, D)), _resident((D, ROUTER_W)), _resident((D, ROUTER_W))],
        out_specs=[row(D), row(D), row(ROUTER_W)],
        out_shape=[jax.ShapeDtypeStruct((T, D), F32), jax.ShapeDtypeStruct((T, D), BF16),
                   jax.ShapeDtypeStruct((T, ROUTER_W), F32)],
        scratch_shapes=[pltpu.VMEM((tm, D), F32)],
        compiler_params=_cparams("parallel", "arbitrary"),
        name="gated_merge",
    )(h, *outs, x, wg, bg, wl, wo, g2, wr_hi, wr_lo)


def _route(logits, b_rg, b_re):
    T = logits.shape[0]
    lg = logits[:, :N_GROUPS] + b_rg.astype(F32)
    pg = jax.nn.softmax(lg, axis=-1)
    g_top = jnp.argmax(lg, axis=-1)
    pg_top = jnp.take_along_axis(pg, g_top[:, None], axis=1)
    le = (logits[:, N_GROUPS:N_GROUPS + N_EXPERTS] + b_re.astype(F32)).reshape(T, N_GROUPS, EXP_PER_GROUP)
    le = jnp.take_along_axis(le, g_top[:, None, None], axis=1)[:, 0]
    pe = jax.nn.softmax(le, axis=-1)
    top_p, top_i = lax.top_k(pe, TOP_K)
    gates = pg_top * top_p / jnp.sum(top_p, axis=-1, keepdims=True)
    expert = g_top[:, None] * EXP_PER_GROUP + top_i
    return expert.astype(jnp.int32), gates


def _dispatch(expert, gates, tb):
    T = expert.shape[0]
    A = T * TOP_K
    flat_e = expert.reshape(A)
    iota = jnp.arange(A, dtype=jnp.int32)
    se, order, sw = lax.sort((flat_e, iota, gates.reshape(A)), num_keys=1, is_stable=True)
    counts = jnp.sum((flat_e[:, None] == jnp.arange(N_EXPERTS, dtype=jnp.int32)[None, :]).astype(jnp.int32), axis=0)
    start = jnp.cumsum(counts) - counts
    padded = (counts + tb - 1) // tb * tb
    pend = jnp.cumsum(padded)
    pstart = pend - padded
    n_blocks = -(-A // tb) + N_EXPERTS
    P = n_blocks * tb
    blk_first = jnp.arange(n_blocks, dtype=jnp.int32) * tb
    blk_exp = jnp.minimum(jnp.sum((pend[None, :] <= blk_first[:, None]).astype(jnp.int32), axis=1), N_EXPERTS - 1)
    n_valid = (pend[-1] // tb).astype(jnp.int32).reshape(1)
    per_slot = lambda v: jnp.repeat(v[blk_exp], tb)
    slot = jnp.arange(P, dtype=jnp.int32)
    in_expert = slot - per_slot(pstart)
    valid = in_expert < per_slot(counts)
    src = jnp.where(valid, in_expert + per_slot(start), slot % A)
    slot_tok = order[src] // TOP_K
    slot_w = jnp.where(valid, sw[src], 0.0)
    dest = (pstart - start)[se] + iota
    _, slot_of = lax.sort((order, dest), num_keys=1)
    return slot_tok, slot_w, blk_exp, n_valid, slot_of.reshape(T, TOP_K)


def _expert_body(be_ref, nv_ref, x_ref, sw_ref, wg_ref, wu_ref, wd_ref, o_ref, wg_s, wu_s, wd_s):
    i = pl.program_id(0)

    @pl.when(jnp.logical_or(i == 0, be_ref[i] != be_ref[jnp.maximum(i - 1, 0)]))
    def _():
        wg_s[...] = wg_ref[0, 0].astype(BF16)
        wu_s[...] = wu_ref[0, 0].astype(BF16)
        wd_s[...] = wd_ref[0, 0].astype(BF16)

    @pl.when(i < nv_ref[0])
    def _():
        x = x_ref[...]
        g = jnp.dot(x, wg_s[...], preferred_element_type=F32)
        u = jnp.dot(x, wu_s[...], preferred_element_type=F32)
        a = (g * jax.nn.sigmoid(g) * u).astype(BF16)
        y = jnp.dot(a, wd_s[...], preferred_element_type=F32)
        o_ref[...] = (y * sw_ref[...]).astype(o_ref.dtype)

    @pl.when(i >= nv_ref[0])
    def _():
        o_ref[...] = jnp.zeros(o_ref.shape, o_ref.dtype)


def _expert_ffn(xs, slot_w, blk_exp, n_valid, wg, wu, wd, layer, tb):
    P, D = xs.shape
    grid_spec = pltpu.PrefetchScalarGridSpec(
        num_scalar_prefetch=2,
        grid=(P // tb,),
        in_specs=[pl.BlockSpec((tb, D), lambda i, be, nv: (i, 0)),
                  pl.BlockSpec((tb, 1), lambda i, be, nv: (i, 0)),
                  pl.BlockSpec((1, 1, D, D_EXPERT), lambda i, be, nv: (layer, be[i], 0, 0)),
                  pl.BlockSpec((1, 1, D, D_EXPERT), lambda i, be, nv: (layer, be[i], 0, 0)),
                  pl.BlockSpec((1, 1, D_EXPERT, D), lambda i, be, nv: (layer, be[i], 0, 0))],
        out_specs=pl.BlockSpec((tb, D), lambda i, be, nv: (i, 0)),
        scratch_shapes=[pltpu.VMEM((D, D_EXPERT), BF16), pltpu.VMEM((D, D_EXPERT), BF16),
                        pltpu.VMEM((D_EXPERT, D), BF16)],
    )
    return pl.pallas_call(
        _expert_body,
        grid_spec=grid_spec,
        out_shape=jax.ShapeDtypeStruct((P, D), BF16),
        compiler_params=_cparams("arbitrary"),
        name="expert_ffn",
    )(blk_exp, n_valid, xs, slot_w.reshape(P, 1), wg, wu, wd)


def _final_norm_body(*refs):
    x_ref, y_refs, (g_ref, o_ref) = refs[0], refs[1:-2], refs[-2:]
    o_ref[...] = _rms(_residual_sum(x_ref, y_refs), g_ref[...])


def _final_norm(x, adds, g, row0, rows, tm):
    D = x.shape[1]
    first = row0 // tm
    src = pl.BlockSpec((tm, D), lambda i: (first + i, 0))
    return pl.pallas_call(
        _final_norm_body, grid=(rows // tm,),
        in_specs=[src] * (1 + len(adds)) + [_resident((1, D))],
        out_specs=pl.BlockSpec((tm, D), lambda i: (i, 0)),
        out_shape=jax.ShapeDtypeStruct((rows, D), F32), compiler_params=_cparams("parallel"), name="final_norm",
    )(x, *adds, g)


def _rope_tables(seq):
    def angles(pos, dim):
        inv = 1.0 / (ROPE_THETA ** (jnp.arange(0, dim, 2, dtype=F32) / dim))
        ang = pos.astype(F32)[:, None] * inv[None, :]
        return jnp.cos(ang), jnp.sin(ang)
    pos = jnp.arange(seq)
    c1, s1 = angles(pos, A_ROPE)
    cr, sr = angles(pos // GRID_W, D_HEAD // 2)
    cc, sc = angles(pos % GRID_W, D_HEAD // 2)
    pad1, pad0 = jnp.ones((seq, ROPE_PAD), F32), jnp.zeros((seq, ROPE_PAD), F32)
    return (jnp.concatenate([c1, c1, pad1], 1), jnp.concatenate([-s1, s1, pad0], 1),
            jnp.concatenate([cr, cr, cc, cc], 1), jnp.concatenate([-sr, sr, -sc, sc], 1))


def _layer_weights(l, w_in, w_uq, w_ukv, w_route_group, w_route_expert):
    D = D_MODEL
    k_r_end = A_Q_LORA + A_KV_LORA + A_ROPE
    win = jnp.concatenate([w_in[l][:, :k_r_end], jnp.zeros((D, ROPE_PAD), F32), w_in[l][:, k_r_end:]], 1)
    uq = w_uq[l].reshape(A_Q_LORA, A_HEADS, A_NOPE + A_ROPE)
    uq_rope = jnp.pad(uq[:, :, A_NOPE:], ((0, 0), (0, 0), (0, ROPE_PAD)))
    wuq = jnp.concatenate([uq[:, :, :A_NOPE].reshape(A_Q_LORA, -1), uq_rope.reshape(A_Q_LORA, -1)], 1)
    ukv = w_ukv[l].reshape(A_KV_LORA, A_HEADS, A_NOPE + A_V)
    wukv = jnp.concatenate([ukv[:, :, :A_NOPE].reshape(A_KV_LORA, -1), ukv[:, :, A_NOPE:].reshape(A_KV_LORA, -1)], 1)
    wr = jnp.concatenate([w_route_group[l], w_route_expert[l],
                          jnp.zeros((D, ROUTER_W - N_GROUPS - N_EXPERTS), F32)], 1)
    wr_hi = wr.astype(BF16)
    wr_lo = (wr - wr_hi.astype(F32)).astype(BF16)
    return win.astype(BF16), wuq.astype(BF16), wukv.astype(BF16), wr_hi, wr_lo


def kernel(x_prompt, x_sample, norm1_g, w_in, q_norm_g, kv_norm_g, w_uq, w_ukv, lam_q1, lam_k1, lam_q2, lam_k2,
           rel_bias, w_pool, pool_scale, qk_norm_q, qk_norm_k, w_lift, w_gate, b_gate, w_out, norm2_g,
           w_route_group, b_route_group, w_route_expert, b_route_expert, w_exp_gate, w_exp_up, w_exp_down,
           final_g):
    assert x_prompt.shape[1:] == x_sample.shape[1:], "both request groups must share (seq, d_model)"
    n_prompt = x_prompt.shape[0]
    x = jnp.concatenate([x_prompt, x_sample], axis=0)
    B, S, D = x.shape
    T = B * S
    t = _tiles(S, T)
    x = x.reshape(T, D)
    depth = w_in.shape[0]
    tabs = _rope_tables(S)
    near, far = _diff_bias_tables(rel_bias, t["tq"])
    row2 = lambda v: v.reshape(1, -1).astype(F32)

    moe_out = ()
    for l in range(depth):
        lam_init = 0.8 - 0.6 * math.exp(-0.3 * l)
        lam = (jnp.exp(jnp.sum(lam_q1[l] * lam_k1[l])) - jnp.exp(jnp.sum(lam_q2[l] * lam_k2[l])) + lam_init)
        win, wuq, wukv, wr_hi, wr_lo = _layer_weights(l, w_in, w_uq, w_ukv, w_route_group, w_route_expert)

        x, h, za, zb, zc, zd = _norm_inproj(x, moe_out, row2(norm1_g[l]), win, t["tm"])
        qa, ka, va, qd, kd = _prep(za, zd, row2(q_norm_g[l]), row2(kv_norm_g[l]), wuq, wukv,
                                   row2(qk_norm_q[l]), row2(qk_norm_k[l]), tabs, S, t["tm"])
        seq3 = lambda a: a.reshape(B, S, a.shape[-1])
        rep = D_HEADS // D_KV_HEADS
        o_a = _flash(seq3(qa), seq3(ka), seq3(va), kv_heads=A_HEADS, stack=1, q_w=A_QK_W, k_col0=0, v_col0=0,
                     tq=t["rows"], ts=t["tq"], tk=t["tk"])
        o_b = _diff_attn(seq3(zb), lam.reshape(1).astype(F32), near, far, tq=t["tq"], tk=t["tk"],
                         out_scale=1.0 - lam_init)
        o_c = _pool(seq3(zc), w_pool[l].astype(BF16), row2(pool_scale[l]), t["pool_chunk"])
        o_d = _flash(seq3(qd), seq3(kd), seq3(zd), kv_heads=D_KV_HEADS, stack=rep, q_w=D_HEAD, k_col0=0,
                     v_col0=D_HEADS + D_KV_HEADS, tq=t["rows"] // rep, ts=t["tq"], tk=t["tk"])
        outs = [o.reshape(T, BRANCH_W) for o in (o_a, o_b, o_c, o_d)]
        x, h2, logits = _merge(h, outs, x, w_gate[l].astype(BF16), b_gate[l].astype(F32), w_lift[l].astype(BF16),
                                w_out[l].astype(BF16), row2(norm2_g[l]), wr_hi, wr_lo, t["tm"], t["tn"])

        expert, gates = _route(logits, b_route_group[l], b_route_expert[l])
        slot_tok, slot_w, blk_exp, n_valid, slot_of = _dispatch(expert, gates, t["tb"])
        ys = _expert_ffn(h2[slot_tok], slot_w, blk_exp, n_valid, w_exp_gate, w_exp_up, w_exp_down, l, t["tb"])
        moe_out = tuple(ys[slot_of[:, k]] for k in range(TOP_K))

    t_prompt = n_prompt * S
    y_prompt = _final_norm(x, moe_out, row2(final_g), 0, t_prompt, t["tm"]).reshape(n_prompt, S, D)
    y_sample = _final_norm(x, moe_out, row2(final_g), t_prompt, T - t_prompt, t["tm"]).reshape(B - n_prompt, S, D)
    return y_prompt, y_sample
```

```python
import functools
import math

import jax
import jax.numpy as jnp
from jax import lax
from jax.experimental import pallas as pl
from jax.experimental.pallas import tpu as pltpu

F32 = jnp.float32
BF16 = jnp.bfloat16
LOG2E = 1.4426950408889634

D_MODEL = 2048
GRID_W = 64
BRANCH_W = 512
N_BRANCH = 4
ROPE_THETA = 10000.0
EPS = 1e-6
A_HEADS, A_Q_LORA, A_KV_LORA, A_NOPE, A_ROPE, A_V = 4, 384, 128, 128, 64, 128
B_HEADS, B_QK, B_V = 4, 64, 128
REL_BUCKETS, REL_MAX_DIST = 32, 128
POOL_WINDOWS = (2, 4, 8, 16)
C_GROUPS, C_GROUP_W = 4, 128
D_HEADS, D_KV_HEADS, D_HEAD = 4, 2, 128
N_GROUPS, EXP_PER_GROUP, N_EXPERTS, TOP_K, D_EXPERT = 4, 8, 32, 2, 512

LANES = 128
ROPE_PAD = LANES - A_ROPE
ZA_W = A_Q_LORA + A_KV_LORA + LANES
ZB_W = 3 * B_HEADS * B_V
ZC_W = BRANCH_W
ZD_W = (D_HEADS + 2 * D_KV_HEADS) * D_HEAD
A_QK_W = 2 * LANES
ROUTER_W = LANES
HALO = 16
VMEM_LIMIT = 56 * 2 ** 20


def _tiles(seq, tokens):
    def pick(n, pref):
        t = min(pref, n)
        while n % t:
            t //= 2
        return t
    return dict(
        tm=pick(seq, 512),
        tq=pick(seq, 512),
        rows=pick(seq, 1024),
        tk=pick(seq, 4096),
        tn=256,
        tb=256,
        pool_chunk=pick(seq, 1024),
    )


def _cparams(*sem):
    return pltpu.CompilerParams(dimension_semantics=sem, vmem_limit_bytes=VMEM_LIMIT)


def _resident(shape):
    zeros = (0,) * len(shape)
    return pl.BlockSpec(shape, lambda *_: zeros, pipeline_mode=pl.Buffered(1))


def _rms(x, g=None):
    y = x * lax.rsqrt(jnp.mean(x * x, axis=-1, keepdims=True) + EPS)
    return y if g is None else y * g


def _rope_lanes(x, c, s):
    lane = lax.broadcasted_iota(jnp.int32, x.shape, 1)
    first_half = (lane & 32) == 0
    partner = jnp.where(first_half, pltpu.roll(x, 96, 1), pltpu.roll(x, 32, 1))
    return x * c + partner * s


def _residual_sum(x_ref, y_refs):
    x = x_ref[...]
    for y_ref in y_refs:
        x = x + y_ref[...].astype(F32)
    return x


def _norm_inproj_body(*refs, n_add):
    x_ref, y_refs = refs[0], refs[1:1 + n_add]
    g_ref, w_ref = refs[1 + n_add:3 + n_add]
    outs = refs[3 + n_add:]
    x = _residual_sum(x_ref, y_refs)
    if n_add:
        outs[0][...] = x
        outs = outs[1:]
    h_ref, z_refs = outs[0], outs[1:]
    h = _rms(x, g_ref[...]).astype(BF16)
    h_ref[...] = h
    off = 0
    for z_ref in z_refs:
        w = z_ref.shape[1]
        z_ref[...] = jnp.dot(h, w_ref[:, off:off + w], preferred_element_type=F32).astype(BF16)
        off += w


def _norm_inproj(x, adds, g, w, tm):
    T, D = x.shape
    widths = (ZA_W, ZB_W, ZC_W, ZD_W)
    row = lambda w_: pl.BlockSpec((tm, w_), lambda i: (i, 0))
    sum_spec, sum_shape = ([row(D)], [jax.ShapeDtypeStruct((T, D), F32)]) if adds else ([], [])
    outs = pl.pallas_call(
        functools.partial(_norm_inproj_body, n_add=len(adds)),
        grid=(T // tm,),
        in_specs=[row(D)] * (1 + len(adds)) + [_resident((1, D)), _resident((D, sum(widths)))],
        out_specs=sum_spec + [row(D)] + [row(w_) for w_ in widths],
        out_shape=sum_shape + [jax.ShapeDtypeStruct((T, D), BF16)]
        + [jax.ShapeDtypeStruct((T, w_), BF16) for w_ in widths],
        compiler_params=_cparams("parallel"),
        name="norm_inproj",
    )(x, *adds, g, w)
    return outs if adds else [x] + list(outs)


def _prep_body(za_ref, zd_ref, gq_ref, gkv_ref, wuq_ref, wukv_ref, gdq_ref, gdk_ref,
               ca_ref, sa_ref, cd_ref, sd_ref, qa_ref, ka_ref, va_ref, qd_ref, kd_ref):
    a_scale = (A_NOPE + A_ROPE) ** -0.5 * LOG2E
    d_scale = D_HEAD ** -0.5 * LOG2E
    ca, sa, cd, sd = ca_ref[...], sa_ref[...], cd_ref[...], sd_ref[...]
    cq = _rms(za_ref[:, :A_Q_LORA].astype(F32), gq_ref[...]).astype(BF16)
    q = jnp.dot(cq, wuq_ref[...], preferred_element_type=F32)
    ckv = _rms(za_ref[:, A_Q_LORA:A_Q_LORA + A_KV_LORA].astype(F32), gkv_ref[...]).astype(BF16)
    kv = jnp.dot(ckv, wukv_ref[...], preferred_element_type=F32)
    k_rope = _rope_lanes(za_ref[:, A_Q_LORA + A_KV_LORA:].astype(F32), ca, sa).astype(BF16)
    nope_w = A_HEADS * A_NOPE
    for h in range(A_HEADS):
        lo = h * A_QK_W
        qa_ref[:, lo:lo + LANES] = (q[:, h * LANES:(h + 1) * LANES] * a_scale).astype(BF16)
        q_rope = _rope_lanes(q[:, nope_w + h * LANES:nope_w + (h + 1) * LANES], ca, sa)
        qa_ref[:, lo + LANES:lo + 2 * LANES] = (q_rope * a_scale).astype(BF16)
        ka_ref[:, lo:lo + LANES] = kv[:, h * LANES:(h + 1) * LANES].astype(BF16)
        ka_ref[:, lo + LANES:lo + 2 * LANES] = k_rope
    va_ref[...] = kv[:, nope_w:].astype(BF16)
    for h in range(D_HEADS):
        xh = _rms(zd_ref[:, h * D_HEAD:(h + 1) * D_HEAD].astype(F32), gdq_ref[...])
        qd_ref[:, h * D_HEAD:(h + 1) * D_HEAD] = (_rope_lanes(xh, cd, sd) * d_scale).astype(BF16)
    k_off = D_HEADS * D_HEAD
    for h in range(D_KV_HEADS):
        xh = _rms(zd_ref[:, k_off + h * D_HEAD:k_off + (h + 1) * D_HEAD].astype(F32), gdk_ref[...])
        kd_ref[:, h * D_HEAD:(h + 1) * D_HEAD] = _rope_lanes(xh, cd, sd).astype(BF16)


def _prep(za, zd, gq, gkv, wuq, wukv, gdq, gdk, tabs, seq, tm):
    T = za.shape[0]
    per_seq = seq // tm
    row = lambda w_: pl.BlockSpec((tm, w_), lambda i: (i, 0))
    tab = pl.BlockSpec((tm, LANES), lambda i: (i % per_seq, 0))
    out_w = (A_HEADS * A_QK_W, A_HEADS * A_QK_W, A_HEADS * A_V, D_HEADS * D_HEAD, D_KV_HEADS * D_HEAD)
    return pl.pallas_call(
        _prep_body,
        grid=(T // tm,),
        in_specs=[row(ZA_W), row(ZD_W), _resident(gq.shape), _resident(gkv.shape), _resident(wuq.shape),
                  _resident(wukv.shape), _resident(gdq.shape), _resident(gdk.shape), tab, tab, tab, tab],
        out_specs=[row(w_) for w_ in out_w],
        out_shape=[jax.ShapeDtypeStruct((T, w_), BF16) for w_ in out_w],
        compiler_params=_cparams("parallel"),
        name="mixer_prep",
    )(za, zd, gq, gkv, wuq, wukv, gdq, gdk, *tabs)


def _qk(q, k):
    return lax.dot_general(q, k, (((1,), (1,)), ((), ())), preferred_element_type=F32)


def _lane_fold(fn, acc, x):
    for c in range(x.shape[1] // LANES):
        acc = fn(acc, x[:, c * LANES:(c + 1) * LANES])
    return acc


def _score_pass(q, k_ref, s_ref, tk, max_shift=None):
    tq, ts = s_ref.shape[1:]
    sub = tk // ts

    def step(j, mrun):
        ks = pl.multiple_of(j * tk, tk)
        s = _qk(q, k_ref[0, pl.ds(ks, tk), :])
        for c in range(sub):
            jj = j * sub + c
            sc = s[:, c * ts:(c + 1) * ts]
            s_ref[jj] = sc
            m_sub = _lane_fold(jnp.maximum, sc[:, :LANES], sc[:, LANES:])
            mrun = jnp.maximum(mrun, m_sub if max_shift is None else m_sub + max_shift(jj))
        return mrun

    return lax.fori_loop(0, k_ref.shape[1] // tk, step, jnp.full((tq, LANES), -jnp.inf, F32))


def _value_pass(v_ref, s_ref, tk, m_of):
    tq, ts = s_ref.shape[1:]
    sub = tk // ts
    dv = v_ref.shape[2]
    ones = jnp.ones((tk, LANES), BF16)

    def step(j, acc):
        ks = pl.multiple_of(j * tk, tk)
        p = []
        for c in range(sub):
            jj = j * sub + c
            m = m_of(jj)
            s = s_ref[jj]
            p += [jnp.exp2((s[:, i * LANES:(i + 1) * LANES] - m).astype(BF16)) for i in range(ts // LANES)]
        v_ext = jnp.concatenate([v_ref[0, pl.ds(ks, tk), :], ones], axis=1)
        return acc + jnp.dot(jnp.concatenate(p, axis=1), v_ext, preferred_element_type=F32)

    acc = lax.fori_loop(0, v_ref.shape[1] // tk, step, jnp.zeros((tq, dv + LANES), F32))
    return acc[:, :dv] / acc[:, dv:]


def _row_max(mrun):
    return jnp.broadcast_to(jnp.max(mrun, axis=-1, keepdims=True), mrun.shape)


def _flash_body(q_ref, k_ref, v_ref, o_ref, s_ref, *, tk, stack):
    tq = q_ref.shape[1]
    q_w = q_ref.shape[2] // stack
    q = jnp.concatenate([q_ref[0, :, c * q_w:(c + 1) * q_w] for c in range(stack)], axis=0)
    m = _row_max(_score_pass(q, k_ref, s_ref, tk))
    o = _value_pass(v_ref, s_ref, tk, lambda jj: m).astype(o_ref.dtype)
    o_ref[0] = jnp.concatenate([o[c * tq:(c + 1) * tq] for c in range(stack)], axis=1)


def _flash(q, k, v, *, kv_heads, stack, q_w, k_col0, v_col0, tq, ts, tk):
    B, S, _ = q.shape
    dv = LANES
    return pl.pallas_call(
        functools.partial(_flash_body, tk=tk, stack=stack),
        grid=(B, kv_heads, S // tq),
        in_specs=[pl.BlockSpec((1, tq, stack * q_w), lambda b, h, i: (b, i, h)),
                  pl.BlockSpec((1, S, q_w), lambda b, h, i: (b, 0, k_col0 + h)),
                  pl.BlockSpec((1, S, dv), lambda b, h, i: (b, 0, v_col0 + h))],
        out_specs=pl.BlockSpec((1, tq, stack * dv), lambda b, h, i: (b, i, h)),
        out_shape=jax.ShapeDtypeStruct((B, S, kv_heads * stack * dv), BF16),
        scratch_shapes=[pltpu.VMEM((S // ts, stack * tq, ts), F32)],
        compiler_params=_cparams("parallel", "parallel", "arbitrary"),
        name="flash_attn",
    )(q, k, v)


def _diff_body(lam_ref, cb_ref, q_ref, k_ref, v_ref, diag_ref, corner_ref, o_ref, s_ref, *, tk, out_scale):
    h, qi = pl.program_id(1), pl.program_id(2)
    n_sub = s_ref.shape[0]
    tq = q_ref.shape[1]
    nd = REL_MAX_DIST
    qf = q_ref[0].astype(F32) * (B_QK ** -0.5 * LOG2E)
    lane = lax.broadcasted_iota(jnp.int32, qf.shape, 1)
    c_lo, c_hi = cb_ref[0, h], cb_ref[1, h]
    side_shift = lambda jj: jnp.where(jj < qi, c_lo, jnp.where(jj > qi, c_hi, 0.0))
    tile_max = lambda s: _lane_fold(jnp.maximum, s[:, :LANES], s[:, LANES:])

    q = jnp.concatenate([jnp.where(lane < B_QK, qf, 0.0), jnp.where(lane >= B_QK, qf, 0.0)], axis=0).astype(BF16)
    far = lambda jj: jnp.where(jj < qi - 1, c_lo, jnp.where(jj > qi + 1, c_hi, -jnp.inf))
    mrun = _score_pass(q, k_ref, s_ref, tk, max_shift=far)

    diag = diag_ref[0]
    s = s_ref[qi] + jnp.concatenate([diag, diag], axis=0)
    s_ref[qi] = s
    mrun = jnp.maximum(mrun, tile_max(s))

    def neighbour(jj, rows0, cols0, corner, c):
        def fix(mrun):
            for r in (rows0, tq + rows0):
                s_ref[jj, r:r + nd, cols0:cols0 + nd] = s_ref[jj, r:r + nd, cols0:cols0 + nd] + corner
            return jnp.maximum(mrun, tile_max(s_ref[jj]) + c)
        return fix

    mrun = lax.cond(qi >= 1, neighbour(jnp.maximum(qi - 1, 0), 0, tq - nd, corner_ref[0, 0], c_lo),
                    lambda m_: m_, mrun)
    mrun = lax.cond(qi + 1 < n_sub, neighbour(jnp.minimum(qi + 1, n_sub - 1), tq - nd, 0, corner_ref[0, 1], c_hi),
                    lambda m_: m_, mrun)
    m = _row_max(mrun)
    o = _value_pass(v_ref, s_ref, tk, lambda jj: m - side_shift(jj))
    o = o[:tq] - lam_ref[0] * o[tq:]
    o_ref[0] = (_rms(o) * out_scale).astype(o_ref.dtype)


def _rel_bucket(rel):
    half = REL_BUCKETS // 2
    max_exact = half // 2
    ret = (rel > 0).astype(jnp.int32) * half
    n = jnp.abs(rel)
    large = max_exact + (jnp.log(jnp.maximum(n, 1).astype(F32) / max_exact)
                         / math.log(REL_MAX_DIST / max_exact) * (half - max_exact)).astype(jnp.int32)
    large = jnp.minimum(large, half - 1)
    return ret + jnp.where(n < max_exact, n, large)


def _diff_bias_tables(rel_bias, tq):
    nd = REL_MAX_DIST
    rb = rel_bias.astype(F32) * LOG2E
    far = jnp.stack([rb[REL_BUCKETS // 2 - 1], rb[REL_BUCKETS - 1]])

    def table(rel):
        bucket = _rel_bucket(rel)
        out = jnp.zeros((B_HEADS,) + rel.shape, F32)
        for b in range(REL_BUCKETS):
            out = jnp.where(bucket[None] == b, rb[b].reshape((B_HEADS,) + (1,) * rel.ndim), out)
        return out

    a, e = jnp.arange(tq), jnp.arange(nd)
    diag = table(a[None, :] - a[:, None])
    lo = table((tq - nd + e[None, :]) - tq - e[:, None]) - far[0][:, None, None]
    hi = table(e[None, :] + tq - (tq - nd + e[:, None])) - far[1][:, None, None]
    return diag, jnp.stack([lo, hi], axis=1), far


def _diff_attn(zb, lam, diag, corner, far, *, tq, tk, out_scale):
    B, S, _ = zb.shape
    nd = REL_MAX_DIST
    assert tq >= nd, "far key chunks must lie beyond the last distinct relative bucket"
    H = B_HEADS
    smem = pl.BlockSpec(memory_space=pltpu.SMEM)
    return pl.pallas_call(
        functools.partial(_diff_body, tk=tk, out_scale=out_scale),
        grid=(B, H, S // tq),
        in_specs=[smem, smem,
                  pl.BlockSpec((1, tq, B_V), lambda b, h, i: (b, i, h)),
                  pl.BlockSpec((1, S, B_V), lambda b, h, i: (b, 0, H + h)),
                  pl.BlockSpec((1, S, B_V), lambda b, h, i: (b, 0, 2 * H + h)),
                  pl.BlockSpec((1, tq, tq), lambda b, h, i: (h, 0, 0)),
                  pl.BlockSpec((1, 2, nd, nd), lambda b, h, i: (h, 0, 0, 0))],
        out_specs=pl.BlockSpec((1, tq, B_V), lambda b, h, i: (b, i, h)),
        out_shape=jax.ShapeDtypeStruct((B, S, H * B_V), BF16),
        scratch_shapes=[pltpu.VMEM((S // tq, 2 * tq, tq), F32)],
        compiler_params=_cparams("parallel", "parallel", "arbitrary"),
        name="diff_attn",
    )(lam, far, zb, zb, zb, diag, corner)


def _pool_body(u_ref, w_ref, sc_ref, o_ref, pad_ref, *, chunk):
    g = pl.program_id(1)
    S = u_ref.shape[1]
    pad_ref[0:HALO, :] = jnp.zeros((HALO, LANES), F32)
    pad_ref[HALO + S:, :] = jnp.zeros((HALO, LANES), F32)
    pad_ref[HALO:HALO + S, :] = u_ref[0].astype(F32)
    w_mat, sc = w_ref[0], sc_ref[...]

    def pooled(win):
        def body(c, _):
            r0 = pl.multiple_of(c * chunk, chunk)
            tot = pad_ref[pl.ds(r0 + HALO - win // 2, chunk), :]
            for j in range(1 - win // 2, win // 2):
                tot = tot + pad_ref[pl.ds(r0 + HALO + j, chunk), :]
            t = r0 + lax.broadcasted_iota(jnp.int32, (chunk, 1), 0)
            cnt = jnp.clip(t - win // 2 + win, 0, S) - jnp.clip(t - win // 2, 0, S)
            d = tot / cnt.astype(F32) - pad_ref[pl.ds(r0 + HALO, chunk), :]
            y = jnp.dot(d.astype(BF16), w_mat, preferred_element_type=F32) * sc
            o_ref[0, pl.ds(r0, chunk), :] = y.astype(o_ref.dtype)
            return 0
        lax.fori_loop(0, S // chunk, body, 0)

    for gi, win in enumerate(POOL_WINDOWS):
        pl.when(g == gi)(functools.partial(pooled, win))


def _pool(zc, w_pool, pool_scale, chunk):
    B, S, _ = zc.shape
    blk = pl.BlockSpec((1, S, LANES), lambda b, g: (b, 0, g))
    return pl.pallas_call(
        functools.partial(_pool_body, chunk=chunk),
        grid=(B, C_GROUPS),
        in_specs=[blk, pl.BlockSpec((1, C_GROUP_W, C_GROUP_W), lambda b, g: (g, 0, 0)),
                  pl.BlockSpec((1, LANES), lambda b, g: (0, g))],
        out_specs=blk,
        out_shape=jax.ShapeDtypeStruct((B, S, BRANCH_W), BF16),
        scratch_shapes=[pltpu.VMEM((S + 2 * HALO, LANES), F32)],
        compiler_params=_cparams("parallel", "arbitrary"),
        name="pool_mixer",
    )(zc, w_pool, pool_scale)


def _merge_body(h_ref, oa_ref, ob_ref, oc_ref, od_ref, x_ref, wg_ref, bg_ref, wl_ref, wo_ref, g2_ref,
                wr_ref, xn_ref, h2_ref, lg_ref, acc_ref):
    j = pl.program_id(1)

    @pl.when(j == 0)
    def _():
        acc_ref[...] = x_ref[...]

    h = h_ref[...]
    merged = None
    for b, o_ref in enumerate((oa_ref, ob_ref, oc_ref, od_ref)):
        gate = jax.nn.sigmoid(jnp.dot(h, wg_ref[0, b], preferred_element_type=F32) + bg_ref[b:b + 1, :])
        term = gate * jnp.dot(o_ref[...], wl_ref[0, b], preferred_element_type=F32)
        merged = term if merged is None else merged + term
    acc_ref[...] += jnp.dot(merged.astype(BF16), wo_ref[...], preferred_element_type=F32)

    @pl.when(j == pl.num_programs(1) - 1)
    def _():
        xn = acc_ref[...]
        xn_ref[...] = xn
        h2 = _rms(xn, g2_ref[...])
        h2_ref[...] = h2.astype(BF16)
        h_hi = h2.astype(BF16)
        h_lo = (h2 - h_hi.astype(F32)).astype(BF16)
        hi = jnp.dot(h_hi, wr_ref[...], preferred_element_type=F32)
        lg_ref[...] = (hi[:, :ROUTER_W] + hi[:, ROUTER_W:]
                       + jnp.dot(h_lo, wr_ref[:, :ROUTER_W], preferred_element_type=F32))


def _merge(h, outs, x, wg, bg, wl, wo, g2, wr, tm, tn):
    T, D = x.shape
    row = lambda w_: pl.BlockSpec((tm, w_), lambda i, j: (i, 0))
    return pl.pallas_call(
        _merge_body,
        grid=(T // tm, D // tn),
        in_specs=[row(D)] + [row(BRANCH_W)] * N_BRANCH + [
            row(D),
            pl.BlockSpec((1, N_BRANCH, D, tn), lambda i, j: (j, 0, 0, 0)),
            pl.BlockSpec((N_BRANCH, tn), lambda i, j: (0, j)),
            pl.BlockSpec((1, N_BRANCH, BRANCH_W, tn), lambda i, j: (j, 0, 0, 0)),
            pl.BlockSpec((tn, D), lambda i, j: (j, 0)),
            _resident((1, D)), _resident((D, 2 * ROUTER_W))],
        out_specs=[row(D), row(D), row(ROUTER_W)],
        out_shape=[jax.ShapeDtypeStruct((T, D), F32), jax.ShapeDtypeStruct((T, D), BF16),
                   jax.ShapeDtypeStruct((T, ROUTER_W), F32)],
        scratch_shapes=[pltpu.VMEM((tm, D), F32)],
        compiler_params=_cparams("parallel", "arbitrary"),
        name="gated_merge",
    )(h, *outs, x, wg, bg, wl, wo, g2, wr)


def _route(logits, b_rg, b_re):
    T = logits.shape[0]
    lg = logits[:, :N_GROUPS] + b_rg.astype(F32)
    pg = jax.nn.softmax(lg, axis=-1)
    g_top = jnp.argmax(lg, axis=-1)
    pg_top = jnp.take_along_axis(pg, g_top[:, None], axis=1)
    le = (logits[:, N_GROUPS:N_GROUPS + N_EXPERTS] + b_re.astype(F32)).reshape(T, N_GROUPS, EXP_PER_GROUP)
    le = jnp.take_along_axis(le, g_top[:, None, None], axis=1)[:, 0]
    pe = jax.nn.softmax(le, axis=-1)
    top_p, top_i = lax.top_k(pe, TOP_K)
    gates = pg_top * top_p / jnp.sum(top_p, axis=-1, keepdims=True)
    expert = g_top[:, None] * EXP_PER_GROUP + top_i
    return expert.astype(jnp.int32), gates


def _dispatch(expert, gates, tb):
    T = expert.shape[0]
    A = T * TOP_K
    flat_e = expert.reshape(A)
    iota = jnp.arange(A, dtype=jnp.int32)
    se, order, sw = lax.sort((flat_e, iota, gates.reshape(A)), num_keys=1, is_stable=True)
    counts = jnp.sum((flat_e[:, None] == jnp.arange(N_EXPERTS, dtype=jnp.int32)[None, :]).astype(jnp.int32), axis=0)
    start = jnp.cumsum(counts) - counts
    padded = (counts + tb - 1) // tb * tb
    pend = jnp.cumsum(padded)
    pstart = pend - padded
    n_blocks = -(-A // tb) + N_EXPERTS
    P = n_blocks * tb
    blk_first = jnp.arange(n_blocks, dtype=jnp.int32) * tb
    blk_exp = jnp.minimum(jnp.sum((pend[None, :] <= blk_first[:, None]).astype(jnp.int32), axis=1), N_EXPERTS - 1)
    n_valid = (pend[-1] // tb).astype(jnp.int32).reshape(1)
    per_slot = lambda v: jnp.repeat(v[blk_exp], tb)
    slot = jnp.arange(P, dtype=jnp.int32)
    in_expert = slot - per_slot(pstart)
    valid = in_expert < per_slot(counts)
    src = jnp.where(valid, in_expert + per_slot(start), slot % A)
    slot_tok = order[src] // TOP_K
    slot_w = jnp.where(valid, sw[src], 0.0)
    dest = (pstart - start)[se] + iota
    _, slot_of = lax.sort((order, dest), num_keys=1)
    return slot_tok, slot_w, blk_exp, n_valid, slot_of.reshape(T, TOP_K)


def _expert_body(be_ref, nv_ref, x_ref, sw_ref, wg_ref, wu_ref, wd_ref, o_ref, wg_s, wu_s, wd_s):
    i = pl.program_id(0)

    @pl.when(jnp.logical_or(i == 0, be_ref[i] != be_ref[jnp.maximum(i - 1, 0)]))
    def _():
        wg_s[...] = wg_ref[0, 0].astype(BF16)
        wu_s[...] = wu_ref[0, 0].astype(BF16)
        wd_s[...] = wd_ref[0, 0].astype(BF16)

    @pl.when(i < nv_ref[0])
    def _():
        x = x_ref[...]
        g = jnp.dot(x, wg_s[...], preferred_element_type=F32)
        u = jnp.dot(x, wu_s[...], preferred_element_type=F32)
        a = (g * jax.nn.sigmoid(g) * u).astype(BF16)
        y = jnp.dot(a, wd_s[...], preferred_element_type=F32)
        o_ref[...] = (y * sw_ref[...]).astype(o_ref.dtype)

    @pl.when(i >= nv_ref[0])
    def _():
        o_ref[...] = jnp.zeros(o_ref.shape, o_ref.dtype)


def _expert_ffn(xs, slot_w, blk_exp, n_valid, wg, wu, wd, layer, tb):
    P, D = xs.shape
    grid_spec = pltpu.PrefetchScalarGridSpec(
        num_scalar_prefetch=2,
        grid=(P // tb,),
        in_specs=[pl.BlockSpec((tb, D), lambda i, be, nv: (i, 0)),
                  pl.BlockSpec((tb, 1), lambda i, be, nv: (i, 0)),
                  pl.BlockSpec((1, 1, D, D_EXPERT), lambda i, be, nv: (layer, be[i], 0, 0)),
                  pl.BlockSpec((1, 1, D, D_EXPERT), lambda i, be, nv: (layer, be[i], 0, 0)),
                  pl.BlockSpec((1, 1, D_EXPERT, D), lambda i, be, nv: (layer, be[i], 0, 0))],
        out_specs=pl.BlockSpec((tb, D), lambda i, be, nv: (i, 0)),
        scratch_shapes=[pltpu.VMEM((D, D_EXPERT), BF16), pltpu.VMEM((D, D_EXPERT), BF16),
                        pltpu.VMEM((D_EXPERT, D), BF16)],
    )
    return pl.pallas_call(
        _expert_body,
        grid_spec=grid_spec,
        out_shape=jax.ShapeDtypeStruct((P, D), BF16),
        compiler_params=_cparams("arbitrary"),
        name="expert_ffn",
    )(blk_exp, n_valid, xs, slot_w.reshape(P, 1), wg, wu, wd)


def _final_norm_body(*refs):
    x_ref, y_refs, (g_ref, o_ref) = refs[0], refs[1:-2], refs[-2:]
    o_ref[...] = _rms(_residual_sum(x_ref, y_refs), g_ref[...])


def _final_norm(x, adds, g, row0, rows, tm):
    D = x.shape[1]
    first = row0 // tm
    src = pl.BlockSpec((tm, D), lambda i: (first + i, 0))
    return pl.pallas_call(
        _final_norm_body, grid=(rows // tm,),
        in_specs=[src] * (1 + len(adds)) + [_resident((1, D))],
        out_specs=pl.BlockSpec((tm, D), lambda i: (i, 0)),
        out_shape=jax.ShapeDtypeStruct((rows, D), F32), compiler_params=_cparams("parallel"), name="final_norm",
    )(x, *adds, g)


def _rope_tables(seq):
    def angles(pos, dim):
        inv = 1.0 / (ROPE_THETA ** (jnp.arange(0, dim, 2, dtype=F32) / dim))
        ang = pos.astype(F32)[:, None] * inv[None, :]
        return jnp.cos(ang), jnp.sin(ang)
    pos = jnp.arange(seq)
    c1, s1 = angles(pos, A_ROPE)
    cr, sr = angles(pos // GRID_W, D_HEAD // 2)
    cc, sc = angles(pos % GRID_W, D_HEAD // 2)
    pad1, pad0 = jnp.ones((seq, ROPE_PAD), F32), jnp.zeros((seq, ROPE_PAD), F32)
    return (jnp.concatenate([c1, c1, pad1], 1), jnp.concatenate([-s1, s1, pad0], 1),
            jnp.concatenate([cr, cr, cc, cc], 1), jnp.concatenate([-sr, sr, -sc, sc], 1))


def _layer_weights(l, w_in, w_uq, w_ukv, w_route_group, w_route_expert):
    D = D_MODEL
    k_r_end = A_Q_LORA + A_KV_LORA + A_ROPE
    win = jnp.concatenate([w_in[l][:, :k_r_end], jnp.zeros((D, ROPE_PAD), F32), w_in[l][:, k_r_end:]], 1)
    uq = w_uq[l].reshape(A_Q_LORA, A_HEADS, A_NOPE + A_ROPE)
    uq_rope = jnp.pad(uq[:, :, A_NOPE:], ((0, 0), (0, 0), (0, ROPE_PAD)))
    wuq = jnp.concatenate([uq[:, :, :A_NOPE].reshape(A_Q_LORA, -1), uq_rope.reshape(A_Q_LORA, -1)], 1)
    ukv = w_ukv[l].reshape(A_KV_LORA, A_HEADS, A_NOPE + A_V)
    wukv = jnp.concatenate([ukv[:, :, :A_NOPE].reshape(A_KV_LORA, -1), ukv[:, :, A_NOPE:].reshape(A_KV_LORA, -1)], 1)
    wr = jnp.concatenate([w_route_group[l], w_route_expert[l],
                          jnp.zeros((D, ROUTER_W - N_GROUPS - N_EXPERTS), F32)], 1)
    wr_hi = wr.astype(BF16)
    wr_lo = (wr - wr_hi.astype(F32)).astype(BF16)
    return win.astype(BF16), wuq.astype(BF16), wukv.astype(BF16), jnp.concatenate([wr_hi, wr_lo], 1)


def _column_tiled(w, tn):
    nb, k, n = w.shape
    return jnp.transpose(w.astype(BF16).reshape(nb, k, n // tn, tn), (2, 0, 1, 3))


def kernel(x_prompt, x_sample, norm1_g, w_in, q_norm_g, kv_norm_g, w_uq, w_ukv, lam_q1, lam_k1, lam_q2, lam_k2,
           rel_bias, w_pool, pool_scale, qk_norm_q, qk_norm_k, w_lift, w_gate, b_gate, w_out, norm2_g,
           w_route_group, b_route_group, w_route_expert, b_route_expert, w_exp_gate, w_exp_up, w_exp_down,
           final_g):
    assert x_prompt.shape[1:] == x_sample.shape[1:], "both request groups must share (seq, d_model)"
    n_prompt = x_prompt.shape[0]
    x = jnp.concatenate([x_prompt, x_sample], axis=0)
    B, S, D = x.shape
    T = B * S
    t = _tiles(S, T)
    x = x.reshape(T, D)
    depth = w_in.shape[0]
    tabs = _rope_tables(S)
    diag, corner, far = _diff_bias_tables(rel_bias, t["tq"])
    row2 = lambda v: v.reshape(1, -1).astype(F32)

    moe_out = ()
    for l in range(depth):
        lam_init = 0.8 - 0.6 * math.exp(-0.3 * l)
        lam = (jnp.exp(jnp.sum(lam_q1[l] * lam_k1[l])) - jnp.exp(jnp.sum(lam_q2[l] * lam_k2[l])) + lam_init)
        win, wuq, wukv, wr = _layer_weights(l, w_in, w_uq, w_ukv, w_route_group, w_route_expert)

        x, h, za, zb, zc, zd = _norm_inproj(x, moe_out, row2(norm1_g[l]), win, t["tm"])
        qa, ka, va, qd, kd = _prep(za, zd, row2(q_norm_g[l]), row2(kv_norm_g[l]), wuq, wukv,
                                   row2(qk_norm_q[l]), row2(qk_norm_k[l]), tabs, S, t["tm"])
        seq3 = lambda a: a.reshape(B, S, a.shape[-1])
        rep = D_HEADS // D_KV_HEADS
        o_a = _flash(seq3(qa), seq3(ka), seq3(va), kv_heads=A_HEADS, stack=1, q_w=A_QK_W, k_col0=0, v_col0=0,
                     tq=t["rows"], ts=t["tq"], tk=t["tk"])
        o_b = _diff_attn(seq3(zb), lam.reshape(1).astype(F32), diag, corner, far, tq=t["tq"], tk=t["tk"],
                         out_scale=1.0 - lam_init)
        o_c = _pool(seq3(zc), w_pool[l].astype(BF16), row2(pool_scale[l]), t["pool_chunk"])
        o_d = _flash(seq3(qd), seq3(kd), seq3(zd), kv_heads=D_KV_HEADS, stack=rep, q_w=D_HEAD, k_col0=0,
                     v_col0=D_HEADS + D_KV_HEADS, tq=t["rows"] // rep, ts=t["tq"], tk=t["tk"])
        outs = [o.reshape(T, BRANCH_W) for o in (o_a, o_b, o_c, o_d)]
        x, h2, logits = _merge(h, outs, x, _column_tiled(w_gate[l], t["tn"]), b_gate[l].astype(F32),
                               _column_tiled(w_lift[l], t["tn"]), w_out[l].astype(BF16), row2(norm2_g[l]), wr,
                               t["tm"], t["tn"])

        expert, gates = _route(logits, b_route_group[l], b_route_expert[l])
        slot_tok, slot_w, blk_exp, n_valid, slot_of = _dispatch(expert, gates, t["tb"])
        ys = _expert_ffn(h2[slot_tok], slot_w, blk_exp, n_valid, w_exp_gate, w_exp_up, w_exp_down, l, t["tb"])
        moe_out = tuple(ys[slot_of[:, k]] for k in range(TOP_K))

    t_prompt = n_prompt * S
    y_prompt = _final_norm(x, moe_out, row2(final_g), 0, t_prompt, t["tm"]).reshape(n_prompt, S, D)
    y_sample = _final_norm(x, moe_out, row2(final_g), t_prompt, T - t_prompt, t["tm"]).reshape(B - n_prompt, S, D)
    return y_prompt, y_sample
```

```python
import functools
import math

import jax
import jax.numpy as jnp
from jax import lax
from jax.experimental import pallas as pl
from jax.experimental.pallas import tpu as pltpu

F32 = jnp.float32
BF16 = jnp.bfloat16
LOG2E = 1.4426950408889634

D_MODEL = 2048
GRID_W = 64
BRANCH_W = 512
N_BRANCH = 4
ROPE_THETA = 10000.0
EPS = 1e-6
A_HEADS, A_Q_LORA, A_KV_LORA, A_NOPE, A_ROPE, A_V = 4, 384, 128, 128, 64, 128
B_HEADS, B_QK, B_V = 4, 64, 128
REL_BUCKETS, REL_MAX_DIST = 32, 128
POOL_WINDOWS = (2, 4, 8, 16)
C_GROUPS, C_GROUP_W = 4, 128
D_HEADS, D_KV_HEADS, D_HEAD = 4, 2, 128
N_GROUPS, EXP_PER_GROUP, N_EXPERTS, TOP_K, D_EXPERT = 4, 8, 32, 2, 512

LANES = 128
ROPE_PAD = LANES - A_ROPE
ZA_W = A_Q_LORA + A_KV_LORA + LANES
ZB_W = 3 * B_HEADS * B_V
ZC_W = BRANCH_W
ZD_W = (D_HEADS + 2 * D_KV_HEADS) * D_HEAD
A_QK_W = 2 * LANES
ROUTER_W = LANES
HALO = 16
VMEM_LIMIT = 56 * 2 ** 20


def _tiles(seq, tokens):
    def pick(n, pref):
        t = min(pref, n)
        while n % t:
            t //= 2
        return t
    return dict(
        tm=pick(seq, 512),
        tq=pick(seq, 512),
        rows=pick(seq, 1024),
        tk=pick(seq, 4096),
        tn=256,
        tb=256,
        pool_chunk=pick(seq, 1024),
    )


def _cparams(*sem):
    return pltpu.CompilerParams(dimension_semantics=sem, vmem_limit_bytes=VMEM_LIMIT)


def _resident(shape):
    zeros = (0,) * len(shape)
    return pl.BlockSpec(shape, lambda *_: zeros, pipeline_mode=pl.Buffered(1))


def _rms(x, g=None):
    y = x * lax.rsqrt(jnp.mean(x * x, axis=-1, keepdims=True) + EPS)
    return y if g is None else y * g


def _rope_lanes(x, c, s):
    lane = lax.broadcasted_iota(jnp.int32, x.shape, 1)
    first_half = (lane & 32) == 0
    partner = jnp.where(first_half, pltpu.roll(x, 96, 1), pltpu.roll(x, 32, 1))
    return x * c + partner * s


def _residual_sum(x_ref, y_refs):
    x = x_ref[...]
    for y_ref in y_refs:
        x = x + y_ref[...].astype(F32)
    return x


def _norm_inproj_body(*refs, n_add):
    x_ref, y_refs = refs[0], refs[1:1 + n_add]
    g_ref, w_ref = refs[1 + n_add:3 + n_add]
    outs = refs[3 + n_add:]
    x = _residual_sum(x_ref, y_refs)
    if n_add:
        outs[0][...] = x
        outs = outs[1:]
    h_ref, z_refs = outs[0], outs[1:]
    h = _rms(x, g_ref[...]).astype(BF16)
    h_ref[...] = h
    off = 0
    for z_ref in z_refs:
        w = z_ref.shape[1]
        z_ref[...] = jnp.dot(h, w_ref[:, off:off + w], preferred_element_type=F32).astype(BF16)
        off += w


def _norm_inproj(x, adds, g, w, tm):
    T, D = x.shape
    widths = (ZA_W, ZB_W, ZC_W, ZD_W)
    row = lambda w_: pl.BlockSpec((tm, w_), lambda i: (i, 0))
    sum_spec, sum_shape = ([row(D)], [jax.ShapeDtypeStruct((T, D), F32)]) if adds else ([], [])
    outs = pl.pallas_call(
        functools.partial(_norm_inproj_body, n_add=len(adds)),
        grid=(T // tm,),
        in_specs=[row(D)] * (1 + len(adds)) + [_resident((1, D)), _resident((D, sum(widths)))],
        out_specs=sum_spec + [row(D)] + [row(w_) for w_ in widths],
        out_shape=sum_shape + [jax.ShapeDtypeStruct((T, D), BF16)]
        + [jax.ShapeDtypeStruct((T, w_), BF16) for w_ in widths],
        compiler_params=_cparams("parallel"),
        name="norm_inproj",
    )(x, *adds, g, w)
    return outs if adds else [x] + list(outs)


def _prep_body(za_ref, zd_ref, gq_ref, gkv_ref, wuq_ref, wukv_ref, gdq_ref, gdk_ref,
               ca_ref, sa_ref, cd_ref, sd_ref, qa_ref, ka_ref, va_ref, qd_ref, kd_ref):
    a_scale = (A_NOPE + A_ROPE) ** -0.5 * LOG2E
    d_scale = D_HEAD ** -0.5 * LOG2E
    ca, sa, cd, sd = ca_ref[...], sa_ref[...], cd_ref[...], sd_ref[...]
    cq = _rms(za_ref[:, :A_Q_LORA].astype(F32), gq_ref[...]).astype(BF16)
    q = jnp.dot(cq, wuq_ref[...], preferred_element_type=F32)
    ckv = _rms(za_ref[:, A_Q_LORA:A_Q_LORA + A_KV_LORA].astype(F32), gkv_ref[...]).astype(BF16)
    kv = jnp.dot(ckv, wukv_ref[...], preferred_element_type=F32)
    k_rope = _rope_lanes(za_ref[:, A_Q_LORA + A_KV_LORA:].astype(F32), ca, sa).astype(BF16)
    nope_w = A_HEADS * A_NOPE
    for h in range(A_HEADS):
        lo = h * A_QK_W
        qa_ref[:, lo:lo + LANES] = (q[:, h * LANES:(h + 1) * LANES] * a_scale).astype(BF16)
        q_rope = _rope_lanes(q[:, nope_w + h * LANES:nope_w + (h + 1) * LANES], ca, sa)
        qa_ref[:, lo + LANES:lo + 2 * LANES] = (q_rope * a_scale).astype(BF16)
        ka_ref[:, lo:lo + LANES] = kv[:, h * LANES:(h + 1) * LANES].astype(BF16)
        ka_ref[:, lo + LANES:lo + 2 * LANES] = k_rope
    va_ref[...] = kv[:, nope_w:].astype(BF16)
    for h in range(D_HEADS):
        xh = _rms(zd_ref[:, h * D_HEAD:(h + 1) * D_HEAD].astype(F32), gdq_ref[...])
        qd_ref[:, h * D_HEAD:(h + 1) * D_HEAD] = (_rope_lanes(xh, cd, sd) * d_scale).astype(BF16)
    k_off = D_HEADS * D_HEAD
    for h in range(D_KV_HEADS):
        xh = _rms(zd_ref[:, k_off + h * D_HEAD:k_off + (h + 1) * D_HEAD].astype(F32), gdk_ref[...])
        kd_ref[:, h * D_HEAD:(h + 1) * D_HEAD] = _rope_lanes(xh, cd, sd).astype(BF16)


def _prep(za, zd, gq, gkv, wuq, wukv, gdq, gdk, tabs, seq, tm):
    T = za.shape[0]
    per_seq = seq // tm
    row = lambda w_: pl.BlockSpec((tm, w_), lambda i: (i, 0))
    tab = pl.BlockSpec((tm, LANES), lambda i: (i % per_seq, 0))
    out_w = (A_HEADS * A_QK_W, A_HEADS * A_QK_W, A_HEADS * A_V, D_HEADS * D_HEAD, D_KV_HEADS * D_HEAD)
    return pl.pallas_call(
        _prep_body,
        grid=(T // tm,),
        in_specs=[row(ZA_W), row(ZD_W), _resident(gq.shape), _resident(gkv.shape), _resident(wuq.shape),
                  _resident(wukv.shape), _resident(gdq.shape), _resident(gdk.shape), tab, tab, tab, tab],
        out_specs=[row(w_) for w_ in out_w],
        out_shape=[jax.ShapeDtypeStruct((T, w_), BF16) for w_ in out_w],
        compiler_params=_cparams("parallel"),
        name="mixer_prep",
    )(za, zd, gq, gkv, wuq, wukv, gdq, gdk, *tabs)


def _qk(q, k):
    return lax.dot_general(q, k, (((1,), (1,)), ((), ())), preferred_element_type=F32)


def _lane_fold(fn, acc, x):
    for c in range(x.shape[1] // LANES):
        acc = fn(acc, x[:, c * LANES:(c + 1) * LANES])
    return acc


def _score_pass(q, k_ref, s_ref, tk, max_shift=None):
    tq, ts = s_ref.shape[1:]
    sub = tk // ts

    def step(j, mrun):
        ks = pl.multiple_of(j * tk, tk)
        s = _qk(q, k_ref[0, pl.ds(ks, tk), :])
        for c in range(sub):
            jj = j * sub + c
            sc = s[:, c * ts:(c + 1) * ts]
            s_ref[jj] = sc
            m_sub = _lane_fold(jnp.maximum, sc[:, :LANES], sc[:, LANES:])
            mrun = jnp.maximum(mrun, m_sub if max_shift is None else m_sub + max_shift(jj))
        return mrun

    return lax.fori_loop(0, k_ref.shape[1] // tk, step, jnp.full((tq, LANES), -jnp.inf, F32))


def _value_pass(v_ref, s_ref, tk, m_of):
    tq, ts = s_ref.shape[1:]
    sub = tk // ts
    dv = v_ref.shape[2]
    ones = jnp.ones((tk, LANES), BF16)

    def step(j, acc):
        ks = pl.multiple_of(j * tk, tk)
        p = []
        for c in range(sub):
            jj = j * sub + c
            m = m_of(jj)
            s = s_ref[jj]
            p += [jnp.exp2((s[:, i * LANES:(i + 1) * LANES] - m).astype(BF16)) for i in range(ts // LANES)]
        v_ext = jnp.concatenate([v_ref[0, pl.ds(ks, tk), :], ones], axis=1)
        return acc + jnp.dot(jnp.concatenate(p, axis=1), v_ext, preferred_element_type=F32)

    acc = lax.fori_loop(0, v_ref.shape[1] // tk, step, jnp.zeros((tq, dv + LANES), F32))
    return acc[:, :dv] / acc[:, dv:]


def _row_max(mrun):
    return jnp.broadcast_to(jnp.max(mrun, axis=-1, keepdims=True), mrun.shape)


def _flash_body(q_ref, k_ref, v_ref, o_ref, s_ref, *, tk, stack):
    tq = q_ref.shape[1]
    q_w = q_ref.shape[2] // stack
    q = jnp.concatenate([q_ref[0, :, c * q_w:(c + 1) * q_w] for c in range(stack)], axis=0)
    m = _row_max(_score_pass(q, k_ref, s_ref, tk))
    o = _value_pass(v_ref, s_ref, tk, lambda jj: m).astype(o_ref.dtype)
    o_ref[0] = jnp.concatenate([o[c * tq:(c + 1) * tq] for c in range(stack)], axis=1)


def _flash(q, k, v, *, kv_heads, stack, q_w, k_col0, v_col0, tq, ts, tk):
    B, S, _ = q.shape
    dv = LANES
    return pl.pallas_call(
        functools.partial(_flash_body, tk=tk, stack=stack),
        grid=(B, kv_heads, S // tq),
        in_specs=[pl.BlockSpec((1, tq, stack * q_w), lambda b, h, i: (b, i, h)),
                  pl.BlockSpec((1, S, q_w), lambda b, h, i: (b, 0, k_col0 + h)),
                  pl.BlockSpec((1, S, dv), lambda b, h, i: (b, 0, v_col0 + h))],
        out_specs=pl.BlockSpec((1, tq, stack * dv), lambda b, h, i: (b, i, h)),
        out_shape=jax.ShapeDtypeStruct((B, S, kv_heads * stack * dv), BF16),
        scratch_shapes=[pltpu.VMEM((S // ts, stack * tq, ts), F32)],
        compiler_params=_cparams("parallel", "parallel", "arbitrary"),
        name="flash_attn",
    )(q, k, v)


def _diff_body(lam_ref, cb_ref, q_ref, k_ref, v_ref, diag_ref, corner_ref, o_ref, s_ref, *, tk, out_scale):
    h, qi = pl.program_id(1), pl.program_id(2)
    n_sub = s_ref.shape[0]
    tq = q_ref.shape[1]
    nd = REL_MAX_DIST
    qf = q_ref[0].astype(F32) * (B_QK ** -0.5 * LOG2E)
    lane = lax.broadcasted_iota(jnp.int32, qf.shape, 1)
    c_lo, c_hi = cb_ref[0, h], cb_ref[1, h]
    side_shift = lambda jj: jnp.where(jj < qi, c_lo, jnp.where(jj > qi, c_hi, 0.0))
    tile_max = lambda s: _lane_fold(jnp.maximum, s[:, :LANES], s[:, LANES:])

    q = jnp.concatenate([jnp.where(lane < B_QK, qf, 0.0), jnp.where(lane >= B_QK, qf, 0.0)], axis=0).astype(BF16)
    far = lambda jj: jnp.where(jj < qi - 1, c_lo, jnp.where(jj > qi + 1, c_hi, -jnp.inf))
    mrun = _score_pass(q, k_ref, s_ref, tk, max_shift=far)

    diag = diag_ref[0]
    s = s_ref[qi] + jnp.concatenate([diag, diag], axis=0)
    s_ref[qi] = s
    mrun = jnp.maximum(mrun, tile_max(s))

    def neighbour(jj, rows0, cols0, corner, c):
        def fix(mrun):
            for r in (rows0, tq + rows0):
                s_ref[jj, r:r + nd, cols0:cols0 + nd] = s_ref[jj, r:r + nd, cols0:cols0 + nd] + corner
            return jnp.maximum(mrun, tile_max(s_ref[jj]) + c)
        return fix

    mrun = lax.cond(qi >= 1, neighbour(jnp.maximum(qi - 1, 0), 0, tq - nd, corner_ref[0, 0], c_lo),
                    lambda m_: m_, mrun)
    mrun = lax.cond(qi + 1 < n_sub, neighbour(jnp.minimum(qi + 1, n_sub - 1), tq - nd, 0, corner_ref[0, 1], c_hi),
                    lambda m_: m_, mrun)
    m = _row_max(mrun)
    o = _value_pass(v_ref, s_ref, tk, lambda jj: m - side_shift(jj))
    o = o[:tq] - lam_ref[0] * o[tq:]
    o_ref[0] = (_rms(o) * out_scale).astype(o_ref.dtype)


def _rel_bucket(rel):
    half = REL_BUCKETS // 2
    max_exact = half // 2
    ret = (rel > 0).astype(jnp.int32) * half
    n = jnp.abs(rel)
    large = max_exact + (jnp.log(jnp.maximum(n, 1).astype(F32) / max_exact)
                         / math.log(REL_MAX_DIST / max_exact) * (half - max_exact)).astype(jnp.int32)
    large = jnp.minimum(large, half - 1)
    return ret + jnp.where(n < max_exact, n, large)


def _diff_bias_tables(rel_bias, tq):
    nd = REL_MAX_DIST
    rb = rel_bias.astype(F32) * LOG2E
    far = jnp.stack([rb[REL_BUCKETS // 2 - 1], rb[REL_BUCKETS - 1]])

    def table(rel):
        bucket = _rel_bucket(rel)
        out = jnp.zeros((B_HEADS,) + rel.shape, F32)
        for b in range(REL_BUCKETS):
            out = jnp.where(bucket[None] == b, rb[b].reshape((B_HEADS,) + (1,) * rel.ndim), out)
        return out

    a, e = jnp.arange(tq), jnp.arange(nd)
    diag = table(a[None, :] - a[:, None])
    lo = table((tq - nd + e[None, :]) - tq - e[:, None]) - far[0][:, None, None]
    hi = table(e[None, :] + tq - (tq - nd + e[:, None])) - far[1][:, None, None]
    return diag, jnp.stack([lo, hi], axis=1), far


def _diff_attn(zb, lam, diag, corner, far, *, tq, tk, out_scale):
    B, S, _ = zb.shape
    nd = REL_MAX_DIST
    assert tq >= nd, "far key chunks must lie beyond the last distinct relative bucket"
    H = B_HEADS
    smem = pl.BlockSpec(memory_space=pltpu.SMEM)
    return pl.pallas_call(
        functools.partial(_diff_body, tk=tk, out_scale=out_scale),
        grid=(B, H, S // tq),
        in_specs=[smem, smem,
                  pl.BlockSpec((1, tq, B_V), lambda b, h, i: (b, i, h)),
                  pl.BlockSpec((1, S, B_V), lambda b, h, i: (b, 0, H + h)),
                  pl.BlockSpec((1, S, B_V), lambda b, h, i: (b, 0, 2 * H + h)),
                  pl.BlockSpec((1, tq, tq), lambda b, h, i: (h, 0, 0)),
                  pl.BlockSpec((1, 2, nd, nd), lambda b, h, i: (h, 0, 0, 0))],
        out_specs=pl.BlockSpec((1, tq, B_V), lambda b, h, i: (b, i, h)),
        out_shape=jax.ShapeDtypeStruct((B, S, H * B_V), BF16),
        scratch_shapes=[pltpu.VMEM((S // tq, 2 * tq, tq), F32)],
        compiler_params=_cparams("parallel", "parallel", "arbitrary"),
        name="diff_attn",
    )(lam, far, zb, zb, zb, diag, corner)


def _pool_body(u_ref, w_ref, sc_ref, o_ref, pad_ref, *, chunk):
    g = pl.program_id(1)
    S = u_ref.shape[1]
    pad_ref[0:HALO, :] = jnp.zeros((HALO, LANES), F32)
    pad_ref[HALO + S:, :] = jnp.zeros((HALO, LANES), F32)
    pad_ref[HALO:HALO + S, :] = u_ref[0].astype(F32)
    w_mat, sc = w_ref[0], sc_ref[...]

    def pooled(win):
        def body(c, _):
            r0 = pl.multiple_of(c * chunk, chunk)
            tot = pad_ref[pl.ds(r0 + HALO - win // 2, chunk), :]
            for j in range(1 - win // 2, win // 2):
                tot = tot + pad_ref[pl.ds(r0 + HALO + j, chunk), :]
            t = r0 + lax.broadcasted_iota(jnp.int32, (chunk, 1), 0)
            cnt = jnp.clip(t - win // 2 + win, 0, S) - jnp.clip(t - win // 2, 0, S)
            d = tot / cnt.astype(F32) - pad_ref[pl.ds(r0 + HALO, chunk), :]
            y = jnp.dot(d.astype(BF16), w_mat, preferred_element_type=F32) * sc
            o_ref[0, pl.ds(r0, chunk), :] = y.astype(o_ref.dtype)
            return 0
        lax.fori_loop(0, S // chunk, body, 0)

    for gi, win in enumerate(POOL_WINDOWS):
        pl.when(g == gi)(functools.partial(pooled, win))


def _pool(zc, w_pool, pool_scale, chunk):
    B, S, _ = zc.shape
    blk = pl.BlockSpec((1, S, LANES), lambda b, g: (b, 0, g))
    return pl.pallas_call(
        functools.partial(_pool_body, chunk=chunk),
        grid=(B, C_GROUPS),
        in_specs=[blk, pl.BlockSpec((1, C_GROUP_W, C_GROUP_W), lambda b, g: (g, 0, 0)),
                  pl.BlockSpec((1, LANES), lambda b, g: (0, g))],
        out_specs=blk,
        out_shape=jax.ShapeDtypeStruct((B, S, BRANCH_W), BF16),
        scratch_shapes=[pltpu.VMEM((S + 2 * HALO, LANES), F32)],
        compiler_params=_cparams("parallel", "arbitrary"),
        name="pool_mixer",
    )(zc, w_pool, pool_scale)


def _merge_body(h_ref, oa_ref, ob_ref, oc_ref, od_ref, x_ref, wg_ref, bg_ref, wl_ref, wo_ref, g2_ref,
                wr_ref, xn_ref, h2_ref, lg_ref, acc_ref):
    j = pl.program_id(1)

    @pl.when(j == 0)
    def _():
        acc_ref[...] = x_ref[...]

    h = h_ref[...]
    merged = None
    for b, o_ref in enumerate((oa_ref, ob_ref, oc_ref, od_ref)):
        gate = jax.nn.sigmoid(jnp.dot(h, wg_ref[0, b], preferred_element_type=F32) + bg_ref[b:b + 1, :])
        term = gate * jnp.dot(o_ref[...], wl_ref[0, b], preferred_element_type=F32)
        merged = term if merged is None else merged + term
    acc_ref[...] += jnp.dot(merged.astype(BF16), wo_ref[...], preferred_element_type=F32)

    @pl.when(j == pl.num_programs(1) - 1)
    def _():
        xn = acc_ref[...]
        xn_ref[...] = xn
        h2 = _rms(xn, g2_ref[...])
        h2_ref[...] = h2.astype(BF16)
        h_hi = h2.astype(BF16)
        h_lo = (h2 - h_hi.astype(F32)).astype(BF16)
        hi = jnp.dot(h_hi, wr_ref[...], preferred_element_type=F32)
        lg_ref[...] = (hi[:, :ROUTER_W] + hi[:, ROUTER_W:]
                       + jnp.dot(h_lo, wr_ref[:, :ROUTER_W], preferred_element_type=F32))


def _merge(h, outs, x, wg, bg, wl, wo, g2, wr, tm, tn):
    T, D = x.shape
    row = lambda w_: pl.BlockSpec((tm, w_), lambda i, j: (i, 0))
    return pl.pallas_call(
        _merge_body,
        grid=(T // tm, D // tn),
        in_specs=[row(D)] + [row(BRANCH_W)] * N_BRANCH + [
            row(D),
            pl.BlockSpec((1, N_BRANCH, D, tn), lambda i, j: (j, 0, 0, 0)),
            pl.BlockSpec((N_BRANCH, tn), lambda i, j: (0, j)),
            pl.BlockSpec((1, N_BRANCH, BRANCH_W, tn), lambda i, j: (j, 0, 0, 0)),
            pl.BlockSpec((tn, D), lambda i, j: (j, 0)),
            _resident((1, D)), _resident((D, 2 * ROUTER_W))],
        out_specs=[row(D), row(D), row(ROUTER_W)],
        out_shape=[jax.ShapeDtypeStruct((T, D), F32), jax.ShapeDtypeStruct((T, D), BF16),
                   jax.ShapeDtypeStruct((T, ROUTER_W), F32)],
        scratch_shapes=[pltpu.VMEM((tm, D), F32)],
        compiler_params=_cparams("parallel", "arbitrary"),
        name="gated_merge",
    )(h, *outs, x, wg, bg, wl, wo, g2, wr)


ROUTE_ROWS = 8


def _route_body(lg_ref, b_ref, o_ref):
    lg = lg_ref[...] + b_ref[...]
    lane = lax.broadcasted_iota(jnp.int32, lg.shape, 1).astype(F32)
    neg = -jnp.inf
    row_max = lambda v: jnp.max(v, axis=-1, keepdims=True)
    first_at = lambda v, m: jnp.min(jnp.where(v == m, lane, float(LANES)), axis=-1, keepdims=True)
    g_lg = jnp.where(lane < N_GROUPS, lg, neg)
    g_max = row_max(g_lg)
    g_top = first_at(g_lg, g_max)
    pg_top = 1.0 / jnp.sum(jnp.exp(g_lg - g_max), axis=-1, keepdims=True)
    e_lo = N_GROUPS + EXP_PER_GROUP * g_top
    e_lg = jnp.where(jnp.logical_and(lane >= e_lo, lane < e_lo + EXP_PER_GROUP), lg, neg)
    picks, maxes = [], []
    for _ in range(TOP_K):
        m = row_max(e_lg)
        i = first_at(e_lg, m)
        picks.append(i)
        maxes.append(m)
        e_lg = jnp.where(lane == i, neg, e_lg)
    w = [jnp.exp(m - maxes[0]) for m in maxes]
    w_sum = functools.reduce(lambda a, b: a + b, w)
    cols = [i - N_GROUPS for i in picks] + [pg_top * wk / w_sum for wk in w]
    tile = jnp.zeros(lg.shape, F32)
    for c, v in enumerate(cols):
        tile = jnp.where(lane == c, v, tile)
    o_ref[...] = tile.T[:ROUTE_ROWS]


def _route(logits, bias, tm):
    T = logits.shape[0]
    return pl.pallas_call(
        _route_body, grid=(T // tm,),
        in_specs=[pl.BlockSpec((tm, ROUTER_W), lambda i: (i, 0)), _resident((1, ROUTER_W))],
        out_specs=pl.BlockSpec((ROUTE_ROWS, tm), lambda i: (0, i)),
        out_shape=jax.ShapeDtypeStruct((ROUTE_ROWS, T), F32), compiler_params=_cparams("parallel"), name="router",
    )(logits, bias)


def _dispatch(routed, tb):
    T = routed.shape[1]
    A = T * TOP_K
    flat_e = routed[:TOP_K].astype(jnp.int32).reshape(A)
    flat_w = routed[TOP_K:2 * TOP_K].reshape(A)
    iota = jnp.arange(A, dtype=jnp.int32)
    se, order, sw = lax.sort((flat_e, iota, flat_w), num_keys=1, is_stable=True)
    experts = jnp.arange(N_EXPERTS, dtype=jnp.int32)
    counts = jnp.sum((flat_e[:, None] == experts[None, :]).astype(jnp.int32), axis=0)
    start = jnp.cumsum(counts) - counts
    padded = (counts + tb - 1) // tb * tb
    pend = jnp.cumsum(padded)
    pstart = pend - padded
    n_blocks = -(-A // tb) + N_EXPERTS
    P = n_blocks * tb
    blk_first = jnp.arange(n_blocks, dtype=jnp.int32) * tb
    blk_exp = jnp.minimum(jnp.sum((pend[None, :] <= blk_first[:, None]).astype(jnp.int32), axis=1), N_EXPERTS - 1)
    n_valid = (pend[-1] // tb).astype(jnp.int32).reshape(1)
    slot = jnp.arange(P, dtype=jnp.int32)
    before = (pend[None, :] <= slot[:, None]).astype(jnp.int32)
    shift = jnp.sum(before * (padded - counts)[None, :], axis=1)
    in_pad = jnp.any(jnp.logical_and(slot[:, None] >= (pstart + counts)[None, :], slot[:, None] < pend[None, :]),
                     axis=1)
    valid = jnp.logical_and(jnp.logical_not(in_pad), slot < pend[-1])
    src = jnp.where(valid, slot - shift, slot % A)
    slot_tok = order[src] % T
    slot_w = jnp.where(valid, sw[src], 0.0)
    dest = iota + jnp.sum((se[:, None] == experts[None, :]).astype(jnp.int32) * (pstart - start)[None, :], axis=1)
    _, slot_of = lax.sort((order, dest), num_keys=1)
    return slot_tok, slot_w, blk_exp, n_valid, slot_of.reshape(TOP_K, T)


def _expert_body(be_ref, nv_ref, x_ref, sw_ref, wg_ref, wu_ref, wd_ref, o_ref, wg_s, wu_s, wd_s):
    i = pl.program_id(0)

    @pl.when(jnp.logical_or(i == 0, be_ref[i] != be_ref[jnp.maximum(i - 1, 0)]))
    def _():
        wg_s[...] = wg_ref[0, 0].astype(BF16)
        wu_s[...] = wu_ref[0, 0].astype(BF16)
        wd_s[...] = wd_ref[0, 0].astype(BF16)

    @pl.when(i < nv_ref[0])
    def _():
        x = x_ref[...]
        g = jnp.dot(x, wg_s[...], preferred_element_type=F32)
        u = jnp.dot(x, wu_s[...], preferred_element_type=F32)
        a = (g * jax.nn.sigmoid(g) * u).astype(BF16)
        y = jnp.dot(a, wd_s[...], preferred_element_type=F32)
        sw = sw_ref[...]
        o_ref[...] = jnp.concatenate([y[:, c * LANES:(c + 1) * LANES] * sw for c in range(y.shape[1] // LANES)],
                                     axis=1).astype(o_ref.dtype)

    @pl.when(i >= nv_ref[0])
    def _():
        o_ref[...] = jnp.zeros(o_ref.shape, o_ref.dtype)


def _expert_ffn(xs, slot_w, blk_exp, n_valid, wg, wu, wd, layer, tb):
    P, D = xs.shape
    grid_spec = pltpu.PrefetchScalarGridSpec(
        num_scalar_prefetch=2,
        grid=(P // tb,),
        in_specs=[pl.BlockSpec((tb, D), lambda i, be, nv: (i, 0)),
                  pl.BlockSpec((tb, LANES), lambda i, be, nv: (i, 0)),
                  pl.BlockSpec((1, 1, D, D_EXPERT), lambda i, be, nv: (layer, be[i], 0, 0)),
                  pl.BlockSpec((1, 1, D, D_EXPERT), lambda i, be, nv: (layer, be[i], 0, 0)),
                  pl.BlockSpec((1, 1, D_EXPERT, D), lambda i, be, nv: (layer, be[i], 0, 0))],
        out_specs=pl.BlockSpec((tb, D), lambda i, be, nv: (i, 0)),
        scratch_shapes=[pltpu.VMEM((D, D_EXPERT), BF16), pltpu.VMEM((D, D_EXPERT), BF16),
                        pltpu.VMEM((D_EXPERT, D), BF16)],
    )
    return pl.pallas_call(
        _expert_body,
        grid_spec=grid_spec,
        out_shape=jax.ShapeDtypeStruct((P, D), BF16),
        compiler_params=_cparams("arbitrary"),
        name="expert_ffn",
    )(blk_exp, n_valid, xs, jnp.broadcast_to(slot_w[:, None], (P, LANES)), wg, wu, wd)


def _final_norm_body(*refs):
    x_ref, y_refs, (g_ref, o_ref) = refs[0], refs[1:-2], refs[-2:]
    o_ref[...] = _rms(_residual_sum(x_ref, y_refs), g_ref[...])


def _final_norm(x, adds, g, row0, rows, tm):
    D = x.shape[1]
    first = row0 // tm
    src = pl.BlockSpec((tm, D), lambda i: (first + i, 0))
    return pl.pallas_call(
        _final_norm_body, grid=(rows // tm,),
        in_specs=[src] * (1 + len(adds)) + [_resident((1, D))],
        out_specs=pl.BlockSpec((tm, D), lambda i: (i, 0)),
        out_shape=jax.ShapeDtypeStruct((rows, D), F32), compiler_params=_cparams("parallel"), name="final_norm",
    )(x, *adds, g)


def _rope_tables(seq):
    def angles(pos, dim):
        inv = 1.0 / (ROPE_THETA ** (jnp.arange(0, dim, 2, dtype=F32) / dim))
        ang = pos.astype(F32)[:, None] * inv[None, :]
        return jnp.cos(ang), jnp.sin(ang)
    pos = jnp.arange(seq)
    c1, s1 = angles(pos, A_ROPE)
    cr, sr = angles(pos // GRID_W, D_HEAD // 2)
    cc, sc = angles(pos % GRID_W, D_HEAD // 2)
    pad1, pad0 = jnp.ones((seq, ROPE_PAD), F32), jnp.zeros((seq, ROPE_PAD), F32)
    return (jnp.concatenate([c1, c1, pad1], 1), jnp.concatenate([-s1, s1, pad0], 1),
            jnp.concatenate([cr, cr, cc, cc], 1), jnp.concatenate([-sr, sr, -sc, sc], 1))


def _layer_weights(l, w_in, w_uq, w_ukv, w_route_group, w_route_expert):
    D = D_MODEL
    k_r_end = A_Q_LORA + A_KV_LORA + A_ROPE
    win = jnp.concatenate([w_in[l][:, :k_r_end], jnp.zeros((D, ROPE_PAD), F32), w_in[l][:, k_r_end:]], 1)
    uq = w_uq[l].reshape(A_Q_LORA, A_HEADS, A_NOPE + A_ROPE)
    uq_rope = jnp.pad(uq[:, :, A_NOPE:], ((0, 0), (0, 0), (0, ROPE_PAD)))
    wuq = jnp.concatenate([uq[:, :, :A_NOPE].reshape(A_Q_LORA, -1), uq_rope.reshape(A_Q_LORA, -1)], 1)
    ukv = w_ukv[l].reshape(A_KV_LORA, A_HEADS, A_NOPE + A_V)
    wukv = jnp.concatenate([ukv[:, :, :A_NOPE].reshape(A_KV_LORA, -1), ukv[:, :, A_NOPE:].reshape(A_KV_LORA, -1)], 1)
    wr = jnp.concatenate([w_route_group[l], w_route_expert[l],
                          jnp.zeros((D, ROUTER_W - N_GROUPS - N_EXPERTS), F32)], 1)
    wr_hi = wr.astype(BF16)
    wr_lo = (wr - wr_hi.astype(F32)).astype(BF16)
    return win.astype(BF16), wuq.astype(BF16), wukv.astype(BF16), jnp.concatenate([wr_hi, wr_lo], 1)


def _column_tiled(w, tn):
    nb, k, n = w.shape
    return jnp.transpose(w.astype(BF16).reshape(nb, k, n // tn, tn), (2, 0, 1, 3))


def kernel(x_prompt, x_sample, norm1_g, w_in, q_norm_g, kv_norm_g, w_uq, w_ukv, lam_q1, lam_k1, lam_q2, lam_k2,
           rel_bias, w_pool, pool_scale, qk_norm_q, qk_norm_k, w_lift, w_gate, b_gate, w_out, norm2_g,
           w_route_group, b_route_group, w_route_expert, b_route_expert, w_exp_gate, w_exp_up, w_exp_down,
           final_g):
    assert x_prompt.shape[1:] == x_sample.shape[1:], "both request groups must share (seq, d_model)"
    n_prompt = x_prompt.shape[0]
    x = jnp.concatenate([x_prompt, x_sample], axis=0)
    B, S, D = x.shape
    T = B * S
    t = _tiles(S, T)
    x = x.reshape(T, D)
    depth = w_in.shape[0]
    tabs = _rope_tables(S)
    diag, corner, far = _diff_bias_tables(rel_bias, t["tq"])
    row2 = lambda v: v.reshape(1, -1).astype(F32)

    moe_out = ()
    for l in range(depth):
        lam_init = 0.8 - 0.6 * math.exp(-0.3 * l)
        lam = (jnp.exp(jnp.sum(lam_q1[l] * lam_k1[l])) - jnp.exp(jnp.sum(lam_q2[l] * lam_k2[l])) + lam_init)
        win, wuq, wukv, wr = _layer_weights(l, w_in, w_uq, w_ukv, w_route_group, w_route_expert)

        x, h, za, zb, zc, zd = _norm_inproj(x, moe_out, row2(norm1_g[l]), win, t["tm"])
        qa, ka, va, qd, kd = _prep(za, zd, row2(q_norm_g[l]), row2(kv_norm_g[l]), wuq, wukv,
                                   row2(qk_norm_q[l]), row2(qk_norm_k[l]), tabs, S, t["tm"])
        seq3 = lambda a: a.reshape(B, S, a.shape[-1])
        rep = D_HEADS // D_KV_HEADS
        o_a = _flash(seq3(qa), seq3(ka), seq3(va), kv_heads=A_HEADS, stack=1, q_w=A_QK_W, k_col0=0, v_col0=0,
                     tq=t["rows"], ts=t["tq"], tk=t["tk"])
        o_b = _diff_attn(seq3(zb), lam.reshape(1).astype(F32), diag, corner, far, tq=t["tq"], tk=t["tk"],
                         out_scale=1.0 - lam_init)
        o_c = _pool(seq3(zc), w_pool[l].astype(BF16), row2(pool_scale[l]), t["pool_chunk"])
        o_d = _flash(seq3(qd), seq3(kd), seq3(zd), kv_heads=D_KV_HEADS, stack=rep, q_w=D_HEAD, k_col0=0,
                     v_col0=D_HEADS + D_KV_HEADS, tq=t["rows"] // rep, ts=t["tq"], tk=t["tk"])
        outs = [o.reshape(T, BRANCH_W) for o in (o_a, o_b, o_c, o_d)]
        x, h2, logits = _merge(h, outs, x, _column_tiled(w_gate[l], t["tn"]), b_gate[l].astype(F32),
                               _column_tiled(w_lift[l], t["tn"]), w_out[l].astype(BF16), row2(norm2_g[l]), wr,
                               t["tm"], t["tn"])

        route_bias = jnp.concatenate([b_route_group[l], b_route_expert[l],
                                      jnp.zeros((ROUTER_W - N_GROUPS - N_EXPERTS,), F32)]).reshape(1, ROUTER_W)
        routed = _route(logits, route_bias.astype(F32), t["tm"])
        slot_tok, slot_w, blk_exp, n_valid, slot_of = _dispatch(routed, t["tb"])
        ys = _expert_ffn(h2[slot_tok], slot_w, blk_exp, n_valid, w_exp_gate, w_exp_up, w_exp_down, l, t["tb"])
        moe_out = tuple(ys[slot_of[k]] for k in range(TOP_K))

    t_prompt = n_prompt * S
    y_prompt = _final_norm(x, moe_out, row2(final_g), 0, t_prompt, t["tm"]).reshape(n_prompt, S, D)
    y_sample = _final_norm(x, moe_out, row2(final_g), t_prompt, T - t_prompt, t["tm"]).reshape(B - n_prompt, S, D)
    return y_prompt, y_sample
```

```python
import functools
import math

import jax
import jax.numpy as jnp
from jax import lax
from jax.experimental import pallas as pl
from jax.experimental.pallas import tpu as pltpu

F32 = jnp.float32
BF16 = jnp.bfloat16
LOG2E = 1.4426950408889634

D_MODEL = 2048
GRID_W = 64
BRANCH_W = 512
N_BRANCH = 4
ROPE_THETA = 10000.0
EPS = 1e-6
A_HEADS, A_Q_LORA, A_KV_LORA, A_NOPE, A_ROPE, A_V = 4, 384, 128, 128, 64, 128
B_HEADS, B_QK, B_V = 4, 64, 128
REL_BUCKETS, REL_MAX_DIST = 32, 128
POOL_WINDOWS = (2, 4, 8, 16)
C_GROUPS, C_GROUP_W = 4, 128
D_HEADS, D_KV_HEADS, D_HEAD = 4, 2, 128
N_GROUPS, EXP_PER_GROUP, N_EXPERTS, TOP_K, D_EXPERT = 4, 8, 32, 2, 512

LANES = 128
ROPE_PAD = LANES - A_ROPE
ZA_W = A_Q_LORA + A_KV_LORA + LANES
ZB_W = 3 * B_HEADS * B_V
ZC_W = BRANCH_W
ZD_W = (D_HEADS + 2 * D_KV_HEADS) * D_HEAD
A_QK_W = 2 * LANES
ROUTER_W = LANES
HALO = 16
VMEM_LIMIT = 56 * 2 ** 20


def _tiles(seq, tokens):
    def pick(n, pref):
        t = min(pref, n)
        while n % t:
            t //= 2
        return t
    return dict(
        tm=pick(seq, 512),
        tq=pick(seq, 512),
        rows=pick(seq, 1024),
        tk=pick(seq, 4096),
        tm_merge=pick(tokens, 1024),
        tn=512,
        tb=256,
        pool_chunk=pick(seq, 1024),
    )


def _cparams(*sem):
    return pltpu.CompilerParams(dimension_semantics=sem, vmem_limit_bytes=VMEM_LIMIT)


def _resident(shape):
    zeros = (0,) * len(shape)
    return pl.BlockSpec(shape, lambda *_: zeros, pipeline_mode=pl.Buffered(1))


def _rms(x, g=None):
    y = x * lax.rsqrt(jnp.mean(x * x, axis=-1, keepdims=True) + EPS)
    return y if g is None else y * g


def _rope_lanes(x, c, s):
    lane = lax.broadcasted_iota(jnp.int32, x.shape, 1)
    first_half = (lane & 32) == 0
    partner = jnp.where(first_half, pltpu.roll(x, 96, 1), pltpu.roll(x, 32, 1))
    return x * c + partner * s


def _residual_sum(x_ref, y_refs):
    x = x_ref[...]
    for y_ref in y_refs:
        x = x + y_ref[...].astype(F32)
    return x


def _norm_inproj_body(*refs, n_add, first_rows):
    x_ref, y_refs = refs[0], refs[1:1 + n_add]
    g_ref, w_ref = refs[1 + n_add:3 + n_add]
    outs = refs[3 + n_add:]
    if first_rows is None:
        x = _residual_sum(x_ref, y_refs)
    else:
        x = jnp.where(pl.program_id(0) < first_rows, x_ref[...], y_refs[0][...])
    if n_add:
        outs[0][...] = x
        outs = outs[1:]
    h_ref, z_refs = outs[0], outs[1:]
    h = _rms(x, g_ref[...]).astype(BF16)
    h_ref[...] = h
    off = 0
    for z_ref in z_refs:
        w = z_ref.shape[1]
        z_ref[...] = jnp.dot(h, w_ref[:, off:off + w], preferred_element_type=F32).astype(BF16)
        off += w


def _norm_inproj(x, adds, g, w, tm):
    T, D = x.shape
    widths = (ZA_W, ZB_W, ZC_W, ZD_W)
    row = lambda w_: pl.BlockSpec((tm, w_), lambda i: (i, 0))
    sum_spec, sum_shape = ([row(D)], [jax.ShapeDtypeStruct((T, D), F32)]) if adds else ([], [])
    outs = pl.pallas_call(
        functools.partial(_norm_inproj_body, n_add=len(adds), first_rows=None),
        grid=(T // tm,),
        in_specs=[row(D)] * (1 + len(adds)) + [_resident((1, D)), _resident((D, sum(widths)))],
        out_specs=sum_spec + [row(D)] + [row(w_) for w_ in widths],
        out_shape=sum_shape + [jax.ShapeDtypeStruct((T, D), BF16)]
        + [jax.ShapeDtypeStruct((T, w_), BF16) for w_ in widths],
        compiler_params=_cparams("parallel"),
        name="norm_inproj",
    )(x, *adds, g, w)
    return outs if adds else [x] + list(outs)


def _norm_inproj_first(xa, xb, g, w, tm):
    (Ta, D), Tb = xa.shape, xb.shape[0]
    T, na = Ta + Tb, Ta // tm
    widths = (ZA_W, ZB_W, ZC_W, ZD_W)
    row = lambda w_: pl.BlockSpec((tm, w_), lambda i: (i, 0))
    return pl.pallas_call(
        functools.partial(_norm_inproj_body, n_add=1, first_rows=na),
        grid=(T // tm,),
        in_specs=[pl.BlockSpec((tm, D), lambda i: (jnp.minimum(i, na - 1), 0)),
                  pl.BlockSpec((tm, D), lambda i: (jnp.maximum(i - na, 0), 0)),
                  _resident((1, D)), _resident((D, sum(widths)))],
        out_specs=[row(D), row(D)] + [row(w_) for w_ in widths],
        out_shape=[jax.ShapeDtypeStruct((T, D), F32), jax.ShapeDtypeStruct((T, D), BF16)]
        + [jax.ShapeDtypeStruct((T, w_), BF16) for w_ in widths],
        compiler_params=_cparams("parallel"),
        name="norm_inproj",
    )(xa, xb, g, w)


def _prep_body(za_ref, zd_ref, gq_ref, gkv_ref, wuq_ref, wukv_ref, gdq_ref, gdk_ref,
               ca_ref, sa_ref, cd_ref, sd_ref, qa_ref, ka_ref, va_ref, qd_ref, kd_ref):
    a_scale = (A_NOPE + A_ROPE) ** -0.5 * LOG2E
    d_scale = D_HEAD ** -0.5 * LOG2E
    ca, sa, cd, sd = ca_ref[...], sa_ref[...], cd_ref[...], sd_ref[...]
    cq = _rms(za_ref[:, :A_Q_LORA].astype(F32), gq_ref[...]).astype(BF16)
    q = jnp.dot(cq, wuq_ref[...], preferred_element_type=F32)
    ckv = _rms(za_ref[:, A_Q_LORA:A_Q_LORA + A_KV_LORA].astype(F32), gkv_ref[...]).astype(BF16)
    kv = jnp.dot(ckv, wukv_ref[...], preferred_element_type=F32)
    k_rope = _rope_lanes(za_ref[:, A_Q_LORA + A_KV_LORA:].astype(F32), ca, sa).astype(BF16)
    nope_w = A_HEADS * A_NOPE
    for h in range(A_HEADS):
        lo = h * A_QK_W
        qa_ref[:, lo:lo + LANES] = (q[:, h * LANES:(h + 1) * LANES] * a_scale).astype(BF16)
        q_rope = _rope_lanes(q[:, nope_w + h * LANES:nope_w + (h + 1) * LANES], ca, sa)
        qa_ref[:, lo + LANES:lo + 2 * LANES] = (q_rope * a_scale).astype(BF16)
        ka_ref[:, lo:lo + LANES] = kv[:, h * LANES:(h + 1) * LANES].astype(BF16)
        ka_ref[:, lo + LANES:lo + 2 * LANES] = k_rope
    va_ref[...] = kv[:, nope_w:].astype(BF16)
    for h in range(D_HEADS):
        xh = _rms(zd_ref[:, h * D_HEAD:(h + 1) * D_HEAD].astype(F32), gdq_ref[...])
        qd_ref[:, h * D_HEAD:(h + 1) * D_HEAD] = (_rope_lanes(xh, cd, sd) * d_scale).astype(BF16)
    k_off = D_HEADS * D_HEAD
    for h in range(D_KV_HEADS):
        xh = _rms(zd_ref[:, k_off + h * D_HEAD:k_off + (h + 1) * D_HEAD].astype(F32), gdk_ref[...])
        kd_ref[:, h * D_HEAD:(h + 1) * D_HEAD] = _rope_lanes(xh, cd, sd).astype(BF16)


def _prep(za, zd, gq, gkv, wuq, wukv, gdq, gdk, tabs, seq, tm):
    T = za.shape[0]
    per_seq = seq // tm
    row = lambda w_: pl.BlockSpec((tm, w_), lambda i: (i, 0))
    tab = pl.BlockSpec((tm, LANES), lambda i: (i % per_seq, 0))
    out_w = (A_HEADS * A_QK_W, A_HEADS * A_QK_W, A_HEADS * A_V, D_HEADS * D_HEAD, D_KV_HEADS * D_HEAD)
    return pl.pallas_call(
        _prep_body,
        grid=(T // tm,),
        in_specs=[row(ZA_W), row(ZD_W), _resident(gq.shape), _resident(gkv.shape), _resident(wuq.shape),
                  _resident(wukv.shape), _resident(gdq.shape), _resident(gdk.shape), tab, tab, tab, tab],
        out_specs=[row(w_) for w_ in out_w],
        out_shape=[jax.ShapeDtypeStruct((T, w_), BF16) for w_ in out_w],
        compiler_params=_cparams("parallel"),
        name="mixer_prep",
    )(za, zd, gq, gkv, wuq, wukv, gdq, gdk, *tabs)


def _qk(q, k):
    return lax.dot_general(q, k, (((1,), (1,)), ((), ())), preferred_element_type=F32)


def _lane_fold(fn, acc, x):
    for c in range(x.shape[1] // LANES):
        acc = fn(acc, x[:, c * LANES:(c + 1) * LANES])
    return acc


def _score_pass(q, k_ref, s_ref, tk, max_shift=None):
    tq, ts = s_ref.shape[1:]
    sub = tk // ts

    def step(j, mrun):
        ks = pl.multiple_of(j * tk, tk)
        s = _qk(q, k_ref[0, pl.ds(ks, tk), :])
        for c in range(sub):
            jj = j * sub + c
            sc = s[:, c * ts:(c + 1) * ts]
            s_ref[jj] = sc
            m_sub = _lane_fold(jnp.maximum, sc[:, :LANES], sc[:, LANES:])
            mrun = jnp.maximum(mrun, m_sub if max_shift is None else m_sub + max_shift(jj))
        return mrun

    return lax.fori_loop(0, k_ref.shape[1] // tk, step, jnp.full((tq, LANES), -jnp.inf, F32))


def _value_pass(v_ref, s_ref, tk, m_of):
    tq, ts = s_ref.shape[1:]
    sub = tk // ts
    dv = v_ref.shape[2]
    ones = jnp.ones((tk, LANES), BF16)

    def step(j, acc):
        ks = pl.multiple_of(j * tk, tk)
        p = []
        for c in range(sub):
            jj = j * sub + c
            m = m_of(jj)
            s = s_ref[jj]
            p += [jnp.exp2((s[:, i * LANES:(i + 1) * LANES] - m).astype(BF16)) for i in range(ts // LANES)]
        v_ext = jnp.concatenate([v_ref[0, pl.ds(ks, tk), :], ones], axis=1)
        return acc + jnp.dot(jnp.concatenate(p, axis=1), v_ext, preferred_element_type=F32)

    acc = lax.fori_loop(0, v_ref.shape[1] // tk, step, jnp.zeros((tq, dv + LANES), F32))
    return acc[:, :dv] / acc[:, dv:]


def _row_max(mrun):
    return jnp.broadcast_to(jnp.max(mrun, axis=-1, keepdims=True), mrun.shape)


def _flash_body(q_ref, k_ref, v_ref, o_ref, s_ref, *, tk, stack):
    tq = q_ref.shape[1]
    q_w = q_ref.shape[2] // stack
    q = jnp.concatenate([q_ref[0, :, c * q_w:(c + 1) * q_w] for c in range(stack)], axis=0)
    m = _row_max(_score_pass(q, k_ref, s_ref, tk))
    o = _value_pass(v_ref, s_ref, tk, lambda jj: m).astype(o_ref.dtype)
    o_ref[0] = jnp.concatenate([o[c * tq:(c + 1) * tq] for c in range(stack)], axis=1)


def _flash(q, k, v, *, kv_heads, stack, q_w, k_col0, v_col0, tq, ts, tk):
    B, S, _ = q.shape
    dv = LANES
    return pl.pallas_call(
        functools.partial(_flash_body, tk=tk, stack=stack),
        grid=(B, kv_heads, S // tq),
        in_specs=[pl.BlockSpec((1, tq, stack * q_w), lambda b, h, i: (b, i, h)),
                  pl.BlockSpec((1, S, q_w), lambda b, h, i: (b, 0, k_col0 + h)),
                  pl.BlockSpec((1, S, dv), lambda b, h, i: (b, 0, v_col0 + h))],
        out_specs=pl.BlockSpec((1, tq, stack * dv), lambda b, h, i: (b, i, h)),
        out_shape=jax.ShapeDtypeStruct((B, S, kv_heads * stack * dv), BF16),
        scratch_shapes=[pltpu.VMEM((S // ts, stack * tq, ts), F32)],
        compiler_params=_cparams("parallel", "parallel", "arbitrary"),
        name="flash_attn",
    )(q, k, v)


def _diff_body(lam_ref, cb_ref, q_ref, k_ref, v_ref, diag_ref, corner_ref, o_ref, s_ref, *, tk, out_scale):
    h, qi = pl.program_id(1), pl.program_id(2)
    n_sub = s_ref.shape[0]
    tq = q_ref.shape[1]
    nd = REL_MAX_DIST
    qf = q_ref[0].astype(F32) * (B_QK ** -0.5 * LOG2E)
    lane = lax.broadcasted_iota(jnp.int32, qf.shape, 1)
    c_lo, c_hi = cb_ref[0, h], cb_ref[1, h]
    side_shift = lambda jj: jnp.where(jj < qi, c_lo, jnp.where(jj > qi, c_hi, 0.0))
    tile_max = lambda s: _lane_fold(jnp.maximum, s[:, :LANES], s[:, LANES:])

    q = jnp.concatenate([jnp.where(lane < B_QK, qf, 0.0), jnp.where(lane >= B_QK, qf, 0.0)], axis=0).astype(BF16)
    far = lambda jj: jnp.where(jj < qi - 1, c_lo, jnp.where(jj > qi + 1, c_hi, -jnp.inf))
    mrun = _score_pass(q, k_ref, s_ref, tk, max_shift=far)

    diag = diag_ref[0]
    s = s_ref[qi] + jnp.concatenate([diag, diag], axis=0)
    s_ref[qi] = s
    mrun = jnp.maximum(mrun, tile_max(s))

    def neighbour(jj, rows0, cols0, corner, c):
        def fix(mrun):
            for r in (rows0, tq + rows0):
                s_ref[jj, r:r + nd, cols0:cols0 + nd] = s_ref[jj, r:r + nd, cols0:cols0 + nd] + corner
            return jnp.maximum(mrun, tile_max(s_ref[jj]) + c)
        return fix

    mrun = lax.cond(qi >= 1, neighbour(jnp.maximum(qi - 1, 0), 0, tq - nd, corner_ref[0, 0], c_lo),
                    lambda m_: m_, mrun)
    mrun = lax.cond(qi + 1 < n_sub, neighbour(jnp.minimum(qi + 1, n_sub - 1), tq - nd, 0, corner_ref[0, 1], c_hi),
                    lambda m_: m_, mrun)
    m = _row_max(mrun)
    o = _value_pass(v_ref, s_ref, tk, lambda jj: m - side_shift(jj))
    o = o[:tq] - lam_ref[0] * o[tq:]
    o_ref[0] = (_rms(o) * out_scale).astype(o_ref.dtype)


def _rel_bucket(rel):
    half = REL_BUCKETS // 2
    max_exact = half // 2
    ret = (rel > 0).astype(jnp.int32) * half
    n = jnp.abs(rel)
    large = max_exact + (jnp.log(jnp.maximum(n, 1).astype(F32) / max_exact)
                         / math.log(REL_MAX_DIST / max_exact) * (half - max_exact)).astype(jnp.int32)
    large = jnp.minimum(large, half - 1)
    return ret + jnp.where(n < max_exact, n, large)


def _diff_bias_tables(rel_bias, tq):
    nd = REL_MAX_DIST
    rb = rel_bias.astype(F32) * LOG2E
    far = jnp.stack([rb[REL_BUCKETS // 2 - 1], rb[REL_BUCKETS - 1]])

    def table(rel):
        bucket = _rel_bucket(rel)
        out = jnp.zeros((B_HEADS,) + rel.shape, F32)
        for b in range(REL_BUCKETS):
            out = jnp.where(bucket[None] == b, rb[b].reshape((B_HEADS,) + (1,) * rel.ndim), out)
        return out

    a, e = jnp.arange(tq), jnp.arange(nd)
    diag = table(a[None, :] - a[:, None])
    lo = table((tq - nd + e[None, :]) - tq - e[:, None]) - far[0][:, None, None]
    hi = table(e[None, :] + tq - (tq - nd + e[:, None])) - far[1][:, None, None]
    return diag, jnp.stack([lo, hi], axis=1), far


def _diff_attn(zb, lam, diag, corner, far, *, tq, tk, out_scale):
    B, S, _ = zb.shape
    nd = REL_MAX_DIST
    assert tq >= nd, "far key chunks must lie beyond the last distinct relative bucket"
    H = B_HEADS
    smem = pl.BlockSpec(memory_space=pltpu.SMEM)
    return pl.pallas_call(
        functools.partial(_diff_body, tk=tk, out_scale=out_scale),
        grid=(B, H, S // tq),
        in_specs=[smem, smem,
                  pl.BlockSpec((1, tq, B_V), lambda b, h, i: (b, i, h)),
                  pl.BlockSpec((1, S, B_V), lambda b, h, i: (b, 0, H + h)),
                  pl.BlockSpec((1, S, B_V), lambda b, h, i: (b, 0, 2 * H + h)),
                  pl.BlockSpec((1, tq, tq), lambda b, h, i: (h, 0, 0)),
                  pl.BlockSpec((1, 2, nd, nd), lambda b, h, i: (h, 0, 0, 0))],
        out_specs=pl.BlockSpec((1, tq, B_V), lambda b, h, i: (b, i, h)),
        out_shape=jax.ShapeDtypeStruct((B, S, H * B_V), BF16),
        scratch_shapes=[pltpu.VMEM((S // tq, 2 * tq, tq), F32)],
        compiler_params=_cparams("parallel", "parallel", "arbitrary"),
        name="diff_attn",
    )(lam, far, zb, zb, zb, diag, corner)


def _pool_body(u_ref, w_ref, sc_ref, o_ref, pad_ref, *, chunk):
    g = pl.program_id(1)
    S = u_ref.shape[1]
    pad_ref[0:HALO, :] = jnp.zeros((HALO, LANES), F32)
    pad_ref[HALO + S:, :] = jnp.zeros((HALO, LANES), F32)
    pad_ref[HALO:HALO + S, :] = u_ref[0].astype(F32)
    w_mat, sc = w_ref[0], sc_ref[...]

    def pooled(win):
        def body(c, _):
            r0 = pl.multiple_of(c * chunk, chunk)
            tot = pad_ref[pl.ds(r0 + HALO - win // 2, chunk), :]
            for j in range(1 - win // 2, win // 2):
                tot = tot + pad_ref[pl.ds(r0 + HALO + j, chunk), :]
            t = r0 + lax.broadcasted_iota(jnp.int32, (chunk, 1), 0)
            cnt = jnp.clip(t - win // 2 + win, 0, S) - jnp.clip(t - win // 2, 0, S)
            d = tot / cnt.astype(F32) - pad_ref[pl.ds(r0 + HALO, chunk), :]
            y = jnp.dot(d.astype(BF16), w_mat, preferred_element_type=F32) * sc
            o_ref[0, pl.ds(r0, chunk), :] = y.astype(o_ref.dtype)
            return 0
        lax.fori_loop(0, S // chunk, body, 0)

    for gi, win in enumerate(POOL_WINDOWS):
        pl.when(g == gi)(functools.partial(pooled, win))


def _pool(zc, w_pool, pool_scale, chunk):
    B, S, _ = zc.shape
    blk = pl.BlockSpec((1, S, LANES), lambda b, g: (b, 0, g))
    return pl.pallas_call(
        functools.partial(_pool_body, chunk=chunk),
        grid=(B, C_GROUPS),
        in_specs=[blk, pl.BlockSpec((1, C_GROUP_W, C_GROUP_W), lambda b, g: (g, 0, 0)),
                  pl.BlockSpec((1, LANES), lambda b, g: (0, g))],
        out_specs=blk,
        out_shape=jax.ShapeDtypeStruct((B, S, BRANCH_W), BF16),
        scratch_shapes=[pltpu.VMEM((S + 2 * HALO, LANES), F32)],
        compiler_params=_cparams("parallel", "arbitrary"),
        name="pool_mixer",
    )(zc, w_pool, pool_scale)


def _gated_merge_body(h_ref, oa_ref, ob_ref, oc_ref, od_ref, wg_ref, bg_ref, wl_ref, m_ref):
    h = h_ref[...]
    merged = None
    for b, o_ref in enumerate((oa_ref, ob_ref, oc_ref, od_ref)):
        gate = jax.nn.sigmoid(jnp.dot(h, wg_ref[b], preferred_element_type=F32) + bg_ref[b:b + 1, :])
        term = gate * jnp.dot(o_ref[...], wl_ref[b], preferred_element_type=F32)
        merged = term if merged is None else merged + term
    m_ref[...] = merged.astype(BF16)


def _gated_merge(h, outs, wg, bg, wl, tm, tn):
    T, D = h.shape
    row = lambda w_: pl.BlockSpec((tm, w_), lambda i, j: (i, 0))
    return pl.pallas_call(
        _gated_merge_body,
        grid=(T // tm, D // tn),
        in_specs=[row(D)] + [row(BRANCH_W)] * N_BRANCH + [
            pl.BlockSpec((N_BRANCH, D, tn), lambda i, j: (0, 0, j)),
            pl.BlockSpec((N_BRANCH, tn), lambda i, j: (0, j)),
            pl.BlockSpec((N_BRANCH, BRANCH_W, tn), lambda i, j: (0, 0, j))],
        out_specs=pl.BlockSpec((tm, tn), lambda i, j: (i, j)),
        out_shape=jax.ShapeDtypeStruct((T, D), BF16),
        compiler_params=_cparams("parallel", "arbitrary"),
        name="gated_merge",
    )(h, *outs, wg, bg, wl)


ROUTE_ROWS = 8


def _out_proj_body(m_ref, x_ref, wo_ref, g2_ref, wr_ref, rb_ref, xn_ref, h2_ref, rt_ref):
    xn = x_ref[...] + jnp.dot(m_ref[...], wo_ref[...], preferred_element_type=F32)
    xn_ref[...] = xn
    h2 = _rms(xn, g2_ref[...])
    h2_ref[...] = h2.astype(BF16)
    h_hi = h2.astype(BF16)
    h_lo = (h2 - h_hi.astype(F32)).astype(BF16)
    hi = jnp.dot(h_hi, wr_ref[...], preferred_element_type=F32)
    logits = hi[:, :ROUTER_W] + hi[:, ROUTER_W:] + jnp.dot(h_lo, wr_ref[:, :ROUTER_W], preferred_element_type=F32)
    rt_ref[...] = _route_tile(logits + rb_ref[...])


def _out_proj(merged, x, wo, g2, wr, route_bias, tm):
    T, D = x.shape
    row = pl.BlockSpec((tm, D), lambda i: (i, 0))
    return pl.pallas_call(
        _out_proj_body,
        grid=(T // tm,),
        in_specs=[row, row, _resident((D, D)), _resident((1, D)), _resident((D, 2 * ROUTER_W)),
                  _resident((1, ROUTER_W))],
        out_specs=[row, row, pl.BlockSpec((ROUTE_ROWS, tm), lambda i: (0, i))],
        out_shape=[jax.ShapeDtypeStruct((T, D), F32), jax.ShapeDtypeStruct((T, D), BF16),
                   jax.ShapeDtypeStruct((ROUTE_ROWS, T), F32)],
        compiler_params=_cparams("parallel"),
        name="out_proj_route",
    )(merged, x, wo, g2, wr, route_bias)


def _route_tile(lg):
    lane = lax.broadcasted_iota(jnp.int32, lg.shape, 1).astype(F32)
    neg = -jnp.inf
    row_max = lambda v: jnp.max(v, axis=-1, keepdims=True)
    first_at = lambda v, m: jnp.min(jnp.where(v == m, lane, float(LANES)), axis=-1, keepdims=True)
    g_lg = jnp.where(lane < N_GROUPS, lg, neg)
    g_max = row_max(g_lg)
    g_top = first_at(g_lg, g_max)
    pg_top = 1.0 / jnp.sum(jnp.exp(g_lg - g_max), axis=-1, keepdims=True)
    e_lo = N_GROUPS + EXP_PER_GROUP * g_top
    e_lg = jnp.where(jnp.logical_and(lane >= e_lo, lane < e_lo + EXP_PER_GROUP), lg, neg)
    picks, maxes = [], []
    for _ in range(TOP_K):
        m = row_max(e_lg)
        i = first_at(e_lg, m)
        picks.append(i)
        maxes.append(m)
        e_lg = jnp.where(lane == i, neg, e_lg)
    w = [jnp.exp(m - maxes[0]) for m in maxes]
    w_sum = functools.reduce(lambda a, b: a + b, w)
    cols = [i - N_GROUPS for i in picks] + [pg_top * wk / w_sum for wk in w]
    tile = jnp.zeros(lg.shape, F32)
    for c, v in enumerate(cols):
        tile = jnp.where(lane == c, v, tile)
    return tile.T[:ROUTE_ROWS]


def _dispatch(routed, tb):
    T = routed.shape[1]
    A = T * TOP_K
    flat_e = routed[:TOP_K].astype(jnp.int32).T.reshape(A)
    flat_w = routed[TOP_K:2 * TOP_K].T.reshape(A)
    iota = jnp.arange(A, dtype=jnp.int32)
    se, order, sw = lax.sort((flat_e, iota, flat_w), num_keys=1, is_stable=True)
    experts = jnp.arange(N_EXPERTS, dtype=jnp.int32)
    counts = jnp.sum((flat_e[:, None] == experts[None, :]).astype(jnp.int32), axis=0)
    start = jnp.cumsum(counts) - counts
    padded = (counts + tb - 1) // tb * tb
    pend = jnp.cumsum(padded)
    pstart = pend - padded
    n_blocks = -(-A // tb) + N_EXPERTS
    P = n_blocks * tb
    blk_first = jnp.arange(n_blocks, dtype=jnp.int32) * tb
    blk_exp = jnp.minimum(jnp.sum((pend[None, :] <= blk_first[:, None]).astype(jnp.int32), axis=1), N_EXPERTS - 1)
    n_valid = (pend[-1] // tb).astype(jnp.int32).reshape(1)
    slot = jnp.arange(P, dtype=jnp.int32)
    before = (pend[None, :] <= slot[:, None]).astype(jnp.int32)
    shift = jnp.sum(before * (padded - counts)[None, :], axis=1)
    in_pad = jnp.any(jnp.logical_and(slot[:, None] >= (pstart + counts)[None, :], slot[:, None] < pend[None, :]),
                     axis=1)
    valid = jnp.logical_and(jnp.logical_not(in_pad), slot < pend[-1])
    src = jnp.where(valid, slot - shift, slot % A)
    slot_tok = order[src] // TOP_K
    slot_w = jnp.where(valid, sw[src], 0.0)
    dest = iota + jnp.sum((se[:, None] == experts[None, :]).astype(jnp.int32) * (pstart - start)[None, :], axis=1)
    _, slot_of = lax.sort((order, dest), num_keys=1)
    return slot_tok, slot_w, blk_exp, n_valid, slot_of.reshape(T, TOP_K).T


def _expert_body(be_ref, nv_ref, x_ref, sw_ref, wg_ref, wu_ref, wd_ref, o_ref, wg_s, wu_s, wd_s):
    i = pl.program_id(0)

    @pl.when(jnp.logical_or(i == 0, be_ref[i] != be_ref[jnp.maximum(i - 1, 0)]))
    def _():
        wg_s[...] = wg_ref[0, 0].astype(BF16)
        wu_s[...] = wu_ref[0, 0].astype(BF16)
        wd_s[...] = wd_ref[0, 0].astype(BF16)

    @pl.when(i < nv_ref[0])
    def _():
        x = x_ref[...]
        g = jnp.dot(x, wg_s[...], preferred_element_type=F32)
        u = jnp.dot(x, wu_s[...], preferred_element_type=F32)
        a = (g * jax.nn.sigmoid(g) * u).astype(BF16)
        y = jnp.dot(a, wd_s[...], preferred_element_type=F32)
        sw = sw_ref[...]
        o_ref[...] = jnp.concatenate([y[:, c * LANES:(c + 1) * LANES] * sw for c in range(y.shape[1] // LANES)],
                                     axis=1).astype(o_ref.dtype)

    @pl.when(i >= nv_ref[0])
    def _():
        o_ref[...] = jnp.zeros(o_ref.shape, o_ref.dtype)


def _expert_ffn(xs, slot_w, blk_exp, n_valid, wg, wu, wd, layer, tb):
    P, D = xs.shape
    grid_spec = pltpu.PrefetchScalarGridSpec(
        num_scalar_prefetch=2,
        grid=(P // tb,),
        in_specs=[pl.BlockSpec((tb, D), lambda i, be, nv: (i, 0)),
                  pl.BlockSpec((tb, LANES), lambda i, be, nv: (i, 0)),
                  pl.BlockSpec((1, 1, D, D_EXPERT), lambda i, be, nv: (layer, be[i], 0, 0)),
                  pl.BlockSpec((1, 1, D, D_EXPERT), lambda i, be, nv: (layer, be[i], 0, 0)),
                  pl.BlockSpec((1, 1, D_EXPERT, D), lambda i, be, nv: (layer, be[i], 0, 0))],
        out_specs=pl.BlockSpec((tb, D), lambda i, be, nv: (i, 0)),
        scratch_shapes=[pltpu.VMEM((D, D_EXPERT), BF16), pltpu.VMEM((D, D_EXPERT), BF16),
                        pltpu.VMEM((D_EXPERT, D), BF16)],
    )
    return pl.pallas_call(
        _expert_body,
        grid_spec=grid_spec,
        out_shape=jax.ShapeDtypeStruct((P, D), BF16),
        compiler_params=_cparams("arbitrary"),
        name="expert_ffn",
    )(blk_exp, n_valid, xs, jnp.broadcast_to(slot_w[:, None], (P, LANES)), wg, wu, wd)


def _final_norm_body(*refs):
    x_ref, y_refs, (g_ref, o_ref) = refs[0], refs[1:-2], refs[-2:]
    o_ref[...] = _rms(_residual_sum(x_ref, y_refs), g_ref[...])


def _final_norm(x, adds, g, row0, rows, tm):
    D = x.shape[1]
    first = row0 // tm
    src = pl.BlockSpec((tm, D), lambda i: (first + i, 0))
    return pl.pallas_call(
        _final_norm_body, grid=(rows // tm,),
        in_specs=[src] * (1 + len(adds)) + [_resident((1, D))],
        out_specs=pl.BlockSpec((tm, D), lambda i: (i, 0)),
        out_shape=jax.ShapeDtypeStruct((rows, D), F32), compiler_params=_cparams("parallel"), name="final_norm",
    )(x, *adds, g)


def _rope_tables(seq):
    def angles(pos, dim):
        inv = 1.0 / (ROPE_THETA ** (jnp.arange(0, dim, 2, dtype=F32) / dim))
        ang = pos.astype(F32)[:, None] * inv[None, :]
        return jnp.cos(ang), jnp.sin(ang)
    pos = jnp.arange(seq)
    c1, s1 = angles(pos, A_ROPE)
    cr, sr = angles(pos // GRID_W, D_HEAD // 2)
    cc, sc = angles(pos % GRID_W, D_HEAD // 2)
    pad1, pad0 = jnp.ones((seq, ROPE_PAD), F32), jnp.zeros((seq, ROPE_PAD), F32)
    return (jnp.concatenate([c1, c1, pad1], 1), jnp.concatenate([-s1, s1, pad0], 1),
            jnp.concatenate([cr, cr, cc, cc], 1), jnp.concatenate([-sr, sr, -sc, sc], 1))


def _layer_weights(l, w_in, w_uq, w_ukv, w_route_group, w_route_expert):
    D = D_MODEL
    k_r_end = A_Q_LORA + A_KV_LORA + A_ROPE
    win = jnp.concatenate([w_in[l][:, :k_r_end], jnp.zeros((D, ROPE_PAD), F32), w_in[l][:, k_r_end:]], 1)
    uq = w_uq[l].reshape(A_Q_LORA, A_HEADS, A_NOPE + A_ROPE)
    uq_rope = jnp.pad(uq[:, :, A_NOPE:], ((0, 0), (0, 0), (0, ROPE_PAD)))
    wuq = jnp.concatenate([uq[:, :, :A_NOPE].reshape(A_Q_LORA, -1), uq_rope.reshape(A_Q_LORA, -1)], 1)
    ukv = w_ukv[l].reshape(A_KV_LORA, A_HEADS, A_NOPE + A_V)
    wukv = jnp.concatenate([ukv[:, :, :A_NOPE].reshape(A_KV_LORA, -1), ukv[:, :, A_NOPE:].reshape(A_KV_LORA, -1)], 1)
    wr = jnp.concatenate([w_route_group[l], w_route_expert[l],
                          jnp.zeros((D, ROUTER_W - N_GROUPS - N_EXPERTS), F32)], 1)
    wr_hi = wr.astype(BF16)
    wr_lo = (wr - wr_hi.astype(F32)).astype(BF16)
    return win.astype(BF16), wuq.astype(BF16), wukv.astype(BF16), jnp.concatenate([wr_hi, wr_lo], 1)


def kernel(x_prompt, x_sample, norm1_g, w_in, q_norm_g, kv_norm_g, w_uq, w_ukv, lam_q1, lam_k1, lam_q2, lam_k2,
           rel_bias, w_pool, pool_scale, qk_norm_q, qk_norm_k, w_lift, w_gate, b_gate, w_out, norm2_g,
           w_route_group, b_route_group, w_route_expert, b_route_expert, w_exp_gate, w_exp_up, w_exp_down,
           final_g):
    assert x_prompt.shape[1:] == x_sample.shape[1:], "both request groups must share (seq, d_model)"
    n_prompt, S, D = x_prompt.shape
    B = n_prompt + x_sample.shape[0]
    T = B * S
    t = _tiles(S, T)
    x = None
    depth = w_in.shape[0]
    tabs = _rope_tables(S)
    diag, corner, far = _diff_bias_tables(rel_bias, t["tq"])
    row2 = lambda v: v.reshape(1, -1).astype(F32)

    moe_out = ()
    for l in range(depth):
        lam_init = 0.8 - 0.6 * math.exp(-0.3 * l)
        lam = (jnp.exp(jnp.sum(lam_q1[l] * lam_k1[l])) - jnp.exp(jnp.sum(lam_q2[l] * lam_k2[l])) + lam_init)
        win, wuq, wukv, wr = _layer_weights(l, w_in, w_uq, w_ukv, w_route_group, w_route_expert)

        if l == 0:
            x, h, za, zb, zc, zd = _norm_inproj_first(x_prompt.reshape(-1, D), x_sample.reshape(-1, D),
                                                      row2(norm1_g[l]), win, t["tm"])
        else:
            x, h, za, zb, zc, zd = _norm_inproj(x, moe_out, row2(norm1_g[l]), win, t["tm"])
        qa, ka, va, qd, kd = _prep(za, zd, row2(q_norm_g[l]), row2(kv_norm_g[l]), wuq, wukv,
                                   row2(qk_norm_q[l]), row2(qk_norm_k[l]), tabs, S, t["tm"])
        seq3 = lambda a: a.reshape(B, S, a.shape[-1])
        rep = D_HEADS // D_KV_HEADS
        o_a = _flash(seq3(qa), seq3(ka), seq3(va), kv_heads=A_HEADS, stack=1, q_w=A_QK_W, k_col0=0, v_col0=0,
                     tq=t["rows"], ts=t["tq"], tk=t["tk"])
        o_b = _diff_attn(seq3(zb), lam.reshape(1).astype(F32), diag, corner, far, tq=t["tq"], tk=t["tk"],
                         out_scale=1.0 - lam_init)
        o_c = _pool(seq3(zc), w_pool[l].astype(BF16), row2(pool_scale[l]), t["pool_chunk"])
        o_d = _flash(seq3(qd), seq3(kd), seq3(zd), kv_heads=D_KV_HEADS, stack=rep, q_w=D_HEAD, k_col0=0,
                     v_col0=D_HEADS + D_KV_HEADS, tq=t["rows"] // rep, ts=t["tq"], tk=t["tk"])
        outs = [o.reshape(T, BRANCH_W) for o in (o_a, o_b, o_c, o_d)]
        merged = _gated_merge(h, outs, w_gate[l].astype(BF16), b_gate[l].astype(F32), w_lift[l].astype(BF16),
                              t["tm_merge"], t["tn"])
        route_bias = jnp.concatenate([b_route_group[l], b_route_expert[l],
                                      jnp.zeros((ROUTER_W - N_GROUPS - N_EXPERTS,), F32)]).reshape(1, ROUTER_W)
        x, h2, routed = _out_proj(merged, x, w_out[l].astype(BF16), row2(norm2_g[l]), wr, route_bias.astype(F32),
                                  t["tm"])
        slot_tok, slot_w, blk_exp, n_valid, slot_of = _dispatch(routed, t["tb"])
        ys = _expert_ffn(h2[slot_tok], slot_w, blk_exp, n_valid, w_exp_gate, w_exp_up, w_exp_down, l, t["tb"])
        moe_out = tuple(ys[slot_of[k]] for k in range(TOP_K))

    t_prompt = n_prompt * S
    y_prompt = _final_norm(x, moe_out, row2(final_g), 0, t_prompt, t["tm"]).reshape(n_prompt, S, D)
    y_sample = _final_norm(x, moe_out, row2(final_g), t_prompt, T - t_prompt, t["tm"]).reshape(B - n_prompt, S, D)
    return y_prompt, y_sample
```

```python
import functools
import math

import jax
import jax.numpy as jnp
from jax import lax
from jax.experimental import pallas as pl
from jax.experimental.pallas import tpu as pltpu

F32 = jnp.float32
BF16 = jnp.bfloat16
LOG2E = 1.4426950408889634

D_MODEL = 2048
GRID_W = 64
BRANCH_W = 512
N_BRANCH = 4
ROPE_THETA = 10000.0
EPS = 1e-6
A_HEADS, A_Q_LORA, A_KV_LORA, A_NOPE, A_ROPE, A_V = 4, 384, 128, 128, 64, 128
B_HEADS, B_QK, B_V = 4, 64, 128
REL_BUCKETS, REL_MAX_DIST = 32, 128
POOL_WINDOWS = (2, 4, 8, 16)
C_GROUPS, C_GROUP_W = 4, 128
D_HEADS, D_KV_HEADS, D_HEAD = 4, 2, 128
N_GROUPS, EXP_PER_GROUP, N_EXPERTS, TOP_K, D_EXPERT = 4, 8, 32, 2, 512

LANES = 128
ROPE_PAD = LANES - A_ROPE
ZA_W = A_Q_LORA + A_KV_LORA + LANES
ZB_W = 3 * B_HEADS * B_V
ZC_W = BRANCH_W
ZD_W = (D_HEADS + 2 * D_KV_HEADS) * D_HEAD
A_QK_W = 2 * LANES
ROUTER_W = LANES
HALO = 16
VMEM_LIMIT = 56 * 2 ** 20


def _tiles(seq, tokens):
    def pick(n, pref):
        t = min(pref, n)
        while n % t:
            t //= 2
        return t
    return dict(
        tm=pick(seq, 512),
        tm_in=pick(seq, 256),
        tq=pick(seq, 512),
        rows=pick(seq, 1024),
        tk=pick(seq, 4096),
        tm_merge=pick(tokens, 1024),
        tn=512,
        tb=256,
        pool_chunk=pick(seq, 1024),
    )


def _cparams(*sem):
    return pltpu.CompilerParams(dimension_semantics=sem, vmem_limit_bytes=VMEM_LIMIT)


def _resident(shape):
    zeros = (0,) * len(shape)
    return pl.BlockSpec(shape, lambda *_: zeros, pipeline_mode=pl.Buffered(1))


def _rms(x, g=None):
    y = x * lax.rsqrt(jnp.mean(x * x, axis=-1, keepdims=True) + EPS)
    return y if g is None else y * g


def _rope_lanes(x, c, s):
    lane = lax.broadcasted_iota(jnp.int32, x.shape, 1)
    first_half = (lane & 32) == 0
    partner = jnp.where(first_half, pltpu.roll(x, 96, 1), pltpu.roll(x, 32, 1))
    return x * c + partner * s


def _residual_sum(x_ref, y_refs):
    x = x_ref[...]
    for y_ref in y_refs:
        x = x + y_ref[...].astype(F32)
    return x


N_PREP_PARAMS = 10
PREP_OUT_W = (A_HEADS * A_QK_W, A_HEADS * A_QK_W, A_HEADS * A_V, D_HEADS * D_HEAD, D_KV_HEADS * D_HEAD,
              D_KV_HEADS * D_HEAD)


def _norm_inproj_body(*refs, n_x, n_add, first_rows):
    x_refs, y_refs = refs[:n_x], refs[n_x:n_x + n_add]
    g_ref, w_ref = refs[n_x + n_add:n_x + n_add + 2]
    prep_refs = refs[n_x + n_add + 2:n_x + n_add + 2 + N_PREP_PARAMS]
    outs = refs[n_x + n_add + 2 + N_PREP_PARAMS:]
    if n_x == 2:
        x = jnp.where(pl.program_id(0) < first_rows, x_refs[0][...], x_refs[1][...])
    else:
        x = _residual_sum(x_refs[0], y_refs)
    if n_x == 2 or n_add:
        outs[0][...] = x
        outs = outs[1:]
    h_ref, zb_ref, zc_ref = outs[:3]
    h = _rms(x, g_ref[...]).astype(BF16)
    h_ref[...] = h
    proj = lambda lo, w_: jnp.dot(h, w_ref[:, lo:lo + w_], preferred_element_type=F32)
    zb_ref[...] = proj(ZA_W, ZB_W).astype(BF16)
    zc_ref[...] = proj(ZA_W + ZB_W, ZC_W).astype(BF16)
    _mixer_prep(proj(0, ZA_W), proj(ZA_W + ZB_W + ZC_W, ZD_W), prep_refs, outs[3:])


def _norm_inproj(xs, adds, g, w, prep, seq, tm):
    D = xs[0].shape[1]
    T = sum(x.shape[0] for x in xs)
    per_seq = seq // tm
    row = lambda w_: pl.BlockSpec((tm, w_), lambda i: (i, 0))
    if len(xs) == 2:
        na = xs[0].shape[0] // tm
        x_specs = [pl.BlockSpec((tm, D), lambda i: (jnp.minimum(i, na - 1), 0)),
                   pl.BlockSpec((tm, D), lambda i: (jnp.maximum(i - na, 0), 0))]
    else:
        na, x_specs = None, [row(D)]
    new_x = len(xs) == 2 or bool(adds)
    sum_spec, sum_shape = ([row(D)], [jax.ShapeDtypeStruct((T, D), F32)]) if new_x else ([], [])
    tab = pl.BlockSpec((tm, LANES), lambda i: (i % per_seq, 0))
    out_w = (D, ZB_W, ZC_W) + PREP_OUT_W
    outs = pl.pallas_call(
        functools.partial(_norm_inproj_body, n_x=len(xs), n_add=len(adds), first_rows=na),
        grid=(T // tm,),
        in_specs=x_specs + [row(D)] * len(adds) + [_resident((1, D)), _resident(w.shape)]
        + [_resident(p.shape) for p in prep[:N_PREP_PARAMS - 4]] + [tab] * 4,
        out_specs=sum_spec + [row(w_) for w_ in out_w],
        out_shape=sum_shape + [jax.ShapeDtypeStruct((T, w_), BF16) for w_ in out_w],
        compiler_params=_cparams("parallel"),
        name="norm_inproj",
    )(*xs, *adds, g, w, *prep)
    return list(outs) if new_x else [xs[0]] + list(outs)


def _mixer_prep(za, zd, prep_refs, out_refs):
    gq_ref, gkv_ref, wuq_ref, wukv_ref, gdq_ref, gdk_ref, ca_ref, sa_ref, cd_ref, sd_ref = prep_refs
    qa_ref, ka_ref, va_ref, qd_ref, kd_ref, vd_ref = out_refs
    a_scale = (A_NOPE + A_ROPE) ** -0.5 * LOG2E
    d_scale = D_HEAD ** -0.5 * LOG2E
    ca, sa, cd, sd = ca_ref[...], sa_ref[...], cd_ref[...], sd_ref[...]
    cq = _rms(za[:, :A_Q_LORA], gq_ref[...]).astype(BF16)
    q = jnp.dot(cq, wuq_ref[...], preferred_element_type=F32)
    ckv = _rms(za[:, A_Q_LORA:A_Q_LORA + A_KV_LORA], gkv_ref[...]).astype(BF16)
    kv = jnp.dot(ckv, wukv_ref[...], preferred_element_type=F32)
    k_rope = _rope_lanes(za[:, A_Q_LORA + A_KV_LORA:], ca, sa).astype(BF16)
    nope_w = A_HEADS * A_NOPE
    for h in range(A_HEADS):
        lo = h * A_QK_W
        qa_ref[:, lo:lo + LANES] = (q[:, h * LANES:(h + 1) * LANES] * a_scale).astype(BF16)
        q_rope = _rope_lanes(q[:, nope_w + h * LANES:nope_w + (h + 1) * LANES], ca, sa)
        qa_ref[:, lo + LANES:lo + 2 * LANES] = (q_rope * a_scale).astype(BF16)
        ka_ref[:, lo:lo + LANES] = kv[:, h * LANES:(h + 1) * LANES].astype(BF16)
        ka_ref[:, lo + LANES:lo + 2 * LANES] = k_rope
    va_ref[...] = kv[:, nope_w:].astype(BF16)
    for h in range(D_HEADS):
        xh = _rms(zd[:, h * D_HEAD:(h + 1) * D_HEAD], gdq_ref[...])
        qd_ref[:, h * D_HEAD:(h + 1) * D_HEAD] = (_rope_lanes(xh, cd, sd) * d_scale).astype(BF16)
    k_off = D_HEADS * D_HEAD
    for h in range(D_KV_HEADS):
        xh = _rms(zd[:, k_off + h * D_HEAD:k_off + (h + 1) * D_HEAD], gdk_ref[...])
        kd_ref[:, h * D_HEAD:(h + 1) * D_HEAD] = _rope_lanes(xh, cd, sd).astype(BF16)
    vd_ref[...] = zd[:, k_off + D_KV_HEADS * D_HEAD:].astype(BF16)


def _qk(q, k):
    return lax.dot_general(q, k, (((1,), (1,)), ((), ())), preferred_element_type=F32)


def _lane_fold(fn, acc, x):
    for c in range(x.shape[1] // LANES):
        acc = fn(acc, x[:, c * LANES:(c + 1) * LANES])
    return acc


def _score_pass(q, k_ref, s_ref, tk, max_shift=None):
    tq, ts = s_ref.shape[1:]
    sub = tk // ts

    def step(j, mrun):
        ks = pl.multiple_of(j * tk, tk)
        s = _qk(q, k_ref[0, pl.ds(ks, tk), :])
        for c in range(sub):
            jj = j * sub + c
            sc = s[:, c * ts:(c + 1) * ts]
            s_ref[jj] = sc
            m_sub = _lane_fold(jnp.maximum, sc[:, :LANES], sc[:, LANES:])
            mrun = jnp.maximum(mrun, m_sub if max_shift is None else m_sub + max_shift(jj))
        return mrun

    return lax.fori_loop(0, k_ref.shape[1] // tk, step, jnp.full((tq, LANES), -jnp.inf, F32))


def _value_pass(v_ref, s_ref, tk, m_of):
    tq, ts = s_ref.shape[1:]
    sub = tk // ts
    dv = v_ref.shape[2]
    ones = jnp.ones((tk, LANES), BF16)

    def step(j, acc):
        ks = pl.multiple_of(j * tk, tk)
        p = []
        for c in range(sub):
            jj = j * sub + c
            m = m_of(jj)
            s = s_ref[jj]
            p += [jnp.exp2((s[:, i * LANES:(i + 1) * LANES] - m).astype(BF16)) for i in range(ts // LANES)]
        v_ext = jnp.concatenate([v_ref[0, pl.ds(ks, tk), :], ones], axis=1)
        return acc + jnp.dot(jnp.concatenate(p, axis=1), v_ext, preferred_element_type=F32)

    acc = lax.fori_loop(0, v_ref.shape[1] // tk, step, jnp.zeros((tq, dv + LANES), F32))
    return acc[:, :dv] / acc[:, dv:]


def _row_max(mrun):
    return jnp.broadcast_to(jnp.max(mrun, axis=-1, keepdims=True), mrun.shape)


def _flash_body(q_ref, k_ref, v_ref, o_ref, s_ref, *, tk, stack):
    tq = q_ref.shape[1]
    q_w = q_ref.shape[2] // stack
    q = jnp.concatenate([q_ref[0, :, c * q_w:(c + 1) * q_w] for c in range(stack)], axis=0)
    m = _row_max(_score_pass(q, k_ref, s_ref, tk))
    o = _value_pass(v_ref, s_ref, tk, lambda jj: m).astype(o_ref.dtype)
    o_ref[0] = jnp.concatenate([o[c * tq:(c + 1) * tq] for c in range(stack)], axis=1)


def _flash(q, k, v, *, kv_heads, stack, q_w, k_col0, v_col0, tq, ts, tk):
    B, S, _ = q.shape
    dv = LANES
    return pl.pallas_call(
        functools.partial(_flash_body, tk=tk, stack=stack),
        grid=(B, kv_heads, S // tq),
        in_specs=[pl.BlockSpec((1, tq, stack * q_w), lambda b, h, i: (b, i, h)),
                  pl.BlockSpec((1, S, q_w), lambda b, h, i: (b, 0, k_col0 + h)),
                  pl.BlockSpec((1, S, dv), lambda b, h, i: (b, 0, v_col0 + h))],
        out_specs=pl.BlockSpec((1, tq, stack * dv), lambda b, h, i: (b, i, h)),
        out_shape=jax.ShapeDtypeStruct((B, S, kv_heads * stack * dv), BF16),
        scratch_shapes=[pltpu.VMEM((S // ts, stack * tq, ts), F32)],
        compiler_params=_cparams("parallel", "parallel", "arbitrary"),
        name="flash_attn",
    )(q, k, v)


def _diff_body(lam_ref, cb_ref, q_ref, k_ref, v_ref, diag_ref, corner_ref, o_ref, s_ref, *, tk, out_scale):
    h, qi = pl.program_id(1), pl.program_id(2)
    n_sub = s_ref.shape[0]
    tq = q_ref.shape[1]
    nd = REL_MAX_DIST
    qf = q_ref[0].astype(F32) * (B_QK ** -0.5 * LOG2E)
    lane = lax.broadcasted_iota(jnp.int32, qf.shape, 1)
    c_lo, c_hi = cb_ref[0, h], cb_ref[1, h]
    side_shift = lambda jj: jnp.where(jj < qi, c_lo, jnp.where(jj > qi, c_hi, 0.0))
    tile_max = lambda s: _lane_fold(jnp.maximum, s[:, :LANES], s[:, LANES:])

    q = jnp.concatenate([jnp.where(lane < B_QK, qf, 0.0), jnp.where(lane >= B_QK, qf, 0.0)], axis=0).astype(BF16)
    far = lambda jj: jnp.where(jj < qi - 1, c_lo, jnp.where(jj > qi + 1, c_hi, -jnp.inf))
    mrun = _score_pass(q, k_ref, s_ref, tk, max_shift=far)

    diag = diag_ref[0]
    s = s_ref[qi] + jnp.concatenate([diag, diag], axis=0)
    s_ref[qi] = s
    mrun = jnp.maximum(mrun, tile_max(s))

    def neighbour(jj, rows0, cols0, corner, c):
        def fix(mrun):
            for r in (rows0, tq + rows0):
                s_ref[jj, r:r + nd, cols0:cols0 + nd] = s_ref[jj, r:r + nd, cols0:cols0 + nd] + corner
            return jnp.maximum(mrun, tile_max(s_ref[jj]) + c)
        return fix

    mrun = lax.cond(qi >= 1, neighbour(jnp.maximum(qi - 1, 0), 0, tq - nd, corner_ref[0, 0], c_lo),
                    lambda m_: m_, mrun)
    mrun = lax.cond(qi + 1 < n_sub, neighbour(jnp.minimum(qi + 1, n_sub - 1), tq - nd, 0, corner_ref[0, 1], c_hi),
                    lambda m_: m_, mrun)
    m = _row_max(mrun)
    o = _value_pass(v_ref, s_ref, tk, lambda jj: m - side_shift(jj))
    o = o[:tq] - lam_ref[0] * o[tq:]
    o_ref[0] = (_rms(o) * out_scale).astype(o_ref.dtype)


def _rel_bucket(rel):
    half = REL_BUCKETS // 2
    max_exact = half // 2
    ret = (rel > 0).astype(jnp.int32) * half
    n = jnp.abs(rel)
    large = max_exact + (jnp.log(jnp.maximum(n, 1).astype(F32) / max_exact)
                         / math.log(REL_MAX_DIST / max_exact) * (half - max_exact)).astype(jnp.int32)
    large = jnp.minimum(large, half - 1)
    return ret + jnp.where(n < max_exact, n, large)


def _diff_bias_tables(rel_bias, tq):
    nd = REL_MAX_DIST
    rb = rel_bias.astype(F32) * LOG2E
    far = jnp.stack([rb[REL_BUCKETS // 2 - 1], rb[REL_BUCKETS - 1]])

    def table(rel):
        bucket = _rel_bucket(rel)
        out = jnp.zeros((B_HEADS,) + rel.shape, F32)
        for b in range(REL_BUCKETS):
            out = jnp.where(bucket[None] == b, rb[b].reshape((B_HEADS,) + (1,) * rel.ndim), out)
        return out

    a, e = jnp.arange(tq), jnp.arange(nd)
    diag = table(a[None, :] - a[:, None])
    lo = table((tq - nd + e[None, :]) - tq - e[:, None]) - far[0][:, None, None]
    hi = table(e[None, :] + tq - (tq - nd + e[:, None])) - far[1][:, None, None]
    return diag, jnp.stack([lo, hi], axis=1), far


def _diff_attn(zb, lam, diag, corner, far, *, tq, tk, out_scale):
    B, S, _ = zb.shape
    nd = REL_MAX_DIST
    assert tq >= nd, "far key chunks must lie beyond the last distinct relative bucket"
    H = B_HEADS
    smem = pl.BlockSpec(memory_space=pltpu.SMEM)
    return pl.pallas_call(
        functools.partial(_diff_body, tk=tk, out_scale=out_scale),
        grid=(B, H, S // tq),
        in_specs=[smem, smem,
                  pl.BlockSpec((1, tq, B_V), lambda b, h, i: (b, i, h)),
                  pl.BlockSpec((1, S, B_V), lambda b, h, i: (b, 0, H + h)),
                  pl.BlockSpec((1, S, B_V), lambda b, h, i: (b, 0, 2 * H + h)),
                  pl.BlockSpec((1, tq, tq), lambda b, h, i: (h, 0, 0)),
                  pl.BlockSpec((1, 2, nd, nd), lambda b, h, i: (h, 0, 0, 0))],
        out_specs=pl.BlockSpec((1, tq, B_V), lambda b, h, i: (b, i, h)),
        out_shape=jax.ShapeDtypeStruct((B, S, H * B_V), BF16),
        scratch_shapes=[pltpu.VMEM((S // tq, 2 * tq, tq), F32)],
        compiler_params=_cparams("parallel", "parallel", "arbitrary"),
        name="diff_attn",
    )(lam, far, zb, zb, zb, diag, corner)


def _pool_body(u_ref, w_ref, sc_ref, o_ref, pad_ref, *, chunk):
    g = pl.program_id(1)
    S = u_ref.shape[1]
    pad_ref[0:HALO, :] = jnp.zeros((HALO, LANES), F32)
    pad_ref[HALO + S:, :] = jnp.zeros((HALO, LANES), F32)
    pad_ref[HALO:HALO + S, :] = u_ref[0].astype(F32)
    w_mat, sc = w_ref[0], sc_ref[...]

    def pooled(win):
        def body(c, _):
            r0 = pl.multiple_of(c * chunk, chunk)
            tot = pad_ref[pl.ds(r0 + HALO - win // 2, chunk), :]
            for j in range(1 - win // 2, win // 2):
                tot = tot + pad_ref[pl.ds(r0 + HALO + j, chunk), :]
            t = r0 + lax.broadcasted_iota(jnp.int32, (chunk, 1), 0)
            cnt = jnp.clip(t - win // 2 + win, 0, S) - jnp.clip(t - win // 2, 0, S)
            d = tot / cnt.astype(F32) - pad_ref[pl.ds(r0 + HALO, chunk), :]
            y = jnp.dot(d.astype(BF16), w_mat, preferred_element_type=F32) * sc
            o_ref[0, pl.ds(r0, chunk), :] = y.astype(o_ref.dtype)
            return 0
        lax.fori_loop(0, S // chunk, body, 0)

    for gi, win in enumerate(POOL_WINDOWS):
        pl.when(g == gi)(functools.partial(pooled, win))


def _pool(zc, w_pool, pool_scale, chunk):
    B, S, _ = zc.shape
    blk = pl.BlockSpec((1, S, LANES), lambda b, g: (b, 0, g))
    return pl.pallas_call(
        functools.partial(_pool_body, chunk=chunk),
        grid=(B, C_GROUPS),
        in_specs=[blk, pl.BlockSpec((1, C_GROUP_W, C_GROUP_W), lambda b, g: (g, 0, 0)),
                  pl.BlockSpec((1, LANES), lambda b, g: (0, g))],
        out_specs=blk,
        out_shape=jax.ShapeDtypeStruct((B, S, BRANCH_W), BF16),
        scratch_shapes=[pltpu.VMEM((S + 2 * HALO, LANES), F32)],
        compiler_params=_cparams("parallel", "arbitrary"),
        name="pool_mixer",
    )(zc, w_pool, pool_scale)


def _gated_merge_body(h_ref, oa_ref, ob_ref, oc_ref, od_ref, wg_ref, bg_ref, wl_ref, m_ref):
    h = h_ref[...]
    merged = None
    for b, o_ref in enumerate((oa_ref, ob_ref, oc_ref, od_ref)):
        gate = jax.nn.sigmoid(jnp.dot(h, wg_ref[b], preferred_element_type=F32) + bg_ref[b:b + 1, :])
        term = gate * jnp.dot(o_ref[...], wl_ref[b], preferred_element_type=F32)
        merged = term if merged is None else merged + term
    m_ref[...] = merged.astype(BF16)


def _gated_merge(h, outs, wg, bg, wl, tm, tn):
    T, D = h.shape
    row = lambda w_: pl.BlockSpec((tm, w_), lambda i, j: (i, 0))
    return pl.pallas_call(
        _gated_merge_body,
        grid=(T // tm, D // tn),
        in_specs=[row(D)] + [row(BRANCH_W)] * N_BRANCH + [
            pl.BlockSpec((N_BRANCH, D, tn), lambda i, j: (0, 0, j)),
            pl.BlockSpec((N_BRANCH, tn), lambda i, j: (0, j)),
            pl.BlockSpec((N_BRANCH, BRANCH_W, tn), lambda i, j: (0, 0, j))],
        out_specs=pl.BlockSpec((tm, tn), lambda i, j: (i, j)),
        out_shape=jax.ShapeDtypeStruct((T, D), BF16),
        compiler_params=_cparams("parallel", "arbitrary"),
        name="gated_merge",
    )(h, *outs, wg, bg, wl)


ROUTE_ROWS = 8


def _out_proj_body(m_ref, x_ref, wo_ref, g2_ref, wr_ref, rb_ref, xn_ref, h2_ref, rt_ref):
    xn = x_ref[...] + jnp.dot(m_ref[...], wo_ref[...], preferred_element_type=F32)
    xn_ref[...] = xn
    h2 = _rms(xn, g2_ref[...])
    h2_ref[...] = h2.astype(BF16)
    h_hi = h2.astype(BF16)
    h_lo = (h2 - h_hi.astype(F32)).astype(BF16)
    hi = jnp.dot(h_hi, wr_ref[...], preferred_element_type=F32)
    logits = hi[:, :ROUTER_W] + hi[:, ROUTER_W:] + jnp.dot(h_lo, wr_ref[:, :ROUTER_W], preferred_element_type=F32)
    rt_ref[...] = _route_tile(logits + rb_ref[...])


def _out_proj(merged, x, wo, g2, wr, route_bias, tm):
    T, D = x.shape
    row = pl.BlockSpec((tm, D), lambda i: (i, 0))
    return pl.pallas_call(
        _out_proj_body,
        grid=(T // tm,),
        in_specs=[row, row, _resident((D, D)), _resident((1, D)), _resident((D, 2 * ROUTER_W)),
                  _resident((1, ROUTER_W))],
        out_specs=[row, row, pl.BlockSpec((ROUTE_ROWS, tm), lambda i: (0, i))],
        out_shape=[jax.ShapeDtypeStruct((T, D), F32), jax.ShapeDtypeStruct((T, D), BF16),
                   jax.ShapeDtypeStruct((ROUTE_ROWS, T), F32)],
        compiler_params=_cparams("parallel"),
        name="out_proj_route",
    )(merged, x, wo, g2, wr, route_bias)


def _route_tile(lg):
    lane = lax.broadcasted_iota(jnp.int32, lg.shape, 1).astype(F32)
    neg = -jnp.inf
    row_max = lambda v: jnp.max(v, axis=-1, keepdims=True)
    first_at = lambda v, m: jnp.min(jnp.where(v == m, lane, float(LANES)), axis=-1, keepdims=True)
    g_lg = jnp.where(lane < N_GROUPS, lg, neg)
    g_max = row_max(g_lg)
    g_top = first_at(g_lg, g_max)
    pg_top = 1.0 / jnp.sum(jnp.exp(g_lg - g_max), axis=-1, keepdims=True)
    e_lo = N_GROUPS + EXP_PER_GROUP * g_top
    e_lg = jnp.where(jnp.logical_and(lane >= e_lo, lane < e_lo + EXP_PER_GROUP), lg, neg)
    picks, maxes = [], []
    for _ in range(TOP_K):
        m = row_max(e_lg)
        i = first_at(e_lg, m)
        picks.append(i)
        maxes.append(m)
        e_lg = jnp.where(lane == i, neg, e_lg)
    w = [jnp.exp(m - maxes[0]) for m in maxes]
    w_sum = functools.reduce(lambda a, b: a + b, w)
    cols = [i - N_GROUPS for i in picks] + [pg_top * wk / w_sum for wk in w]
    tile = jnp.zeros(lg.shape, F32)
    for c, v in enumerate(cols):
        tile = jnp.where(lane == c, v, tile)
    return tile.T[:ROUTE_ROWS]


def _dispatch(routed, tb):
    T = routed.shape[1]
    A = T * TOP_K
    flat_e = routed[:TOP_K].astype(jnp.int32).T.reshape(A)
    flat_w = routed[TOP_K:2 * TOP_K].T.reshape(A)
    iota = jnp.arange(A, dtype=jnp.int32)
    se, order, sw = lax.sort((flat_e, iota, flat_w), num_keys=1, is_stable=True)
    experts = jnp.arange(N_EXPERTS, dtype=jnp.int32)
    counts = jnp.sum((flat_e[:, None] == experts[None, :]).astype(jnp.int32), axis=0)
    start = jnp.cumsum(counts) - counts
    padded = (counts + tb - 1) // tb * tb
    pend = jnp.cumsum(padded)
    pstart = pend - padded
    n_blocks = -(-A // tb) + N_EXPERTS
    P = n_blocks * tb
    blk_first = jnp.arange(n_blocks, dtype=jnp.int32) * tb
    blk_exp = jnp.minimum(jnp.sum((pend[None, :] <= blk_first[:, None]).astype(jnp.int32), axis=1), N_EXPERTS - 1)
    n_valid = (pend[-1] // tb).astype(jnp.int32).reshape(1)
    slot = jnp.arange(P, dtype=jnp.int32)
    before = (pend[None, :] <= slot[:, None]).astype(jnp.int32)
    shift = jnp.sum(before * (padded - counts)[None, :], axis=1)
    in_pad = jnp.any(jnp.logical_and(slot[:, None] >= (pstart + counts)[None, :], slot[:, None] < pend[None, :]),
                     axis=1)
    valid = jnp.logical_and(jnp.logical_not(in_pad), slot < pend[-1])
    src = jnp.where(valid, slot - shift, slot % A)
    slot_tok = order[src] // TOP_K
    slot_w = jnp.where(valid, sw[src], 0.0)
    dest = iota + jnp.sum((se[:, None] == experts[None, :]).astype(jnp.int32) * (pstart - start)[None, :], axis=1)
    _, slot_of = lax.sort((order, dest), num_keys=1)
    return slot_tok, slot_w, blk_exp, n_valid, slot_of.reshape(T, TOP_K).T


def _expert_body(be_ref, nv_ref, x_ref, sw_ref, wg_ref, wu_ref, wd_ref, o_ref, wg_s, wu_s, wd_s):
    i = pl.program_id(0)

    @pl.when(jnp.logical_or(i == 0, be_ref[i] != be_ref[jnp.maximum(i - 1, 0)]))
    def _():
        wg_s[...] = wg_ref[0, 0].astype(BF16)
        wu_s[...] = wu_ref[0, 0].astype(BF16)
        wd_s[...] = wd_ref[0, 0].astype(BF16)

    @pl.when(i < nv_ref[0])
    def _():
        x = x_ref[...]
        g = jnp.dot(x, wg_s[...], preferred_element_type=F32)
        u = jnp.dot(x, wu_s[...], preferred_element_type=F32)
        a = (g * jax.nn.sigmoid(g) * u).astype(BF16)
        y = jnp.dot(a, wd_s[...], preferred_element_type=F32)
        sw = sw_ref[...]
        o_ref[...] = jnp.concatenate([y[:, c * LANES:(c + 1) * LANES] * sw for c in range(y.shape[1] // LANES)],
                                     axis=1).astype(o_ref.dtype)

    @pl.when(i >= nv_ref[0])
    def _():
        o_ref[...] = jnp.zeros(o_ref.shape, o_ref.dtype)


def _expert_ffn(xs, slot_w, blk_exp, n_valid, wg, wu, wd, layer, tb):
    P, D = xs.shape
    grid_spec = pltpu.PrefetchScalarGridSpec(
        num_scalar_prefetch=2,
        grid=(P // tb,),
        in_specs=[pl.BlockSpec((tb, D), lambda i, be, nv: (i, 0)),
                  pl.BlockSpec((tb, LANES), lambda i, be, nv: (i, 0)),
                  pl.BlockSpec((1, 1, D, D_EXPERT), lambda i, be, nv: (layer, be[i], 0, 0)),
                  pl.BlockSpec((1, 1, D, D_EXPERT), lambda i, be, nv: (layer, be[i], 0, 0)),
                  pl.BlockSpec((1, 1, D_EXPERT, D), lambda i, be, nv: (layer, be[i], 0, 0))],
        out_specs=pl.BlockSpec((tb, D), lambda i, be, nv: (i, 0)),
        scratch_shapes=[pltpu.VMEM((D, D_EXPERT), BF16), pltpu.VMEM((D, D_EXPERT), BF16),
                        pltpu.VMEM((D_EXPERT, D), BF16)],
    )
    return pl.pallas_call(
        _expert_body,
        grid_spec=grid_spec,
        out_shape=jax.ShapeDtypeStruct((P, D), BF16),
        compiler_params=_cparams("arbitrary"),
        name="expert_ffn",
    )(blk_exp, n_valid, xs, jnp.broadcast_to(slot_w[:, None], (P, LANES)), wg, wu, wd)


def _final_norm_body(*refs):
    x_ref, y_refs, (g_ref, o_ref) = refs[0], refs[1:-2], refs[-2:]
    o_ref[...] = _rms(_residual_sum(x_ref, y_refs), g_ref[...])


def _final_norm(x, adds, g, row0, rows, tm):
    D = x.shape[1]
    first = row0 // tm
    src = pl.BlockSpec((tm, D), lambda i: (first + i, 0))
    return pl.pallas_call(
        _final_norm_body, grid=(rows // tm,),
        in_specs=[src] * (1 + len(adds)) + [_resident((1, D))],
        out_specs=pl.BlockSpec((tm, D), lambda i: (i, 0)),
        out_shape=jax.ShapeDtypeStruct((rows, D), F32), compiler_params=_cparams("parallel"), name="final_norm",
    )(x, *adds, g)


def _rope_tables(seq):
    def angles(pos, dim):
        inv = 1.0 / (ROPE_THETA ** (jnp.arange(0, dim, 2, dtype=F32) / dim))
        ang = pos.astype(F32)[:, None] * inv[None, :]
        return jnp.cos(ang), jnp.sin(ang)
    pos = jnp.arange(seq)
    c1, s1 = angles(pos, A_ROPE)
    cr, sr = angles(pos // GRID_W, D_HEAD // 2)
    cc, sc = angles(pos % GRID_W, D_HEAD // 2)
    pad1, pad0 = jnp.ones((seq, ROPE_PAD), F32), jnp.zeros((seq, ROPE_PAD), F32)
    return (jnp.concatenate([c1, c1, pad1], 1), jnp.concatenate([-s1, s1, pad0], 1),
            jnp.concatenate([cr, cr, cc, cc], 1), jnp.concatenate([-sr, sr, -sc, sc], 1))


def _layer_weights(l, w_in, w_uq, w_ukv, w_route_group, w_route_expert):
    D = D_MODEL
    k_r_end = A_Q_LORA + A_KV_LORA + A_ROPE
    win = jnp.concatenate([w_in[l][:, :k_r_end], jnp.zeros((D, ROPE_PAD), F32), w_in[l][:, k_r_end:]], 1)
    uq = w_uq[l].reshape(A_Q_LORA, A_HEADS, A_NOPE + A_ROPE)
    uq_rope = jnp.pad(uq[:, :, A_NOPE:], ((0, 0), (0, 0), (0, ROPE_PAD)))
    wuq = jnp.concatenate([uq[:, :, :A_NOPE].reshape(A_Q_LORA, -1), uq_rope.reshape(A_Q_LORA, -1)], 1)
    ukv = w_ukv[l].reshape(A_KV_LORA, A_HEADS, A_NOPE + A_V)
    wukv = jnp.concatenate([ukv[:, :, :A_NOPE].reshape(A_KV_LORA, -1), ukv[:, :, A_NOPE:].reshape(A_KV_LORA, -1)], 1)
    wr = jnp.concatenate([w_route_group[l], w_route_expert[l],
                          jnp.zeros((D, ROUTER_W - N_GROUPS - N_EXPERTS), F32)], 1)
    wr_hi = wr.astype(BF16)
    wr_lo = (wr - wr_hi.astype(F32)).astype(BF16)
    return win.astype(BF16), wuq.astype(BF16), wukv.astype(BF16), jnp.concatenate([wr_hi, wr_lo], 1)


def kernel(x_prompt, x_sample, norm1_g, w_in, q_norm_g, kv_norm_g, w_uq, w_ukv, lam_q1, lam_k1, lam_q2, lam_k2,
           rel_bias, w_pool, pool_scale, qk_norm_q, qk_norm_k, w_lift, w_gate, b_gate, w_out, norm2_g,
           w_route_group, b_route_group, w_route_expert, b_route_expert, w_exp_gate, w_exp_up, w_exp_down,
           final_g):
    assert x_prompt.shape[1:] == x_sample.shape[1:], "both request groups must share (seq, d_model)"
    n_prompt, S, D = x_prompt.shape
    B = n_prompt + x_sample.shape[0]
    T = B * S
    t = _tiles(S, T)
    x = None
    depth = w_in.shape[0]
    tabs = _rope_tables(S)
    diag, corner, far = _diff_bias_tables(rel_bias, t["tq"])
    row2 = lambda v: v.reshape(1, -1).astype(F32)

    moe_out = ()
    for l in range(depth):
        lam_init = 0.8 - 0.6 * math.exp(-0.3 * l)
        lam = (jnp.exp(jnp.sum(lam_q1[l] * lam_k1[l])) - jnp.exp(jnp.sum(lam_q2[l] * lam_k2[l])) + lam_init)
        win, wuq, wukv, wr = _layer_weights(l, w_in, w_uq, w_ukv, w_route_group, w_route_expert)

        xs = (x_prompt.reshape(-1, D), x_sample.reshape(-1, D)) if l == 0 else (x,)
        prep = (row2(q_norm_g[l]), row2(kv_norm_g[l]), wuq, wukv, row2(qk_norm_q[l]), row2(qk_norm_k[l])) + tabs
        x, h, zb, zc, qa, ka, va, qd, kd, vd = _norm_inproj(xs, moe_out, row2(norm1_g[l]), win, prep, S, t["tm_in"])
        seq3 = lambda a: a.reshape(B, S, a.shape[-1])
        rep = D_HEADS // D_KV_HEADS
        o_a = _flash(seq3(qa), seq3(ka), seq3(va), kv_heads=A_HEADS, stack=1, q_w=A_QK_W, k_col0=0, v_col0=0,
                     tq=t["rows"], ts=t["tq"], tk=t["tk"])
        o_b = _diff_attn(seq3(zb), lam.reshape(1).astype(F32), diag, corner, far, tq=t["tq"], tk=t["tk"],
                         out_scale=1.0 - lam_init)
        o_c = _pool(seq3(zc), w_pool[l].astype(BF16), row2(pool_scale[l]), t["pool_chunk"])
        o_d = _flash(seq3(qd), seq3(kd), seq3(vd), kv_heads=D_KV_HEADS, stack=rep, q_w=D_HEAD, k_col0=0,
                     v_col0=0, tq=t["rows"] // rep, ts=t["tq"], tk=t["tk"])
        outs = [o.reshape(T, BRANCH_W) for o in (o_a, o_b, o_c, o_d)]
        merged = _gated_merge(h, outs, w_gate[l].astype(BF16), b_gate[l].astype(F32), w_lift[l].astype(BF16),
                              t["tm_merge"], t["tn"])
        route_bias = jnp.concatenate([b_route_group[l], b_route_expert[l],
                                      jnp.zeros((ROUTER_W - N_GROUPS - N_EXPERTS,), F32)]).reshape(1, ROUTER_W)
        x, h2, routed = _out_proj(merged, x, w_out[l].astype(BF16), row2(norm2_g[l]), wr, route_bias.astype(F32),
                                  t["tm"])
        slot_tok, slot_w, blk_exp, n_valid, slot_of = _dispatch(routed, t["tb"])
        ys = _expert_ffn(h2[slot_tok], slot_w, blk_exp, n_valid, w_exp_gate, w_exp_up, w_exp_down, l, t["tb"])
        moe_out = tuple(ys[slot_of[k]] for k in range(TOP_K))

    t_prompt = n_prompt * S
    y_prompt = _final_norm(x, moe_out, row2(final_g), 0, t_prompt, t["tm"]).reshape(n_prompt, S, D)
    y_sample = _final_norm(x, moe_out, row2(final_g), t_prompt, T - t_prompt, t["tm"]).reshape(B - n_prompt, S, D)
    return y_prompt, y_sample
```

```python
import functools
import math

import jax
import jax.numpy as jnp
from jax import lax
from jax.experimental import pallas as pl
from jax.experimental.pallas import tpu as pltpu

F32 = jnp.float32
BF16 = jnp.bfloat16
LOG2E = 1.4426950408889634

D_MODEL = 2048
GRID_W = 64
BRANCH_W = 512
N_BRANCH = 4
ROPE_THETA = 10000.0
EPS = 1e-6
A_HEADS, A_Q_LORA, A_KV_LORA, A_NOPE, A_ROPE, A_V = 4, 384, 128, 128, 64, 128
B_HEADS, B_QK, B_V = 4, 64, 128
REL_BUCKETS, REL_MAX_DIST = 32, 128
POOL_WINDOWS = (2, 4, 8, 16)
C_GROUPS, C_GROUP_W = 4, 128
D_HEADS, D_KV_HEADS, D_HEAD = 4, 2, 128
N_GROUPS, EXP_PER_GROUP, N_EXPERTS, TOP_K, D_EXPERT = 4, 8, 32, 2, 512

LANES = 128
ROPE_PAD = LANES - A_ROPE
ZA_W = A_Q_LORA + A_KV_LORA + LANES
ZB_W = 3 * B_HEADS * B_V
ZC_W = BRANCH_W
ZD_W = (D_HEADS + 2 * D_KV_HEADS) * D_HEAD
A_QK_W = 2 * LANES
ROUTER_W = LANES
HALO = 16
VMEM_LIMIT = 56 * 2 ** 20


def _tiles(seq, tokens):
    def pick(n, pref):
        t = min(pref, n)
        while n % t:
            t //= 2
        return t
    return dict(
        tm=pick(seq, 512),
        tm_in=pick(seq, 256),
        tq=pick(seq, 512),
        rows=pick(seq, 1024),
        tk=pick(seq, 4096),
        tm_merge=pick(tokens, 1024),
        tn=512,
        tb=512,
        pool_chunk=pick(seq, 1024),
    )


def _cparams(*sem):
    return pltpu.CompilerParams(dimension_semantics=sem, vmem_limit_bytes=VMEM_LIMIT)


def _resident(shape):
    zeros = (0,) * len(shape)
    return pl.BlockSpec(shape, lambda *_: zeros, pipeline_mode=pl.Buffered(1))


def _rms(x, g=None):
    y = x * lax.rsqrt(jnp.mean(x * x, axis=-1, keepdims=True) + EPS)
    return y if g is None else y * g


def _rope_lanes(x, c, s):
    lane = lax.broadcasted_iota(jnp.int32, x.shape, 1)
    first_half = (lane & 32) == 0
    partner = jnp.where(first_half, pltpu.roll(x, 96, 1), pltpu.roll(x, 32, 1))
    return x * c + partner * s


def _residual_sum(x_ref, y_refs):
    x = x_ref[...]
    for y_ref in y_refs:
        x = x + y_ref[...].astype(F32)
    return x


N_PREP_PARAMS = 10
PREP_OUT_W = (A_HEADS * A_QK_W, A_HEADS * A_QK_W, A_HEADS * A_V, D_HEADS * D_HEAD, D_KV_HEADS * D_HEAD,
              D_KV_HEADS * D_HEAD)


def _norm_inproj_body(*refs, n_x, n_add, first_rows):
    x_refs, y_refs = refs[:n_x], refs[n_x:n_x + n_add]
    g_ref, w_ref = refs[n_x + n_add:n_x + n_add + 2]
    prep_refs = refs[n_x + n_add + 2:n_x + n_add + 2 + N_PREP_PARAMS]
    outs = refs[n_x + n_add + 2 + N_PREP_PARAMS:]
    if n_x == 2:
        x = jnp.where(pl.program_id(0) < first_rows, x_refs[0][...], x_refs[1][...])
    else:
        x = _residual_sum(x_refs[0], y_refs)
    if n_x == 2 or n_add:
        outs[0][...] = x
        outs = outs[1:]
    h_ref, zb_ref, zc_ref = outs[:3]
    h = _rms(x, g_ref[...]).astype(BF16)
    h_ref[...] = h
    proj = lambda lo, w_: jnp.dot(h, w_ref[:, lo:lo + w_], preferred_element_type=F32)
    zb_ref[...] = proj(ZA_W, ZB_W).astype(BF16)
    zc_ref[...] = proj(ZA_W + ZB_W, ZC_W).astype(BF16)
    _mixer_prep(proj(0, ZA_W), proj(ZA_W + ZB_W + ZC_W, ZD_W), prep_refs, outs[3:])


def _norm_inproj(xs, adds, g, w, prep, seq, tm):
    D = xs[0].shape[1]
    T = sum(x.shape[0] for x in xs)
    per_seq = seq // tm
    row = lambda w_: pl.BlockSpec((tm, w_), lambda i: (i, 0))
    if len(xs) == 2:
        na = xs[0].shape[0] // tm
        x_specs = [pl.BlockSpec((tm, D), lambda i: (jnp.minimum(i, na - 1), 0)),
                   pl.BlockSpec((tm, D), lambda i: (jnp.maximum(i - na, 0), 0))]
    else:
        na, x_specs = None, [row(D)]
    new_x = len(xs) == 2 or bool(adds)
    sum_spec, sum_shape = ([row(D)], [jax.ShapeDtypeStruct((T, D), F32)]) if new_x else ([], [])
    tab = pl.BlockSpec((tm, LANES), lambda i: (i % per_seq, 0))
    out_w = (D, ZB_W, ZC_W) + PREP_OUT_W
    outs = pl.pallas_call(
        functools.partial(_norm_inproj_body, n_x=len(xs), n_add=len(adds), first_rows=na),
        grid=(T // tm,),
        in_specs=x_specs + [row(D)] * len(adds) + [_resident((1, D)), _resident(w.shape)]
        + [_resident(p.shape) for p in prep[:N_PREP_PARAMS - 4]] + [tab] * 4,
        out_specs=sum_spec + [row(w_) for w_ in out_w],
        out_shape=sum_shape + [jax.ShapeDtypeStruct((T, w_), BF16) for w_ in out_w],
        compiler_params=_cparams("parallel"),
        name="norm_inproj",
    )(*xs, *adds, g, w, *prep)
    return list(outs) if new_x else [xs[0]] + list(outs)


def _mixer_prep(za, zd, prep_refs, out_refs):
    gq_ref, gkv_ref, wuq_ref, wukv_ref, gdq_ref, gdk_ref, ca_ref, sa_ref, cd_ref, sd_ref = prep_refs
    qa_ref, ka_ref, va_ref, qd_ref, kd_ref, vd_ref = out_refs
    a_scale = (A_NOPE + A_ROPE) ** -0.5 * LOG2E
    d_scale = D_HEAD ** -0.5 * LOG2E
    ca, sa, cd, sd = ca_ref[...], sa_ref[...], cd_ref[...], sd_ref[...]
    cq = _rms(za[:, :A_Q_LORA], gq_ref[...]).astype(BF16)
    q = jnp.dot(cq, wuq_ref[...], preferred_element_type=F32)
    ckv = _rms(za[:, A_Q_LORA:A_Q_LORA + A_KV_LORA], gkv_ref[...]).astype(BF16)
    kv = jnp.dot(ckv, wukv_ref[...], preferred_element_type=F32)
    k_rope = _rope_lanes(za[:, A_Q_LORA + A_KV_LORA:], ca, sa).astype(BF16)
    nope_w = A_HEADS * A_NOPE
    for h in range(A_HEADS):
        lo = h * A_QK_W
        qa_ref[:, lo:lo + LANES] = (q[:, h * LANES:(h + 1) * LANES] * a_scale).astype(BF16)
        q_rope = _rope_lanes(q[:, nope_w + h * LANES:nope_w + (h + 1) * LANES], ca, sa)
        qa_ref[:, lo + LANES:lo + 2 * LANES] = (q_rope * a_scale).astype(BF16)
        ka_ref[:, lo:lo + LANES] = kv[:, h * LANES:(h + 1) * LANES].astype(BF16)
        ka_ref[:, lo + LANES:lo + 2 * LANES] = k_rope
    va_ref[...] = kv[:, nope_w:].astype(BF16)
    for h in range(D_HEADS):
        xh = _rms(zd[:, h * D_HEAD:(h + 1) * D_HEAD], gdq_ref[...])
        qd_ref[:, h * D_HEAD:(h + 1) * D_HEAD] = (_rope_lanes(xh, cd, sd) * d_scale).astype(BF16)
    k_off = D_HEADS * D_HEAD
    for h in range(D_KV_HEADS):
        xh = _rms(zd[:, k_off + h * D_HEAD:k_off + (h + 1) * D_HEAD], gdk_ref[...])
        kd_ref[:, h * D_HEAD:(h + 1) * D_HEAD] = _rope_lanes(xh, cd, sd).astype(BF16)
    vd_ref[...] = zd[:, k_off + D_KV_HEADS * D_HEAD:].astype(BF16)


def _qk(q, k):
    return lax.dot_general(q, k, (((1,), (1,)), ((), ())), preferred_element_type=F32)


def _lane_fold(fn, acc, x):
    for c in range(x.shape[1] // LANES):
        acc = fn(acc, x[:, c * LANES:(c + 1) * LANES])
    return acc


def _lane_blocks(x):
    return [x[:, i * LANES:(i + 1) * LANES] for i in range(x.shape[1] // LANES)]


def _score_pass(q, k_ref, s_ref, tk, sub_max=None):
    tq, ts = s_ref.shape[1:]
    sub = tk // ts

    def step(j, mrun):
        ks = pl.multiple_of(j * tk, tk)
        s = _qk(q, k_ref[0, pl.ds(ks, tk), :])
        for c in range(sub):
            jj = j * sub + c
            sc = s[:, c * ts:(c + 1) * ts]
            s_ref[jj] = sc
            blocks = _lane_blocks(sc)
            m_sub = functools.reduce(jnp.maximum, blocks) if sub_max is None else sub_max(jj, blocks)
            mrun = jnp.maximum(mrun, m_sub)
        return mrun

    return lax.fori_loop(0, k_ref.shape[1] // tk, step, jnp.full((tq, LANES), -jnp.inf, F32))


def _value_pass(v_ref, s_ref, tk, m_of):
    tq, ts = s_ref.shape[1:]
    sub = tk // ts
    dv = v_ref.shape[2]
    ones = jnp.ones((tk, LANES), BF16)

    def step(j, acc):
        ks = pl.multiple_of(j * tk, tk)
        p = []
        for c in range(sub):
            jj = j * sub + c
            m = m_of(jj)
            s = s_ref[jj]
            p += [jnp.exp2((s[:, i * LANES:(i + 1) * LANES] - m).astype(BF16)) for i in range(ts // LANES)]
        v_ext = jnp.concatenate([v_ref[0, pl.ds(ks, tk), :], ones], axis=1)
        return acc + jnp.dot(jnp.concatenate(p, axis=1), v_ext, preferred_element_type=F32)

    acc = lax.fori_loop(0, v_ref.shape[1] // tk, step, jnp.zeros((tq, dv + LANES), F32))
    return acc[:, :dv] / acc[:, dv:]


def _row_max(mrun):
    return jnp.broadcast_to(jnp.max(mrun, axis=-1, keepdims=True), mrun.shape)


def _flash_body(q_ref, k_ref, v_ref, o_ref, s_ref, *, tk, stack):
    tq = q_ref.shape[1]
    q_w = q_ref.shape[2] // stack
    q = jnp.concatenate([q_ref[0, :, c * q_w:(c + 1) * q_w] for c in range(stack)], axis=0)
    m = _row_max(_score_pass(q, k_ref, s_ref, tk))
    o = _value_pass(v_ref, s_ref, tk, lambda jj: m).astype(o_ref.dtype)
    o_ref[0] = jnp.concatenate([o[c * tq:(c + 1) * tq] for c in range(stack)], axis=1)


def _flash(q, k, v, *, kv_heads, stack, q_w, k_col0, v_col0, tq, ts, tk):
    B, S, _ = q.shape
    dv = LANES
    return pl.pallas_call(
        functools.partial(_flash_body, tk=tk, stack=stack),
        grid=(B, kv_heads, S // tq),
        in_specs=[pl.BlockSpec((1, tq, stack * q_w), lambda b, h, i: (b, i, h)),
                  pl.BlockSpec((1, S, q_w), lambda b, h, i: (b, 0, k_col0 + h)),
                  pl.BlockSpec((1, S, dv), lambda b, h, i: (b, 0, v_col0 + h))],
        out_specs=pl.BlockSpec((1, tq, stack * dv), lambda b, h, i: (b, i, h)),
        out_shape=jax.ShapeDtypeStruct((B, S, kv_heads * stack * dv), BF16),
        scratch_shapes=[pltpu.VMEM((S // ts, stack * tq, ts), F32)],
        compiler_params=_cparams("parallel", "parallel", "arbitrary"),
        name="flash_attn",
    )(q, k, v)


def _diff_body(lam_ref, cb_ref, q_ref, k_ref, v_ref, diag_ref, corner_ref, o_ref, s_ref, *, tk, out_scale):
    h, qi = pl.program_id(1), pl.program_id(2)
    n_sub = s_ref.shape[0]
    tq = q_ref.shape[1]
    nd = REL_MAX_DIST
    qf = q_ref[0].astype(F32) * (B_QK ** -0.5 * LOG2E)
    lane = lax.broadcasted_iota(jnp.int32, qf.shape, 1)
    c_lo, c_hi = cb_ref[0, h], cb_ref[1, h]
    side_shift = lambda jj: jnp.where(jj < qi, c_lo, jnp.where(jj > qi, c_hi, 0.0))
    tile_max = lambda s: _lane_fold(jnp.maximum, s[:, :LANES], s[:, LANES:])

    q = jnp.concatenate([jnp.where(lane < B_QK, qf, 0.0), jnp.where(lane >= B_QK, qf, 0.0)], axis=0).astype(BF16)
    row = lax.broadcasted_iota(jnp.int32, (2 * tq, LANES), 0) % tq
    pen_lo = jnp.where(row < nd, -jnp.inf, 0.0)
    pen_hi = jnp.where(row >= tq - nd, -jnp.inf, 0.0)
    zero = jnp.zeros_like(pen_lo)

    def sub_max(jj, blocks):
        blocks = list(blocks)
        blocks[-1] = blocks[-1] + jnp.where(jj == qi - 1, pen_lo, zero)
        blocks[0] = blocks[0] + jnp.where(jj == qi + 1, pen_hi, zero)
        shift = jnp.where(jj < qi, c_lo, jnp.where(jj > qi, c_hi, -jnp.inf))
        return functools.reduce(jnp.maximum, blocks) + shift

    mrun = _score_pass(q, k_ref, s_ref, tk, sub_max=sub_max)

    diag = diag_ref[0]
    s = s_ref[qi] + jnp.concatenate([diag, diag], axis=0)
    s_ref[qi] = s
    mrun = jnp.maximum(mrun, tile_max(s))

    def neighbour(jj, rows0, cols0, corner, c):
        def fix(mrun):
            for r in (rows0, tq + rows0):
                fixed = s_ref[jj, r:r + nd, cols0:cols0 + nd] + corner
                s_ref[jj, r:r + nd, cols0:cols0 + nd] = fixed
                pieces = [mrun[:r], jnp.maximum(mrun[r:r + nd], fixed + c), mrun[r + nd:]]
                mrun = jnp.concatenate([p for p in pieces if p.shape[0]], axis=0)
            return mrun
        return fix

    mrun = lax.cond(qi >= 1, neighbour(jnp.maximum(qi - 1, 0), 0, tq - nd, corner_ref[0, 0], c_lo),
                    lambda m_: m_, mrun)
    mrun = lax.cond(qi + 1 < n_sub, neighbour(jnp.minimum(qi + 1, n_sub - 1), tq - nd, 0, corner_ref[0, 1], c_hi),
                    lambda m_: m_, mrun)
    m = _row_max(mrun)
    o = _value_pass(v_ref, s_ref, tk, lambda jj: m - side_shift(jj))
    o = o[:tq] - lam_ref[0] * o[tq:]
    o_ref[0] = (_rms(o) * out_scale).astype(o_ref.dtype)


def _rel_bucket(rel):
    half = REL_BUCKETS // 2
    max_exact = half // 2
    ret = (rel > 0).astype(jnp.int32) * half
    n = jnp.abs(rel)
    large = max_exact + (jnp.log(jnp.maximum(n, 1).astype(F32) / max_exact)
                         / math.log(REL_MAX_DIST / max_exact) * (half - max_exact)).astype(jnp.int32)
    large = jnp.minimum(large, half - 1)
    return ret + jnp.where(n < max_exact, n, large)


def _diff_bias_tables(rel_bias, tq):
    nd = REL_MAX_DIST
    rb = rel_bias.astype(F32) * LOG2E
    far = jnp.stack([rb[REL_BUCKETS // 2 - 1], rb[REL_BUCKETS - 1]])

    def table(rel):
        bucket = _rel_bucket(rel)
        out = jnp.zeros((B_HEADS,) + rel.shape, F32)
        for b in range(REL_BUCKETS):
            out = jnp.where(bucket[None] == b, rb[b].reshape((B_HEADS,) + (1,) * rel.ndim), out)
        return out

    a, e = jnp.arange(tq), jnp.arange(nd)
    diag = table(a[None, :] - a[:, None])
    lo = table((tq - nd + e[None, :]) - tq - e[:, None]) - far[0][:, None, None]
    hi = table(e[None, :] + tq - (tq - nd + e[:, None])) - far[1][:, None, None]
    return diag, jnp.stack([lo, hi], axis=1), far


def _diff_attn(zb, lam, diag, corner, far, *, tq, tk, out_scale):
    B, S, _ = zb.shape
    nd = REL_MAX_DIST
    assert tq >= nd, "far key chunks must lie beyond the last distinct relative bucket"
    H = B_HEADS
    smem = pl.BlockSpec(memory_space=pltpu.SMEM)
    return pl.pallas_call(
        functools.partial(_diff_body, tk=tk, out_scale=out_scale),
        grid=(B, H, S // tq),
        in_specs=[smem, smem,
                  pl.BlockSpec((1, tq, B_V), lambda b, h, i: (b, i, h)),
                  pl.BlockSpec((1, S, B_V), lambda b, h, i: (b, 0, H + h)),
                  pl.BlockSpec((1, S, B_V), lambda b, h, i: (b, 0, 2 * H + h)),
                  pl.BlockSpec((1, tq, tq), lambda b, h, i: (h, 0, 0)),
                  pl.BlockSpec((1, 2, nd, nd), lambda b, h, i: (h, 0, 0, 0))],
        out_specs=pl.BlockSpec((1, tq, B_V), lambda b, h, i: (b, i, h)),
        out_shape=jax.ShapeDtypeStruct((B, S, H * B_V), BF16),
        scratch_shapes=[pltpu.VMEM((S // tq, 2 * tq, tq), F32)],
        compiler_params=_cparams("parallel", "parallel", "arbitrary"),
        name="diff_attn",
    )(lam, far, zb, zb, zb, diag, corner)


def _pool_body(u_ref, w_ref, sc_ref, o_ref, pad_ref, *, chunk):
    g = pl.program_id(1)
    S = u_ref.shape[1]
    pad_ref[0:HALO, :] = jnp.zeros((HALO, LANES), F32)
    pad_ref[HALO + S:, :] = jnp.zeros((HALO, LANES), F32)
    pad_ref[HALO:HALO + S, :] = u_ref[0].astype(F32)
    w_mat, sc = w_ref[0], sc_ref[...]

    def pooled(win):
        def body(c, _):
            r0 = pl.multiple_of(c * chunk, chunk)
            tot = pad_ref[pl.ds(r0 + HALO - win // 2, chunk), :]
            for j in range(1 - win // 2, win // 2):
                tot = tot + pad_ref[pl.ds(r0 + HALO + j, chunk), :]
            t = r0 + lax.broadcasted_iota(jnp.int32, (chunk, 1), 0)
            cnt = jnp.clip(t - win // 2 + win, 0, S) - jnp.clip(t - win // 2, 0, S)
            d = tot / cnt.astype(F32) - pad_ref[pl.ds(r0 + HALO, chunk), :]
            y = jnp.dot(d.astype(BF16), w_mat, preferred_element_type=F32) * sc
            o_ref[0, pl.ds(r0, chunk), :] = y.astype(o_ref.dtype)
            return 0
        lax.fori_loop(0, S // chunk, body, 0)

    for gi, win in enumerate(POOL_WINDOWS):
        pl.when(g == gi)(functools.partial(pooled, win))


def _pool(zc, w_pool, pool_scale, chunk):
    B, S, _ = zc.shape
    blk = pl.BlockSpec((1, S, LANES), lambda b, g: (b, 0, g))
    return pl.pallas_call(
        functools.partial(_pool_body, chunk=chunk),
        grid=(B, C_GROUPS),
        in_specs=[blk, pl.BlockSpec((1, C_GROUP_W, C_GROUP_W), lambda b, g: (g, 0, 0)),
                  pl.BlockSpec((1, LANES), lambda b, g: (0, g))],
        out_specs=blk,
        out_shape=jax.ShapeDtypeStruct((B, S, BRANCH_W), BF16),
        scratch_shapes=[pltpu.VMEM((S + 2 * HALO, LANES), F32)],
        compiler_params=_cparams("parallel", "arbitrary"),
        name="pool_mixer",
    )(zc, w_pool, pool_scale)


def _gated_merge_body(h_ref, oa_ref, ob_ref, oc_ref, od_ref, wg_ref, bg_ref, wl_ref, m_ref):
    h = h_ref[...]
    merged = None
    for b, o_ref in enumerate((oa_ref, ob_ref, oc_ref, od_ref)):
        gate = jax.nn.sigmoid(jnp.dot(h, wg_ref[b], preferred_element_type=F32) + bg_ref[b:b + 1, :])
        term = gate * jnp.dot(o_ref[...], wl_ref[b], preferred_element_type=F32)
        merged = term if merged is None else merged + term
    m_ref[...] = merged.astype(BF16)


def _gated_merge(h, outs, wg, bg, wl, tm, tn):
    T, D = h.shape
    row = lambda w_: pl.BlockSpec((tm, w_), lambda i, j: (i, 0))
    return pl.pallas_call(
        _gated_merge_body,
        grid=(T // tm, D // tn),
        in_specs=[row(D)] + [row(BRANCH_W)] * N_BRANCH + [
            pl.BlockSpec((N_BRANCH, D, tn), lambda i, j: (0, 0, j)),
            pl.BlockSpec((N_BRANCH, tn), lambda i, j: (0, j)),
            pl.BlockSpec((N_BRANCH, BRANCH_W, tn), lambda i, j: (0, 0, j))],
        out_specs=pl.BlockSpec((tm, tn), lambda i, j: (i, j)),
        out_shape=jax.ShapeDtypeStruct((T, D), BF16),
        compiler_params=_cparams("parallel", "arbitrary"),
        name="gated_merge",
    )(h, *outs, wg, bg, wl)


ROUTE_ROWS = 8


def _out_proj_body(m_ref, x_ref, wo_ref, g2_ref, wr_ref, rb_ref, xn_ref, h2_ref, rt_ref):
    xn = x_ref[...] + jnp.dot(m_ref[...], wo_ref[...], preferred_element_type=F32)
    xn_ref[...] = xn
    h2 = _rms(xn, g2_ref[...])
    h2_ref[...] = h2.astype(BF16)
    h_hi = h2.astype(BF16)
    h_lo = (h2 - h_hi.astype(F32)).astype(BF16)
    hi = jnp.dot(h_hi, wr_ref[...], preferred_element_type=F32)
    logits = hi[:, :ROUTER_W] + hi[:, ROUTER_W:] + jnp.dot(h_lo, wr_ref[:, :ROUTER_W], preferred_element_type=F32)
    rt_ref[...] = _route_tile(logits + rb_ref[...])


def _out_proj(merged, x, wo, g2, wr, route_bias, tm):
    T, D = x.shape
    row = pl.BlockSpec((tm, D), lambda i: (i, 0))
    return pl.pallas_call(
        _out_proj_body,
        grid=(T // tm,),
        in_specs=[row, row, _resident((D, D)), _resident((1, D)), _resident((D, 2 * ROUTER_W)),
                  _resident((1, ROUTER_W))],
        out_specs=[row, row, pl.BlockSpec((ROUTE_ROWS, tm), lambda i: (0, i))],
        out_shape=[jax.ShapeDtypeStruct((T, D), F32), jax.ShapeDtypeStruct((T, D), BF16),
                   jax.ShapeDtypeStruct((ROUTE_ROWS, T), F32)],
        compiler_params=_cparams("parallel"),
        name="out_proj_route",
    )(merged, x, wo, g2, wr, route_bias)


def _route_tile(lg):
    lane = lax.broadcasted_iota(jnp.int32, lg.shape, 1).astype(F32)
    neg = -jnp.inf
    row_max = lambda v: jnp.max(v, axis=-1, keepdims=True)
    first_at = lambda v, m: jnp.min(jnp.where(v == m, lane, float(LANES)), axis=-1, keepdims=True)
    g_lg = jnp.where(lane < N_GROUPS, lg, neg)
    g_max = row_max(g_lg)
    g_top = first_at(g_lg, g_max)
    pg_top = 1.0 / jnp.sum(jnp.exp(g_lg - g_max), axis=-1, keepdims=True)
    e_lo = N_GROUPS + EXP_PER_GROUP * g_top
    e_lg = jnp.where(jnp.logical_and(lane >= e_lo, lane < e_lo + EXP_PER_GROUP), lg, neg)
    picks, maxes = [], []
    for _ in range(TOP_K):
        m = row_max(e_lg)
        i = first_at(e_lg, m)
        picks.append(i)
        maxes.append(m)
        e_lg = jnp.where(lane == i, neg, e_lg)
    w = [jnp.exp(m - maxes[0]) for m in maxes]
    w_sum = functools.reduce(lambda a, b: a + b, w)
    cols = [i - N_GROUPS for i in picks] + [pg_top * wk / w_sum for wk in w]
    tile = jnp.zeros(lg.shape, F32)
    for c, v in enumerate(cols):
        tile = jnp.where(lane == c, v, tile)
    return tile.T[:ROUTE_ROWS]


def _dispatch(routed, tb):
    T = routed.shape[1]
    A = T * TOP_K
    flat_e = routed[:TOP_K].astype(jnp.int32).T.reshape(A)
    flat_w = routed[TOP_K:2 * TOP_K].T.reshape(A)
    iota = jnp.arange(A, dtype=jnp.int32)
    se, order, sw = lax.sort((flat_e, iota, flat_w), num_keys=1, is_stable=True)
    experts = jnp.arange(N_EXPERTS, dtype=jnp.int32)
    counts = jnp.sum((flat_e[:, None] == experts[None, :]).astype(jnp.int32), axis=0)
    start = jnp.cumsum(counts) - counts
    padded = (counts + tb - 1) // tb * tb
    pend = jnp.cumsum(padded)
    pstart = pend - padded
    n_blocks = -(-A // tb) + N_EXPERTS
    P = n_blocks * tb
    blk_first = jnp.arange(n_blocks, dtype=jnp.int32) * tb
    blk_exp = jnp.minimum(jnp.sum((pend[None, :] <= blk_first[:, None]).astype(jnp.int32), axis=1), N_EXPERTS - 1)
    n_valid = (pend[-1] // tb).astype(jnp.int32).reshape(1)
    slot = jnp.arange(P, dtype=jnp.int32)
    before = (pend[None, :] <= slot[:, None]).astype(jnp.int32)
    shift = jnp.sum(before * (padded - counts)[None, :], axis=1)
    in_pad = jnp.any(jnp.logical_and(slot[:, None] >= (pstart + counts)[None, :], slot[:, None] < pend[None, :]),
                     axis=1)
    valid = jnp.logical_and(jnp.logical_not(in_pad), slot < pend[-1])
    src = jnp.where(valid, slot - shift, slot % A)
    slot_tok = order[src] // TOP_K
    slot_w = jnp.where(valid, sw[src], 0.0)
    dest = iota + jnp.sum((se[:, None] == experts[None, :]).astype(jnp.int32) * (pstart - start)[None, :], axis=1)
    _, slot_of = lax.sort((order, dest), num_keys=1)
    return slot_tok, slot_w, blk_exp, n_valid, slot_of.reshape(T, TOP_K).T


def _expert_body(be_ref, nv_ref, x_ref, sw_ref, wg_ref, wu_ref, wd_ref, o_ref, wg_s, wu_s, wd_s):
    i = pl.program_id(0)

    @pl.when(jnp.logical_or(i == 0, be_ref[i] != be_ref[jnp.maximum(i - 1, 0)]))
    def _():
        wg_s[...] = wg_ref[0, 0].astype(BF16)
        wu_s[...] = wu_ref[0, 0].astype(BF16)
        wd_s[...] = wd_ref[0, 0].astype(BF16)

    @pl.when(i < nv_ref[0])
    def _():
        x = x_ref[...]
        g = jnp.dot(x, wg_s[...], preferred_element_type=F32)
        u = jnp.dot(x, wu_s[...], preferred_element_type=F32)
        a = (g * jax.nn.sigmoid(g) * u).astype(BF16)
        y = jnp.dot(a, wd_s[...], preferred_element_type=F32)
        sw = sw_ref[...]
        o_ref[...] = jnp.concatenate([y[:, c * LANES:(c + 1) * LANES] * sw for c in range(y.shape[1] // LANES)],
                                     axis=1).astype(o_ref.dtype)

    @pl.when(i >= nv_ref[0])
    def _():
        o_ref[...] = jnp.zeros(o_ref.shape, o_ref.dtype)


def _expert_ffn(xs, slot_w, blk_exp, n_valid, wg, wu, wd, layer, tb):
    P, D = xs.shape
    grid_spec = pltpu.PrefetchScalarGridSpec(
        num_scalar_prefetch=2,
        grid=(P // tb,),
        in_specs=[pl.BlockSpec((tb, D), lambda i, be, nv: (i, 0)),
                  pl.BlockSpec((tb, LANES), lambda i, be, nv: (i, 0)),
                  pl.BlockSpec((1, 1, D, D_EXPERT), lambda i, be, nv: (layer, be[i], 0, 0)),
                  pl.BlockSpec((1, 1, D, D_EXPERT), lambda i, be, nv: (layer, be[i], 0, 0)),
                  pl.BlockSpec((1, 1, D_EXPERT, D), lambda i, be, nv: (layer, be[i], 0, 0))],
        out_specs=pl.BlockSpec((tb, D), lambda i, be, nv: (i, 0)),
        scratch_shapes=[pltpu.VMEM((D, D_EXPERT), BF16), pltpu.VMEM((D, D_EXPERT), BF16),
                        pltpu.VMEM((D_EXPERT, D), BF16)],
    )
    return pl.pallas_call(
        _expert_body,
        grid_spec=grid_spec,
        out_shape=jax.ShapeDtypeStruct((P, D), BF16),
        compiler_params=_cparams("arbitrary"),
        name="expert_ffn",
    )(blk_exp, n_valid, xs, jnp.broadcast_to(slot_w[:, None], (P, LANES)), wg, wu, wd)


def _final_norm_body(*refs):
    x_ref, y_refs, (g_ref, o_ref) = refs[0], refs[1:-2], refs[-2:]
    o_ref[...] = _rms(_residual_sum(x_ref, y_refs), g_ref[...])


def _final_norm(x, adds, g, row0, rows, tm):
    D = x.shape[1]
    first = row0 // tm
    src = pl.BlockSpec((tm, D), lambda i: (first + i, 0))
    return pl.pallas_call(
        _final_norm_body, grid=(rows // tm,),
        in_specs=[src] * (1 + len(adds)) + [_resident((1, D))],
        out_specs=pl.BlockSpec((tm, D), lambda i: (i, 0)),
        out_shape=jax.ShapeDtypeStruct((rows, D), F32), compiler_params=_cparams("parallel"), name="final_norm",
    )(x, *adds, g)


def _rope_tables(seq):
    def angles(pos, dim):
        inv = 1.0 / (ROPE_THETA ** (jnp.arange(0, dim, 2, dtype=F32) / dim))
        ang = pos.astype(F32)[:, None] * inv[None, :]
        return jnp.cos(ang), jnp.sin(ang)
    pos = jnp.arange(seq)
    c1, s1 = angles(pos, A_ROPE)
    cr, sr = angles(pos // GRID_W, D_HEAD // 2)
    cc, sc = angles(pos % GRID_W, D_HEAD // 2)
    pad1, pad0 = jnp.ones((seq, ROPE_PAD), F32), jnp.zeros((seq, ROPE_PAD), F32)
    return (jnp.concatenate([c1, c1, pad1], 1), jnp.concatenate([-s1, s1, pad0], 1),
            jnp.concatenate([cr, cr, cc, cc], 1), jnp.concatenate([-sr, sr, -sc, sc], 1))


def _layer_weights(l, w_in, w_uq, w_ukv, w_route_group, w_route_expert):
    D = D_MODEL
    k_r_end = A_Q_LORA + A_KV_LORA + A_ROPE
    win = jnp.concatenate([w_in[l][:, :k_r_end], jnp.zeros((D, ROPE_PAD), F32), w_in[l][:, k_r_end:]], 1)
    uq = w_uq[l].reshape(A_Q_LORA, A_HEADS, A_NOPE + A_ROPE)
    uq_rope = jnp.pad(uq[:, :, A_NOPE:], ((0, 0), (0, 0), (0, ROPE_PAD)))
    wuq = jnp.concatenate([uq[:, :, :A_NOPE].reshape(A_Q_LORA, -1), uq_rope.reshape(A_Q_LORA, -1)], 1)
    ukv = w_ukv[l].reshape(A_KV_LORA, A_HEADS, A_NOPE + A_V)
    wukv = jnp.concatenate([ukv[:, :, :A_NOPE].reshape(A_KV_LORA, -1), ukv[:, :, A_NOPE:].reshape(A_KV_LORA, -1)], 1)
    wr = jnp.concatenate([w_route_group[l], w_route_expert[l],
                          jnp.zeros((D, ROUTER_W - N_GROUPS - N_EXPERTS), F32)], 1)
    wr_hi = wr.astype(BF16)
    wr_lo = (wr - wr_hi.astype(F32)).astype(BF16)
    return win.astype(BF16), wuq.astype(BF16), wukv.astype(BF16), jnp.concatenate([wr_hi, wr_lo], 1)


def kernel(x_prompt, x_sample, norm1_g, w_in, q_norm_g, kv_norm_g, w_uq, w_ukv, lam_q1, lam_k1, lam_q2, lam_k2,
           rel_bias, w_pool, pool_scale, qk_norm_q, qk_norm_k, w_lift, w_gate, b_gate, w_out, norm2_g,
           w_route_group, b_route_group, w_route_expert, b_route_expert, w_exp_gate, w_exp_up, w_exp_down,
           final_g):
    assert x_prompt.shape[1:] == x_sample.shape[1:], "both request groups must share (seq, d_model)"
    n_prompt, S, D = x_prompt.shape
    B = n_prompt + x_sample.shape[0]
    T = B * S
    t = _tiles(S, T)
    x = None
    depth = w_in.shape[0]
    tabs = _rope_tables(S)
    diag, corner, far = _diff_bias_tables(rel_bias, t["tq"])
    row2 = lambda v: v.reshape(1, -1).astype(F32)

    moe_out = ()
    for l in range(depth):
        lam_init = 0.8 - 0.6 * math.exp(-0.3 * l)
        lam = (jnp.exp(jnp.sum(lam_q1[l] * lam_k1[l])) - jnp.exp(jnp.sum(lam_q2[l] * lam_k2[l])) + lam_init)
        win, wuq, wukv, wr = _layer_weights(l, w_in, w_uq, w_ukv, w_route_group, w_route_expert)

        xs = (x_prompt.reshape(-1, D), x_sample.reshape(-1, D)) if l == 0 else (x,)
        prep = (row2(q_norm_g[l]), row2(kv_norm_g[l]), wuq, wukv, row2(qk_norm_q[l]), row2(qk_norm_k[l])) + tabs
        x, h, zb, zc, qa, ka, va, qd, kd, vd = _norm_inproj(xs, moe_out, row2(norm1_g[l]), win, prep, S, t["tm_in"])
        seq3 = lambda a: a.reshape(B, S, a.shape[-1])
        rep = D_HEADS // D_KV_HEADS
        o_a = _flash(seq3(qa), seq3(ka), seq3(va), kv_heads=A_HEADS, stack=1, q_w=A_QK_W, k_col0=0, v_col0=0,
                     tq=t["rows"], ts=t["tq"], tk=t["tk"])
        o_b = _diff_attn(seq3(zb), lam.reshape(1).astype(F32), diag, corner, far, tq=t["tq"], tk=t["tk"],
                         out_scale=1.0 - lam_init)
        o_c = _pool(seq3(zc), w_pool[l].astype(BF16), row2(pool_scale[l]), t["pool_chunk"])
        o_d = _flash(seq3(qd), seq3(kd), seq3(vd), kv_heads=D_KV_HEADS, stack=rep, q_w=D_HEAD, k_col0=0,
                     v_col0=0, tq=t["rows"] // rep, ts=t["tq"], tk=t["tk"])
        outs = [o.reshape(T, BRANCH_W) for o in (o_a, o_b, o_c, o_d)]
        merged = _gated_merge(h, outs, w_gate[l].astype(BF16), b_gate[l].astype(F32), w_lift[l].astype(BF16),
                              t["tm_merge"], t["tn"])
        route_bias = jnp.concatenate([b_route_group[l], b_route_expert[l],
                                      jnp.zeros((ROUTER_W - N_GROUPS - N_EXPERTS,), F32)]).reshape(1, ROUTER_W)
        x, h2, routed = _out_proj(merged, x, w_out[l].astype(BF16), row2(norm2_g[l]), wr, route_bias.astype(F32),
                                  t["tm"])
        slot_tok, slot_w, blk_exp, n_valid, slot_of = _dispatch(routed, t["tb"])
        ys = _expert_ffn(h2[slot_tok], slot_w, blk_exp, n_valid, w_exp_gate, w_exp_up, w_exp_down, l, t["tb"])
        moe_out = tuple(ys[slot_of[k]] for k in range(TOP_K))

    t_prompt = n_prompt * S
    y_prompt = _final_norm(x, moe_out, row2(final_g), 0, t_prompt, t["tm"]).reshape(n_prompt, S, D)
    y_sample = _final_norm(x, moe_out, row2(final_g), t_prompt, T - t_prompt, t["tm"]).reshape(B - n_prompt, S, D)
    return y_prompt, y_sample
```

```python
import functools
import math

import jax
import jax.numpy as jnp
from jax import lax
from jax.experimental import pallas as pl
from jax.experimental.pallas import tpu as pltpu

F32 = jnp.float32
BF16 = jnp.bfloat16
LOG2E = 1.4426950408889634

D_MODEL = 2048
GRID_W = 64
BRANCH_W = 512
N_BRANCH = 4
ROPE_THETA = 10000.0
EPS = 1e-6
A_HEADS, A_Q_LORA, A_KV_LORA, A_NOPE, A_ROPE, A_V = 4, 384, 128, 128, 64, 128
B_HEADS, B_QK, B_V = 4, 64, 128
REL_BUCKETS, REL_MAX_DIST = 32, 128
POOL_WINDOWS = (2, 4, 8, 16)
C_GROUPS, C_GROUP_W = 4, 128
D_HEADS, D_KV_HEADS, D_HEAD = 4, 2, 128
N_GROUPS, EXP_PER_GROUP, N_EXPERTS, TOP_K, D_EXPERT = 4, 8, 32, 2, 512

LANES = 128
ROPE_PAD = LANES - A_ROPE
ZA_W = A_Q_LORA + A_KV_LORA + LANES
ZB_W = 3 * B_HEADS * B_V
ZC_W = BRANCH_W
ZD_W = (D_HEADS + 2 * D_KV_HEADS) * D_HEAD
A_QK_W = 2 * LANES
ROUTER_W = LANES
HALO = 16
VMEM_LIMIT = 56 * 2 ** 20


def _tiles(seq, tokens):
    def pick(n, pref):
        t = min(pref, n)
        while n % t:
            t //= 2
        return t
    return dict(
        tm=pick(seq, 512),
        tm_in=pick(seq, 256),
        tq=pick(seq, 512),
        rows=pick(seq, 1024),
        tk=pick(seq, 8192),
        tm_merge=pick(tokens, 1024),
        tn=512,
        tb=512,
        pool_chunk=pick(seq, 1024),
    )


def _cparams(*sem):
    return pltpu.CompilerParams(dimension_semantics=sem, vmem_limit_bytes=VMEM_LIMIT)


def _resident(shape):
    zeros = (0,) * len(shape)
    return pl.BlockSpec(shape, lambda *_: zeros, pipeline_mode=pl.Buffered(1))


def _rms(x, g=None):
    y = x * lax.rsqrt(jnp.mean(x * x, axis=-1, keepdims=True) + EPS)
    return y if g is None else y * g


def _rope_lanes(x, c, s):
    lane = lax.broadcasted_iota(jnp.int32, x.shape, 1)
    first_half = (lane & 32) == 0
    partner = jnp.where(first_half, pltpu.roll(x, 96, 1), pltpu.roll(x, 32, 1))
    return x * c + partner * s


def _residual_sum(x_ref, y_refs):
    x = x_ref[...]
    for y_ref in y_refs:
        x = x + y_ref[...].astype(F32)
    return x


N_PREP_PARAMS = 10
PREP_OUT_W = (A_HEADS * A_QK_W, A_HEADS * A_QK_W, A_HEADS * A_V, D_HEADS * D_HEAD, D_KV_HEADS * D_HEAD,
              D_KV_HEADS * D_HEAD)


def _norm_inproj_body(*refs, n_x, n_add, first_rows):
    x_refs, y_refs = refs[:n_x], refs[n_x:n_x + n_add]
    g_ref, w_ref = refs[n_x + n_add:n_x + n_add + 2]
    prep_refs = refs[n_x + n_add + 2:n_x + n_add + 2 + N_PREP_PARAMS]
    outs = refs[n_x + n_add + 2 + N_PREP_PARAMS:]
    if n_x == 2:
        x = jnp.where(pl.program_id(0) < first_rows, x_refs[0][...], x_refs[1][...])
    else:
        x = _residual_sum(x_refs[0], y_refs)
    if n_x == 2 or n_add:
        outs[0][...] = x
        outs = outs[1:]
    h_ref, zb_ref, zc_ref = outs[:3]
    h = _rms(x, g_ref[...]).astype(BF16)
    h_ref[...] = h
    proj = lambda lo, w_: jnp.dot(h, w_ref[:, lo:lo + w_], preferred_element_type=F32)
    zb_ref[...] = proj(ZA_W, ZB_W).astype(BF16)
    zc_ref[...] = proj(ZA_W + ZB_W, ZC_W).astype(BF16)
    _mixer_prep(proj(0, ZA_W), proj(ZA_W + ZB_W + ZC_W, ZD_W), prep_refs, outs[3:])


def _norm_inproj(xs, adds, g, w, prep, seq, tm):
    D = xs[0].shape[1]
    T = sum(x.shape[0] for x in xs)
    per_seq = seq // tm
    row = lambda w_: pl.BlockSpec((tm, w_), lambda i: (i, 0))
    if len(xs) == 2:
        na = xs[0].shape[0] // tm
        x_specs = [pl.BlockSpec((tm, D), lambda i: (jnp.minimum(i, na - 1), 0)),
                   pl.BlockSpec((tm, D), lambda i: (jnp.maximum(i - na, 0), 0))]
    else:
        na, x_specs = None, [row(D)]
    new_x = len(xs) == 2 or bool(adds)
    sum_spec, sum_shape = ([row(D)], [jax.ShapeDtypeStruct((T, D), F32)]) if new_x else ([], [])
    tab = pl.BlockSpec((tm, LANES), lambda i: (i % per_seq, 0))
    out_w = (D, ZB_W, ZC_W) + PREP_OUT_W
    outs = pl.pallas_call(
        functools.partial(_norm_inproj_body, n_x=len(xs), n_add=len(adds), first_rows=na),
        grid=(T // tm,),
        in_specs=x_specs + [row(D)] * len(adds) + [_resident((1, D)), _resident(w.shape)]
        + [_resident(p.shape) for p in prep[:N_PREP_PARAMS - 4]] + [tab] * 4,
        out_specs=sum_spec + [row(w_) for w_ in out_w],
        out_shape=sum_shape + [jax.ShapeDtypeStruct((T, w_), BF16) for w_ in out_w],
        compiler_params=_cparams("parallel"),
        name="norm_inproj",
    )(*xs, *adds, g, w, *prep)
    return list(outs) if new_x else [xs[0]] + list(outs)


def _mixer_prep(za, zd, prep_refs, out_refs):
    gq_ref, gkv_ref, wuq_ref, wukv_ref, gdq_ref, gdk_ref, ca_ref, sa_ref, cd_ref, sd_ref = prep_refs
    qa_ref, ka_ref, va_ref, qd_ref, kd_ref, vd_ref = out_refs
    a_scale = (A_NOPE + A_ROPE) ** -0.5 * LOG2E
    d_scale = D_HEAD ** -0.5 * LOG2E
    ca, sa, cd, sd = ca_ref[...], sa_ref[...], cd_ref[...], sd_ref[...]
    cq = _rms(za[:, :A_Q_LORA], gq_ref[...]).astype(BF16)
    q = jnp.dot(cq, wuq_ref[...], preferred_element_type=F32)
    ckv = _rms(za[:, A_Q_LORA:A_Q_LORA + A_KV_LORA], gkv_ref[...]).astype(BF16)
    kv = jnp.dot(ckv, wukv_ref[...], preferred_element_type=F32)
    k_rope = _rope_lanes(za[:, A_Q_LORA + A_KV_LORA:], ca, sa).astype(BF16)
    nope_w = A_HEADS * A_NOPE
    for h in range(A_HEADS):
        lo = h * A_QK_W
        qa_ref[:, lo:lo + LANES] = (q[:, h * LANES:(h + 1) * LANES] * a_scale).astype(BF16)
        q_rope = _rope_lanes(q[:, nope_w + h * LANES:nope_w + (h + 1) * LANES], ca, sa)
        qa_ref[:, lo + LANES:lo + 2 * LANES] = (q_rope * a_scale).astype(BF16)
        ka_ref[:, lo:lo + LANES] = kv[:, h * LANES:(h + 1) * LANES].astype(BF16)
        ka_ref[:, lo + LANES:lo + 2 * LANES] = k_rope
    va_ref[...] = kv[:, nope_w:].astype(BF16)
    for h in range(D_HEADS):
        xh = _rms(zd[:, h * D_HEAD:(h + 1) * D_HEAD], gdq_ref[...])
        qd_ref[:, h * D_HEAD:(h + 1) * D_HEAD] = (_rope_lanes(xh, cd, sd) * d_scale).astype(BF16)
    k_off = D_HEADS * D_HEAD
    for h in range(D_KV_HEADS):
        xh = _rms(zd[:, k_off + h * D_HEAD:k_off + (h + 1) * D_HEAD], gdk_ref[...])
        kd_ref[:, h * D_HEAD:(h + 1) * D_HEAD] = _rope_lanes(xh, cd, sd).astype(BF16)
    vd_ref[...] = zd[:, k_off + D_KV_HEADS * D_HEAD:].astype(BF16)


def _qk(q, k):
    return lax.dot_general(q, k, (((1,), (1,)), ((), ())), preferred_element_type=F32)


def _lane_fold(fn, acc, x):
    for c in range(x.shape[1] // LANES):
        acc = fn(acc, x[:, c * LANES:(c + 1) * LANES])
    return acc


def _lane_blocks(x):
    return [x[:, i * LANES:(i + 1) * LANES] for i in range(x.shape[1] // LANES)]


def _score_pass(q, k_ref, s_ref, tk, n_steps, sub_max=None):
    tq, ts = s_ref.shape[1:]
    sub = tk // ts

    def step(j, mrun):
        ks = pl.multiple_of(j * tk, tk)
        s = _qk(q, k_ref[0, pl.ds(ks, tk), :])
        for c in range(sub):
            jj = j * sub + c
            sc = s[:, c * ts:(c + 1) * ts]
            s_ref[jj] = sc
            blocks = _lane_blocks(sc)
            m_sub = functools.reduce(jnp.maximum, blocks) if sub_max is None else sub_max(jj, blocks)
            mrun = jnp.maximum(mrun, m_sub)
        return mrun

    return lax.fori_loop(0, n_steps, step, jnp.full((tq, LANES), -jnp.inf, F32))


def _value_pass(v_ref, s_ref, tk, n_steps, m_of):
    tq, ts = s_ref.shape[1:]
    sub = tk // ts
    dv = v_ref.shape[2]
    ones = jnp.ones((tk, LANES), BF16)

    def step(j, acc):
        ks = pl.multiple_of(j * tk, tk)
        p = []
        for c in range(sub):
            jj = j * sub + c
            m = m_of(jj)
            s = s_ref[jj]
            p += [jnp.exp2((s[:, i * LANES:(i + 1) * LANES] - m).astype(BF16)) for i in range(ts // LANES)]
        v_ext = jnp.concatenate([v_ref[0, pl.ds(ks, tk), :], ones], axis=1)
        return acc + jnp.dot(jnp.concatenate(p, axis=1), v_ext, preferred_element_type=F32)

    acc = lax.fori_loop(0, n_steps, step, jnp.zeros((tq, dv + LANES), F32))
    return acc[:, :dv] / acc[:, dv:]


def _row_max(mrun):
    return jnp.broadcast_to(jnp.max(mrun, axis=-1, keepdims=True), mrun.shape)


def _flash_body(n_ref, q_ref, k_ref, v_ref, o_ref, s_ref, *, tk, stack):
    tq = q_ref.shape[1]
    q_w = q_ref.shape[2] // stack
    q = jnp.concatenate([q_ref[0, :, c * q_w:(c + 1) * q_w] for c in range(stack)], axis=0)
    m = _row_max(_score_pass(q, k_ref, s_ref, tk, n_ref[0]))
    o = _value_pass(v_ref, s_ref, tk, n_ref[0], lambda jj: m).astype(o_ref.dtype)
    o_ref[0] = jnp.concatenate([o[c * tq:(c + 1) * tq] for c in range(stack)], axis=1)


def _flash(q, k, v, *, kv_heads, stack, q_w, k_col0, v_col0, tq, ts, tk):
    B, S, _ = q.shape
    dv = LANES
    return pl.pallas_call(
        functools.partial(_flash_body, tk=tk, stack=stack),
        grid=(B, kv_heads, S // tq),
        in_specs=[pl.BlockSpec(memory_space=pltpu.SMEM),
                  pl.BlockSpec((1, tq, stack * q_w), lambda b, h, i: (b, i, h)),
                  pl.BlockSpec((1, S, q_w), lambda b, h, i: (b, 0, k_col0 + h)),
                  pl.BlockSpec((1, S, dv), lambda b, h, i: (b, 0, v_col0 + h))],
        out_specs=pl.BlockSpec((1, tq, stack * dv), lambda b, h, i: (b, i, h)),
        out_shape=jax.ShapeDtypeStruct((B, S, kv_heads * stack * dv), BF16),
        scratch_shapes=[pltpu.VMEM((S // ts, stack * tq, ts), F32)],
        compiler_params=_cparams("parallel", "parallel", "arbitrary"),
        name="flash_attn",
    )(jnp.full((1,), S // tk, jnp.int32), q, k, v)


def _diff_body(n_ref, lam_ref, cb_ref, q_ref, k_ref, v_ref, diag_ref, corner_ref, o_ref, s_ref, *, tk, out_scale):
    h, qi = pl.program_id(1), pl.program_id(2)
    n_sub = s_ref.shape[0]
    tq = q_ref.shape[1]
    nd = REL_MAX_DIST
    qf = q_ref[0].astype(F32) * (B_QK ** -0.5 * LOG2E)
    lane = lax.broadcasted_iota(jnp.int32, qf.shape, 1)
    c_lo, c_hi = cb_ref[0, h], cb_ref[1, h]
    side_shift = lambda jj: jnp.where(jj < qi, c_lo, jnp.where(jj > qi, c_hi, 0.0))
    tile_max = lambda s: _lane_fold(jnp.maximum, s[:, :LANES], s[:, LANES:])

    q = jnp.concatenate([jnp.where(lane < B_QK, qf, 0.0), jnp.where(lane >= B_QK, qf, 0.0)], axis=0).astype(BF16)
    row = lax.broadcasted_iota(jnp.int32, (2 * tq, LANES), 0) % tq
    pen_lo = jnp.where(row < nd, -jnp.inf, 0.0)
    pen_hi = jnp.where(row >= tq - nd, -jnp.inf, 0.0)
    zero = jnp.zeros_like(pen_lo)

    def sub_max(jj, blocks):
        blocks = list(blocks)
        blocks[-1] = blocks[-1] + jnp.where(jj == qi - 1, pen_lo, zero)
        blocks[0] = blocks[0] + jnp.where(jj == qi + 1, pen_hi, zero)
        shift = jnp.where(jj < qi, c_lo, jnp.where(jj > qi, c_hi, -jnp.inf))
        return functools.reduce(jnp.maximum, blocks) + shift

    mrun = _score_pass(q, k_ref, s_ref, tk, n_ref[0], sub_max=sub_max)

    diag = diag_ref[0]
    s = s_ref[qi] + jnp.concatenate([diag, diag], axis=0)
    s_ref[qi] = s
    mrun = jnp.maximum(mrun, tile_max(s))

    def neighbour(jj, rows0, cols0, corner, c):
        def fix(mrun):
            for r in (rows0, tq + rows0):
                fixed = s_ref[jj, r:r + nd, cols0:cols0 + nd] + corner
                s_ref[jj, r:r + nd, cols0:cols0 + nd] = fixed
                pieces = [mrun[:r], jnp.maximum(mrun[r:r + nd], fixed + c), mrun[r + nd:]]
                mrun = jnp.concatenate([p for p in pieces if p.shape[0]], axis=0)
            return mrun
        return fix

    mrun = lax.cond(qi >= 1, neighbour(jnp.maximum(qi - 1, 0), 0, tq - nd, corner_ref[0, 0], c_lo),
                    lambda m_: m_, mrun)
    mrun = lax.cond(qi + 1 < n_sub, neighbour(jnp.minimum(qi + 1, n_sub - 1), tq - nd, 0, corner_ref[0, 1], c_hi),
                    lambda m_: m_, mrun)
    m = _row_max(mrun)
    o = _value_pass(v_ref, s_ref, tk, n_ref[0], lambda jj: m - side_shift(jj))
    o = o[:tq] - lam_ref[0] * o[tq:]
    o_ref[0] = (_rms(o) * out_scale).astype(o_ref.dtype)


def _rel_bucket(rel):
    half = REL_BUCKETS // 2
    max_exact = half // 2
    ret = (rel > 0).astype(jnp.int32) * half
    n = jnp.abs(rel)
    large = max_exact + (jnp.log(jnp.maximum(n, 1).astype(F32) / max_exact)
                         / math.log(REL_MAX_DIST / max_exact) * (half - max_exact)).astype(jnp.int32)
    large = jnp.minimum(large, half - 1)
    return ret + jnp.where(n < max_exact, n, large)


def _diff_bias_tables(rel_bias, tq):
    nd = REL_MAX_DIST
    rb = rel_bias.astype(F32) * LOG2E
    far = jnp.stack([rb[REL_BUCKETS // 2 - 1], rb[REL_BUCKETS - 1]])

    def table(rel):
        bucket = _rel_bucket(rel)
        out = jnp.zeros((B_HEADS,) + rel.shape, F32)
        for b in range(REL_BUCKETS):
            out = jnp.where(bucket[None] == b, rb[b].reshape((B_HEADS,) + (1,) * rel.ndim), out)
        return out

    a, e = jnp.arange(tq), jnp.arange(nd)
    diag = table(a[None, :] - a[:, None])
    lo = table((tq - nd + e[None, :]) - tq - e[:, None]) - far[0][:, None, None]
    hi = table(e[None, :] + tq - (tq - nd + e[:, None])) - far[1][:, None, None]
    return diag, jnp.stack([lo, hi], axis=1), far


def _diff_attn(zb, lam, diag, corner, far, *, tq, tk, out_scale):
    B, S, _ = zb.shape
    nd = REL_MAX_DIST
    assert tq >= nd, "far key chunks must lie beyond the last distinct relative bucket"
    assert nd == LANES, "the neighbours' near corner is handled as one lane block"
    H = B_HEADS
    smem = pl.BlockSpec(memory_space=pltpu.SMEM)
    return pl.pallas_call(
        functools.partial(_diff_body, tk=tk, out_scale=out_scale),
        grid=(B, H, S // tq),
        in_specs=[smem, smem, smem,
                  pl.BlockSpec((1, tq, B_V), lambda b, h, i: (b, i, h)),
                  pl.BlockSpec((1, S, B_V), lambda b, h, i: (b, 0, H + h)),
                  pl.BlockSpec((1, S, B_V), lambda b, h, i: (b, 0, 2 * H + h)),
                  pl.BlockSpec((1, tq, tq), lambda b, h, i: (h, 0, 0)),
                  pl.BlockSpec((1, 2, nd, nd), lambda b, h, i: (h, 0, 0, 0))],
        out_specs=pl.BlockSpec((1, tq, B_V), lambda b, h, i: (b, i, h)),
        out_shape=jax.ShapeDtypeStruct((B, S, H * B_V), BF16),
        scratch_shapes=[pltpu.VMEM((S // tq, 2 * tq, tq), F32)],
        compiler_params=_cparams("parallel", "parallel", "arbitrary"),
        name="diff_attn",
    )(jnp.full((1,), S // tk, jnp.int32), lam, far, zb, zb, zb, diag, corner)


def _pool_body(u_ref, w_ref, sc_ref, o_ref, pad_ref, *, chunk):
    g = pl.program_id(1)
    S = u_ref.shape[1]
    pad_ref[0:HALO, :] = jnp.zeros((HALO, LANES), F32)
    pad_ref[HALO + S:, :] = jnp.zeros((HALO, LANES), F32)
    pad_ref[HALO:HALO + S, :] = u_ref[0].astype(F32)
    w_mat, sc = w_ref[0], sc_ref[...]

    def pooled(win):
        def body(c, _):
            r0 = pl.multiple_of(c * chunk, chunk)
            tot = pad_ref[pl.ds(r0 + HALO - win // 2, chunk), :]
            for j in range(1 - win // 2, win // 2):
                tot = tot + pad_ref[pl.ds(r0 + HALO + j, chunk), :]
            t = r0 + lax.broadcasted_iota(jnp.int32, (chunk, 1), 0)
            cnt = jnp.clip(t - win // 2 + win, 0, S) - jnp.clip(t - win // 2, 0, S)
            d = tot / cnt.astype(F32) - pad_ref[pl.ds(r0 + HALO, chunk), :]
            y = jnp.dot(d.astype(BF16), w_mat, preferred_element_type=F32) * sc
            o_ref[0, pl.ds(r0, chunk), :] = y.astype(o_ref.dtype)
            return 0
        lax.fori_loop(0, S // chunk, body, 0)

    for gi, win in enumerate(POOL_WINDOWS):
        pl.when(g == gi)(functools.partial(pooled, win))


def _pool(zc, w_pool, pool_scale, chunk):
    B, S, _ = zc.shape
    blk = pl.BlockSpec((1, S, LANES), lambda b, g: (b, 0, g))
    return pl.pallas_call(
        functools.partial(_pool_body, chunk=chunk),
        grid=(B, C_GROUPS),
        in_specs=[blk, pl.BlockSpec((1, C_GROUP_W, C_GROUP_W), lambda b, g: (g, 0, 0)),
                  pl.BlockSpec((1, LANES), lambda b, g: (0, g))],
        out_specs=blk,
        out_shape=jax.ShapeDtypeStruct((B, S, BRANCH_W), BF16),
        scratch_shapes=[pltpu.VMEM((S + 2 * HALO, LANES), F32)],
        compiler_params=_cparams("parallel", "arbitrary"),
        name="pool_mixer",
    )(zc, w_pool, pool_scale)


def _gated_merge_body(h_ref, oa_ref, ob_ref, oc_ref, od_ref, wg_ref, bg_ref, wl_ref, m_ref):
    h = h_ref[...]
    merged = None
    for b, o_ref in enumerate((oa_ref, ob_ref, oc_ref, od_ref)):
        gate = jax.nn.sigmoid(jnp.dot(h, wg_ref[b], preferred_element_type=F32) + bg_ref[b:b + 1, :])
        term = gate * jnp.dot(o_ref[...], wl_ref[b], preferred_element_type=F32)
        merged = term if merged is None else merged + term
    m_ref[...] = merged.astype(BF16)


def _gated_merge(h, outs, wg, bg, wl, tm, tn):
    T, D = h.shape
    row = lambda w_: pl.BlockSpec((tm, w_), lambda i, j: (i, 0))
    return pl.pallas_call(
        _gated_merge_body,
        grid=(T // tm, D // tn),
        in_specs=[row(D)] + [row(BRANCH_W)] * N_BRANCH + [
            pl.BlockSpec((N_BRANCH, D, tn), lambda i, j: (0, 0, j)),
            pl.BlockSpec((N_BRANCH, tn), lambda i, j: (0, j)),
            pl.BlockSpec((N_BRANCH, BRANCH_W, tn), lambda i, j: (0, 0, j))],
        out_specs=pl.BlockSpec((tm, tn), lambda i, j: (i, j)),
        out_shape=jax.ShapeDtypeStruct((T, D), BF16),
        compiler_params=_cparams("parallel", "arbitrary"),
        name="gated_merge",
    )(h, *outs, wg, bg, wl)


ROUTE_ROWS = 8


def _out_proj_body(m_ref, x_ref, wo_ref, g2_ref, wr_ref, rb_ref, xn_ref, h2_ref, rt_ref):
    xn = x_ref[...] + jnp.dot(m_ref[...], wo_ref[...], preferred_element_type=F32)
    xn_ref[...] = xn
    h2 = _rms(xn, g2_ref[...])
    h2_ref[...] = h2.astype(BF16)
    h_hi = h2.astype(BF16)
    h_lo = (h2 - h_hi.astype(F32)).astype(BF16)
    hi = jnp.dot(h_hi, wr_ref[...], preferred_element_type=F32)
    logits = hi[:, :ROUTER_W] + hi[:, ROUTER_W:] + jnp.dot(h_lo, wr_ref[:, :ROUTER_W], preferred_element_type=F32)
    rt_ref[...] = _route_tile(logits + rb_ref[...])


def _out_proj(merged, x, wo, g2, wr, route_bias, tm):
    T, D = x.shape
    row = pl.BlockSpec((tm, D), lambda i: (i, 0))
    return pl.pallas_call(
        _out_proj_body,
        grid=(T // tm,),
        in_specs=[row, row, _resident((D, D)), _resident((1, D)), _resident((D, 2 * ROUTER_W)),
                  _resident((1, ROUTER_W))],
        out_specs=[row, row, pl.BlockSpec((ROUTE_ROWS, tm), lambda i: (0, i))],
        out_shape=[jax.ShapeDtypeStruct((T, D), F32), jax.ShapeDtypeStruct((T, D), BF16),
                   jax.ShapeDtypeStruct((ROUTE_ROWS, T), F32)],
        compiler_params=_cparams("parallel"),
        name="out_proj_route",
    )(merged, x, wo, g2, wr, route_bias)


def _route_tile(lg):
    lane = lax.broadcasted_iota(jnp.int32, lg.shape, 1).astype(F32)
    neg = -jnp.inf
    row_max = lambda v: jnp.max(v, axis=-1, keepdims=True)
    first_at = lambda v, m: jnp.min(jnp.where(v == m, lane, float(LANES)), axis=-1, keepdims=True)
    g_lg = jnp.where(lane < N_GROUPS, lg, neg)
    g_max = row_max(g_lg)
    g_top = first_at(g_lg, g_max)
    pg_top = 1.0 / jnp.sum(jnp.exp(g_lg - g_max), axis=-1, keepdims=True)
    e_lo = N_GROUPS + EXP_PER_GROUP * g_top
    e_lg = jnp.where(jnp.logical_and(lane >= e_lo, lane < e_lo + EXP_PER_GROUP), lg, neg)
    picks, maxes = [], []
    for _ in range(TOP_K):
        m = row_max(e_lg)
        i = first_at(e_lg, m)
        picks.append(i)
        maxes.append(m)
        e_lg = jnp.where(lane == i, neg, e_lg)
    w = [jnp.exp(m - maxes[0]) for m in maxes]
    w_sum = functools.reduce(lambda a, b: a + b, w)
    cols = [i - N_GROUPS for i in picks] + [pg_top * wk / w_sum for wk in w]
    tile = jnp.zeros(lg.shape, F32)
    for c, v in enumerate(cols):
        tile = jnp.where(lane == c, v, tile)
    return tile.T[:ROUTE_ROWS]


def _dispatch(routed, tb):
    T = routed.shape[1]
    A = T * TOP_K
    flat_e = routed[:TOP_K].astype(jnp.int32).T.reshape(A)
    flat_w = routed[TOP_K:2 * TOP_K].T.reshape(A)
    iota = jnp.arange(A, dtype=jnp.int32)
    se, order, sw = lax.sort((flat_e, iota, flat_w), num_keys=1, is_stable=True)
    experts = jnp.arange(N_EXPERTS, dtype=jnp.int32)
    counts = jnp.sum((flat_e[:, None] == experts[None, :]).astype(jnp.int32), axis=0)
    start = jnp.cumsum(counts) - counts
    padded = (counts + tb - 1) // tb * tb
    pend = jnp.cumsum(padded)
    pstart = pend - padded
    n_blocks = -(-A // tb) + N_EXPERTS
    P = n_blocks * tb
    blk_first = jnp.arange(n_blocks, dtype=jnp.int32) * tb
    blk_exp = jnp.minimum(jnp.sum((pend[None, :] <= blk_first[:, None]).astype(jnp.int32), axis=1), N_EXPERTS - 1)
    n_valid = (pend[-1] // tb).astype(jnp.int32).reshape(1)
    slot = jnp.arange(P, dtype=jnp.int32)
    before = (pend[None, :] <= slot[:, None]).astype(jnp.int32)
    shift = jnp.sum(before * (padded - counts)[None, :], axis=1)
    in_pad = jnp.any(jnp.logical_and(slot[:, None] >= (pstart + counts)[None, :], slot[:, None] < pend[None, :]),
                     axis=1)
    valid = jnp.logical_and(jnp.logical_not(in_pad), slot < pend[-1])
    src = jnp.where(valid, slot - shift, slot % A)
    slot_tok = order[src] // TOP_K
    slot_w = jnp.where(valid, sw[src], 0.0)
    dest = iota + jnp.sum((se[:, None] == experts[None, :]).astype(jnp.int32) * (pstart - start)[None, :], axis=1)
    _, slot_of = lax.sort((order, dest), num_keys=1)
    return slot_tok, slot_w, blk_exp, n_valid, slot_of.reshape(T, TOP_K).T


def _expert_body(be_ref, nv_ref, x_ref, sw_ref, wg_ref, wu_ref, wd_ref, o_ref, wg_s, wu_s, wd_s):
    i = pl.program_id(0)

    @pl.when(jnp.logical_or(i == 0, be_ref[i] != be_ref[jnp.maximum(i - 1, 0)]))
    def _():
        wg_s[...] = wg_ref[0, 0].astype(BF16)
        wu_s[...] = wu_ref[0, 0].astype(BF16)
        wd_s[...] = wd_ref[0, 0].astype(BF16)

    @pl.when(i < nv_ref[0])
    def _():
        x = x_ref[...]
        g = jnp.dot(x, wg_s[...], preferred_element_type=F32)
        u = jnp.dot(x, wu_s[...], preferred_element_type=F32)
        a = (g * jax.nn.sigmoid(g) * u).astype(BF16)
        y = jnp.dot(a, wd_s[...], preferred_element_type=F32)
        sw = sw_ref[...]
        o_ref[...] = jnp.concatenate([y[:, c * LANES:(c + 1) * LANES] * sw for c in range(y.shape[1] // LANES)],
                                     axis=1).astype(o_ref.dtype)

    @pl.when(i >= nv_ref[0])
    def _():
        o_ref[...] = jnp.zeros(o_ref.shape, o_ref.dtype)


def _expert_ffn(xs, slot_w, blk_exp, n_valid, wg, wu, wd, layer, tb):
    P, D = xs.shape
    grid_spec = pltpu.PrefetchScalarGridSpec(
        num_scalar_prefetch=2,
        grid=(P // tb,),
        in_specs=[pl.BlockSpec((tb, D), lambda i, be, nv: (i, 0)),
                  pl.BlockSpec((tb, LANES), lambda i, be, nv: (i, 0)),
                  pl.BlockSpec((1, 1, D, D_EXPERT), lambda i, be, nv: (layer, be[i], 0, 0)),
                  pl.BlockSpec((1, 1, D, D_EXPERT), lambda i, be, nv: (layer, be[i], 0, 0)),
                  pl.BlockSpec((1, 1, D_EXPERT, D), lambda i, be, nv: (layer, be[i], 0, 0))],
        out_specs=pl.BlockSpec((tb, D), lambda i, be, nv: (i, 0)),
        scratch_shapes=[pltpu.VMEM((D, D_EXPERT), BF16), pltpu.VMEM((D, D_EXPERT), BF16),
                        pltpu.VMEM((D_EXPERT, D), BF16)],
    )
    return pl.pallas_call(
        _expert_body,
        grid_spec=grid_spec,
        out_shape=jax.ShapeDtypeStruct((P, D), BF16),
        compiler_params=_cparams("arbitrary"),
        name="expert_ffn",
    )(blk_exp, n_valid, xs, jnp.broadcast_to(slot_w[:, None], (P, LANES)), wg, wu, wd)


def _final_norm_body(*refs):
    x_ref, y_refs, (g_ref, o_ref) = refs[0], refs[1:-2], refs[-2:]
    o_ref[...] = _rms(_residual_sum(x_ref, y_refs), g_ref[...])


def _final_norm(x, adds, g, row0, rows, tm):
    D = x.shape[1]
    first = row0 // tm
    src = pl.BlockSpec((tm, D), lambda i: (first + i, 0))
    return pl.pallas_call(
        _final_norm_body, grid=(rows // tm,),
        in_specs=[src] * (1 + len(adds)) + [_resident((1, D))],
        out_specs=pl.BlockSpec((tm, D), lambda i: (i, 0)),
        out_shape=jax.ShapeDtypeStruct((rows, D), F32), compiler_params=_cparams("parallel"), name="final_norm",
    )(x, *adds, g)


def _rope_tables(seq):
    def angles(pos, dim):
        inv = 1.0 / (ROPE_THETA ** (jnp.arange(0, dim, 2, dtype=F32) / dim))
        ang = pos.astype(F32)[:, None] * inv[None, :]
        return jnp.cos(ang), jnp.sin(ang)
    pos = jnp.arange(seq)
    c1, s1 = angles(pos, A_ROPE)
    cr, sr = angles(pos // GRID_W, D_HEAD // 2)
    cc, sc = angles(pos % GRID_W, D_HEAD // 2)
    pad1, pad0 = jnp.ones((seq, ROPE_PAD), F32), jnp.zeros((seq, ROPE_PAD), F32)
    return (jnp.concatenate([c1, c1, pad1], 1), jnp.concatenate([-s1, s1, pad0], 1),
            jnp.concatenate([cr, cr, cc, cc], 1), jnp.concatenate([-sr, sr, -sc, sc], 1))


def _layer_weights(l, w_in, w_uq, w_ukv, w_route_group, w_route_expert):
    D = D_MODEL
    k_r_end = A_Q_LORA + A_KV_LORA + A_ROPE
    win = jnp.concatenate([w_in[l][:, :k_r_end], jnp.zeros((D, ROPE_PAD), F32), w_in[l][:, k_r_end:]], 1)
    uq = w_uq[l].reshape(A_Q_LORA, A_HEADS, A_NOPE + A_ROPE)
    uq_rope = jnp.pad(uq[:, :, A_NOPE:], ((0, 0), (0, 0), (0, ROPE_PAD)))
    wuq = jnp.concatenate([uq[:, :, :A_NOPE].reshape(A_Q_LORA, -1), uq_rope.reshape(A_Q_LORA, -1)], 1)
    ukv = w_ukv[l].reshape(A_KV_LORA, A_HEADS, A_NOPE + A_V)
    wukv = jnp.concatenate([ukv[:, :, :A_NOPE].reshape(A_KV_LORA, -1), ukv[:, :, A_NOPE:].reshape(A_KV_LORA, -1)], 1)
    wr = jnp.concatenate([w_route_group[l], w_route_expert[l],
                          jnp.zeros((D, ROUTER_W - N_GROUPS - N_EXPERTS), F32)], 1)
    wr_hi = wr.astype(BF16)
    wr_lo = (wr - wr_hi.astype(F32)).astype(BF16)
    return win.astype(BF16), wuq.astype(BF16), wukv.astype(BF16), jnp.concatenate([wr_hi, wr_lo], 1)


def kernel(x_prompt, x_sample, norm1_g, w_in, q_norm_g, kv_norm_g, w_uq, w_ukv, lam_q1, lam_k1, lam_q2, lam_k2,
           rel_bias, w_pool, pool_scale, qk_norm_q, qk_norm_k, w_lift, w_gate, b_gate, w_out, norm2_g,
           w_route_group, b_route_group, w_route_expert, b_route_expert, w_exp_gate, w_exp_up, w_exp_down,
           final_g):
    assert x_prompt.shape[1:] == x_sample.shape[1:], "both request groups must share (seq, d_model)"
    n_prompt, S, D = x_prompt.shape
    B = n_prompt + x_sample.shape[0]
    T = B * S
    t = _tiles(S, T)
    x = None
    depth = w_in.shape[0]
    tabs = _rope_tables(S)
    diag, corner, far = _diff_bias_tables(rel_bias, t["tq"])
    row2 = lambda v: v.reshape(1, -1).astype(F32)

    moe_out = ()
    for l in range(depth):
        lam_init = 0.8 - 0.6 * math.exp(-0.3 * l)
        lam = (jnp.exp(jnp.sum(lam_q1[l] * lam_k1[l])) - jnp.exp(jnp.sum(lam_q2[l] * lam_k2[l])) + lam_init)
        win, wuq, wukv, wr = _layer_weights(l, w_in, w_uq, w_ukv, w_route_group, w_route_expert)

        xs = (x_prompt.reshape(-1, D), x_sample.reshape(-1, D)) if l == 0 else (x,)
        prep = (row2(q_norm_g[l]), row2(kv_norm_g[l]), wuq, wukv, row2(qk_norm_q[l]), row2(qk_norm_k[l])) + tabs
        x, h, zb, zc, qa, ka, va, qd, kd, vd = _norm_inproj(xs, moe_out, row2(norm1_g[l]), win, prep, S, t["tm_in"])
        seq3 = lambda a: a.reshape(B, S, a.shape[-1])
        rep = D_HEADS // D_KV_HEADS
        o_a = _flash(seq3(qa), seq3(ka), seq3(va), kv_heads=A_HEADS, stack=1, q_w=A_QK_W, k_col0=0, v_col0=0,
                     tq=t["rows"], ts=t["tq"], tk=t["tk"])
        o_b = _diff_attn(seq3(zb), lam.reshape(1).astype(F32), diag, corner, far, tq=t["tq"], tk=t["tk"],
                         out_scale=1.0 - lam_init)
        o_c = _pool(seq3(zc), w_pool[l].astype(BF16), row2(pool_scale[l]), t["pool_chunk"])
        o_d = _flash(seq3(qd), seq3(kd), seq3(vd), kv_heads=D_KV_HEADS, stack=rep, q_w=D_HEAD, k_col0=0,
                     v_col0=0, tq=t["rows"] // rep, ts=t["tq"], tk=t["tk"])
        outs = [o.reshape(T, BRANCH_W) for o in (o_a, o_b, o_c, o_d)]
        merged = _gated_merge(h, outs, w_gate[l].astype(BF16), b_gate[l].astype(F32), w_lift[l].astype(BF16),
                              t["tm_merge"], t["tn"])
        route_bias = jnp.concatenate([b_route_group[l], b_route_expert[l],
                                      jnp.zeros((ROUTER_W - N_GROUPS - N_EXPERTS,), F32)]).reshape(1, ROUTER_W)
        x, h2, routed = _out_proj(merged, x, w_out[l].astype(BF16), row2(norm2_g[l]), wr, route_bias.astype(F32),
                                  t["tm"])
        slot_tok, slot_w, blk_exp, n_valid, slot_of = _dispatch(routed, t["tb"])
        ys = _expert_ffn(h2[slot_tok], slot_w, blk_exp, n_valid, w_exp_gate, w_exp_up, w_exp_down, l, t["tb"])
        moe_out = tuple(ys[slot_of[k]] for k in range(TOP_K))

    t_prompt = n_prompt * S
    y_prompt = _final_norm(x, moe_out, row2(final_g), 0, t_prompt, t["tm"]).reshape(n_prompt, S, D)
    y_sample = _final_norm(x, moe_out, row2(final_g), t_prompt, T - t_prompt, t["tm"]).reshape(B - n_prompt, S, D)
    return y_prompt, y_sample
```

```python
import functools
import math

import jax
import jax.numpy as jnp
from jax import lax
from jax.experimental import pallas as pl
from jax.experimental.pallas import tpu as pltpu

F32 = jnp.float32
BF16 = jnp.bfloat16
LOG2E = 1.4426950408889634

D_MODEL = 2048
GRID_W = 64
BRANCH_W = 512
N_BRANCH = 4
ROPE_THETA = 10000.0
EPS = 1e-6
A_HEADS, A_Q_LORA, A_KV_LORA, A_NOPE, A_ROPE, A_V = 4, 384, 128, 128, 64, 128
B_HEADS, B_QK, B_V = 4, 64, 128
REL_BUCKETS, REL_MAX_DIST = 32, 128
POOL_WINDOWS = (2, 4, 8, 16)
C_GROUPS, C_GROUP_W = 4, 128
D_HEADS, D_KV_HEADS, D_HEAD = 4, 2, 128
N_GROUPS, EXP_PER_GROUP, N_EXPERTS, TOP_K, D_EXPERT = 4, 8, 32, 2, 512

LANES = 128
ROPE_PAD = LANES - A_ROPE
ZA_W = A_Q_LORA + A_KV_LORA + LANES
ZB_W = 3 * B_HEADS * B_V
ZC_W = BRANCH_W
ZD_W = (D_HEADS + 2 * D_KV_HEADS) * D_HEAD
A_QK_W = 2 * LANES
ROUTER_W = LANES
HALO = 16
VMEM_LIMIT = 56 * 2 ** 20


def _tiles(seq, tokens):
    def pick(n, pref):
        t = min(pref, n)
        while n % t:
            t //= 2
        return t
    return dict(
        tm=pick(seq, 512),
        tm_in=pick(seq, 256),
        tq=pick(seq, 512),
        rows=pick(seq, 1024),
        tk=pick(seq, 8192),
        tm_merge=pick(tokens, 1024),
        tn=512,
        tb=512,
        pool_chunk=pick(seq, 1024),
    )


def _cparams(*sem):
    return pltpu.CompilerParams(dimension_semantics=sem, vmem_limit_bytes=VMEM_LIMIT)


def _resident(shape):
    zeros = (0,) * len(shape)
    return pl.BlockSpec(shape, lambda *_: zeros, pipeline_mode=pl.Buffered(1))


def _rms(x, g=None):
    y = x * lax.rsqrt(jnp.mean(x * x, axis=-1, keepdims=True) + EPS)
    return y if g is None else y * g


def _rope_lanes(x, c, s):
    lane = lax.broadcasted_iota(jnp.int32, x.shape, 1)
    first_half = (lane & 32) == 0
    partner = jnp.where(first_half, pltpu.roll(x, 96, 1), pltpu.roll(x, 32, 1))
    return x * c + partner * s


def _residual_sum(x_ref, y_refs):
    x = x_ref[...]
    for y_ref in y_refs:
        x = x + y_ref[...].astype(F32)
    return x


N_PREP_PARAMS = 10
PREP_OUT_W = (A_HEADS * A_QK_W, A_HEADS * A_QK_W, A_HEADS * A_V, D_HEADS * D_HEAD, D_KV_HEADS * D_HEAD,
              D_KV_HEADS * D_HEAD)


def _norm_inproj_body(*refs, n_x, n_add, first_rows):
    x_refs, y_refs = refs[:n_x], refs[n_x:n_x + n_add]
    g_ref, w_ref = refs[n_x + n_add:n_x + n_add + 2]
    prep_refs = refs[n_x + n_add + 2:n_x + n_add + 2 + N_PREP_PARAMS]
    outs = refs[n_x + n_add + 2 + N_PREP_PARAMS:]
    if n_x == 2:
        x = jnp.where(pl.program_id(0) < first_rows, x_refs[0][...], x_refs[1][...])
    else:
        x = _residual_sum(x_refs[0], y_refs)
    if n_x == 2 or n_add:
        outs[0][...] = x
        outs = outs[1:]
    h_ref, zb_ref, zc_ref = outs[:3]
    h = _rms(x, g_ref[...]).astype(BF16)
    h_ref[...] = h
    proj = lambda lo, w_: jnp.dot(h, w_ref[:, lo:lo + w_], preferred_element_type=F32)
    zb_ref[...] = proj(ZA_W, ZB_W).astype(BF16)
    zc_ref[...] = proj(ZA_W + ZB_W, ZC_W).astype(BF16)
    _mixer_prep(proj(0, ZA_W), proj(ZA_W + ZB_W + ZC_W, ZD_W), prep_refs, outs[3:])


def _norm_inproj(xs, adds, g, w, prep, seq, tm):
    D = xs[0].shape[1]
    T = sum(x.shape[0] for x in xs)
    per_seq = seq // tm
    row = lambda w_: pl.BlockSpec((tm, w_), lambda i: (i, 0))
    if len(xs) == 2:
        na = xs[0].shape[0] // tm
        x_specs = [pl.BlockSpec((tm, D), lambda i: (jnp.minimum(i, na - 1), 0)),
                   pl.BlockSpec((tm, D), lambda i: (jnp.maximum(i - na, 0), 0))]
    else:
        na, x_specs = None, [row(D)]
    new_x = len(xs) == 2 or bool(adds)
    sum_spec, sum_shape = ([row(D)], [jax.ShapeDtypeStruct((T, D), F32)]) if new_x else ([], [])
    tab = pl.BlockSpec((tm, LANES), lambda i: (i % per_seq, 0))
    out_w = (D, ZB_W, ZC_W) + PREP_OUT_W
    outs = pl.pallas_call(
        functools.partial(_norm_inproj_body, n_x=len(xs), n_add=len(adds), first_rows=na),
        grid=(T // tm,),
        in_specs=x_specs + [row(D)] * len(adds) + [_resident((1, D)), _resident(w.shape)]
        + [_resident(p.shape) for p in prep[:N_PREP_PARAMS - 4]] + [tab] * 4,
        out_specs=sum_spec + [row(w_) for w_ in out_w],
        out_shape=sum_shape + [jax.ShapeDtypeStruct((T, w_), BF16) for w_ in out_w],
        compiler_params=_cparams("parallel"),
        name="norm_inproj",
    )(*xs, *adds, g, w, *prep)
    return list(outs) if new_x else [xs[0]] + list(outs)


def _mixer_prep(za, zd, prep_refs, out_refs):
    gq_ref, gkv_ref, wuq_ref, wukv_ref, gdq_ref, gdk_ref, ca_ref, sa_ref, cd_ref, sd_ref = prep_refs
    qa_ref, ka_ref, va_ref, qd_ref, kd_ref, vd_ref = out_refs
    a_scale = (A_NOPE + A_ROPE) ** -0.5 * LOG2E
    d_scale = D_HEAD ** -0.5 * LOG2E
    ca, sa, cd, sd = ca_ref[...], sa_ref[...], cd_ref[...], sd_ref[...]
    cq = _rms(za[:, :A_Q_LORA], gq_ref[...]).astype(BF16)
    q = jnp.dot(cq, wuq_ref[...], preferred_element_type=F32)
    ckv = _rms(za[:, A_Q_LORA:A_Q_LORA + A_KV_LORA], gkv_ref[...]).astype(BF16)
    kv = jnp.dot(ckv, wukv_ref[...], preferred_element_type=F32)
    k_rope = _rope_lanes(za[:, A_Q_LORA + A_KV_LORA:], ca, sa).astype(BF16)
    nope_w = A_HEADS * A_NOPE
    for h in range(A_HEADS):
        lo = h * A_QK_W
        qa_ref[:, lo:lo + LANES] = (q[:, h * LANES:(h + 1) * LANES] * a_scale).astype(BF16)
        q_rope = _rope_lanes(q[:, nope_w + h * LANES:nope_w + (h + 1) * LANES], ca, sa)
        qa_ref[:, lo + LANES:lo + 2 * LANES] = (q_rope * a_scale).astype(BF16)
        ka_ref[:, lo:lo + LANES] = kv[:, h * LANES:(h + 1) * LANES].astype(BF16)
        ka_ref[:, lo + LANES:lo + 2 * LANES] = k_rope
    va_ref[...] = kv[:, nope_w:].astype(BF16)
    for h in range(D_HEADS):
        xh = _rms(zd[:, h * D_HEAD:(h + 1) * D_HEAD], gdq_ref[...])
        qd_ref[:, h * D_HEAD:(h + 1) * D_HEAD] = (_rope_lanes(xh, cd, sd) * d_scale).astype(BF16)
    k_off = D_HEADS * D_HEAD
    for h in range(D_KV_HEADS):
        xh = _rms(zd[:, k_off + h * D_HEAD:k_off + (h + 1) * D_HEAD], gdk_ref[...])
        kd_ref[:, h * D_HEAD:(h + 1) * D_HEAD] = _rope_lanes(xh, cd, sd).astype(BF16)
    vd_ref[...] = zd[:, k_off + D_KV_HEADS * D_HEAD:].astype(BF16)


def _qk(q, k):
    return lax.dot_general(q, k, (((1,), (1,)), ((), ())), preferred_element_type=F32)


def _lane_fold(fn, acc, x):
    for c in range(x.shape[1] // LANES):
        acc = fn(acc, x[:, c * LANES:(c + 1) * LANES])
    return acc


def _lane_blocks(x):
    return [x[:, i * LANES:(i + 1) * LANES] for i in range(x.shape[1] // LANES)]


def _score_pass(q, k_ref, s_ref, m_ref, tk, n_steps, sub_max=None):
    tq, ts = s_ref.shape[1:]
    sub = tk // ts
    m_ref[...] = jnp.full((tq, LANES), -jnp.inf, F32)

    def step(j, _):
        ks = pl.multiple_of(j * tk, tk)
        s = _qk(q, k_ref[0, pl.ds(ks, tk), :])
        mrun = None
        for c in range(sub):
            jj = j * sub + c
            sc = s[:, c * ts:(c + 1) * ts]
            s_ref[jj] = sc
            blocks = _lane_blocks(sc)
            m_sub = functools.reduce(jnp.maximum, blocks) if sub_max is None else sub_max(jj, blocks)
            mrun = m_sub if mrun is None else jnp.maximum(mrun, m_sub)
        m_ref[...] = jnp.maximum(m_ref[...], mrun)
        return 0

    lax.fori_loop(0, n_steps, step, 0)
    return m_ref[...]


def _value_pass(v_ref, s_ref, a_ref, tk, n_steps, m_of):
    tq, ts = s_ref.shape[1:]
    sub = tk // ts
    dv = v_ref.shape[2]
    ones = jnp.ones((tk, LANES), BF16)
    a_ref[...] = jnp.zeros(a_ref.shape, F32)

    def step(j, _):
        ks = pl.multiple_of(j * tk, tk)
        p = []
        for c in range(sub):
            jj = j * sub + c
            m = m_of(jj)
            s = s_ref[jj]
            p += [jnp.exp2((s[:, i * LANES:(i + 1) * LANES] - m).astype(BF16)) for i in range(ts // LANES)]
        v_ext = jnp.concatenate([v_ref[0, pl.ds(ks, tk), :], ones], axis=1)
        a_ref[...] += jnp.dot(jnp.concatenate(p, axis=1), v_ext, preferred_element_type=F32)
        return 0

    lax.fori_loop(0, n_steps, step, 0)
    return a_ref[:, :dv] / a_ref[:, dv:]


def _row_max(mrun):
    return jnp.broadcast_to(jnp.max(mrun, axis=-1, keepdims=True), mrun.shape)


def _attn_scratch(n_sub, rows, ts):
    return [pltpu.VMEM((n_sub, rows, ts), F32), pltpu.VMEM((rows, LANES), F32), pltpu.VMEM((rows, 2 * LANES), F32)]


def _flash_body(n_ref, q_ref, k_ref, v_ref, o_ref, s_ref, m_ref, a_ref, *, tk, stack):
    tq = q_ref.shape[1]
    q_w = q_ref.shape[2] // stack
    q = jnp.concatenate([q_ref[0, :, c * q_w:(c + 1) * q_w] for c in range(stack)], axis=0)
    m = _row_max(_score_pass(q, k_ref, s_ref, m_ref, tk, n_ref[0]))
    o = _value_pass(v_ref, s_ref, a_ref, tk, n_ref[0], lambda jj: m).astype(o_ref.dtype)
    o_ref[0] = jnp.concatenate([o[c * tq:(c + 1) * tq] for c in range(stack)], axis=1)


def _flash(q, k, v, *, kv_heads, stack, q_w, k_col0, v_col0, tq, ts, tk):
    B, S, _ = q.shape
    dv = LANES
    return pl.pallas_call(
        functools.partial(_flash_body, tk=tk, stack=stack),
        grid=(B, kv_heads, S // tq),
        in_specs=[pl.BlockSpec(memory_space=pltpu.SMEM),
                  pl.BlockSpec((1, tq, stack * q_w), lambda b, h, i: (b, i, h)),
                  pl.BlockSpec((1, S, q_w), lambda b, h, i: (b, 0, k_col0 + h)),
                  pl.BlockSpec((1, S, dv), lambda b, h, i: (b, 0, v_col0 + h))],
        out_specs=pl.BlockSpec((1, tq, stack * dv), lambda b, h, i: (b, i, h)),
        out_shape=jax.ShapeDtypeStruct((B, S, kv_heads * stack * dv), BF16),
        scratch_shapes=_attn_scratch(S // ts, stack * tq, ts),
        compiler_params=_cparams("parallel", "parallel", "arbitrary"),
        name="flash_attn",
    )(jnp.full((1,), S // tk, jnp.int32), q, k, v)


def _diff_body(n_ref, lam_ref, cb_ref, q_ref, k_ref, v_ref, diag_ref, corner_ref, o_ref, s_ref, m_ref, a_ref, *,
               tk, out_scale):
    h, qi = pl.program_id(1), pl.program_id(2)
    n_sub = s_ref.shape[0]
    tq = q_ref.shape[1]
    nd = REL_MAX_DIST
    qf = q_ref[0].astype(F32) * (B_QK ** -0.5 * LOG2E)
    lane = lax.broadcasted_iota(jnp.int32, qf.shape, 1)
    c_lo, c_hi = cb_ref[0, h], cb_ref[1, h]
    side_shift = lambda jj: jnp.where(jj < qi, c_lo, jnp.where(jj > qi, c_hi, 0.0))
    tile_max = lambda s: _lane_fold(jnp.maximum, s[:, :LANES], s[:, LANES:])

    q = jnp.concatenate([jnp.where(lane < B_QK, qf, 0.0), jnp.where(lane >= B_QK, qf, 0.0)], axis=0).astype(BF16)
    row = lax.broadcasted_iota(jnp.int32, (2 * tq, LANES), 0) % tq
    pen_lo = jnp.where(row < nd, -jnp.inf, 0.0)
    pen_hi = jnp.where(row >= tq - nd, -jnp.inf, 0.0)
    zero = jnp.zeros_like(pen_lo)

    def sub_max(jj, blocks):
        blocks = list(blocks)
        blocks[-1] = blocks[-1] + jnp.where(jj == qi - 1, pen_lo, zero)
        blocks[0] = blocks[0] + jnp.where(jj == qi + 1, pen_hi, zero)
        shift = jnp.where(jj < qi, c_lo, jnp.where(jj > qi, c_hi, -jnp.inf))
        return functools.reduce(jnp.maximum, blocks) + shift

    mrun = _score_pass(q, k_ref, s_ref, m_ref, tk, n_ref[0], sub_max=sub_max)

    diag = diag_ref[0]
    s = s_ref[qi] + jnp.concatenate([diag, diag], axis=0)
    s_ref[qi] = s
    mrun = jnp.maximum(mrun, tile_max(s))

    def neighbour(jj, rows0, cols0, corner, c):
        def fix(mrun):
            for r in (rows0, tq + rows0):
                fixed = s_ref[jj, r:r + nd, cols0:cols0 + nd] + corner
                s_ref[jj, r:r + nd, cols0:cols0 + nd] = fixed
                pieces = [mrun[:r], jnp.maximum(mrun[r:r + nd], fixed + c), mrun[r + nd:]]
                mrun = jnp.concatenate([p for p in pieces if p.shape[0]], axis=0)
            return mrun
        return fix

    mrun = lax.cond(qi >= 1, neighbour(jnp.maximum(qi - 1, 0), 0, tq - nd, corner_ref[0, 0], c_lo),
                    lambda m_: m_, mrun)
    mrun = lax.cond(qi + 1 < n_sub, neighbour(jnp.minimum(qi + 1, n_sub - 1), tq - nd, 0, corner_ref[0, 1], c_hi),
                    lambda m_: m_, mrun)
    m = _row_max(mrun)
    o = _value_pass(v_ref, s_ref, a_ref, tk, n_ref[0], lambda jj: m - side_shift(jj))
    o = o[:tq] - lam_ref[0] * o[tq:]
    o_ref[0] = (_rms(o) * out_scale).astype(o_ref.dtype)


def _rel_bucket(rel):
    half = REL_BUCKETS // 2
    max_exact = half // 2
    ret = (rel > 0).astype(jnp.int32) * half
    n = jnp.abs(rel)
    large = max_exact + (jnp.log(jnp.maximum(n, 1).astype(F32) / max_exact)
                         / math.log(REL_MAX_DIST / max_exact) * (half - max_exact)).astype(jnp.int32)
    large = jnp.minimum(large, half - 1)
    return ret + jnp.where(n < max_exact, n, large)


def _diff_bias_tables(rel_bias, tq):
    nd = REL_MAX_DIST
    rb = rel_bias.astype(F32) * LOG2E
    far = jnp.stack([rb[REL_BUCKETS // 2 - 1], rb[REL_BUCKETS - 1]])

    def table(rel):
        bucket = _rel_bucket(rel)
        out = jnp.zeros((B_HEADS,) + rel.shape, F32)
        for b in range(REL_BUCKETS):
            out = jnp.where(bucket[None] == b, rb[b].reshape((B_HEADS,) + (1,) * rel.ndim), out)
        return out

    a, e = jnp.arange(tq), jnp.arange(nd)
    diag = table(a[None, :] - a[:, None])
    lo = table((tq - nd + e[None, :]) - tq - e[:, None]) - far[0][:, None, None]
    hi = table(e[None, :] + tq - (tq - nd + e[:, None])) - far[1][:, None, None]
    return diag, jnp.stack([lo, hi], axis=1), far


def _diff_attn(zb, lam, diag, corner, far, *, tq, tk, out_scale):
    B, S, _ = zb.shape
    nd = REL_MAX_DIST
    assert tq >= nd, "far key chunks must lie beyond the last distinct relative bucket"
    assert nd == LANES, "the neighbours' near corner is handled as one lane block"
    H = B_HEADS
    smem = pl.BlockSpec(memory_space=pltpu.SMEM)
    return pl.pallas_call(
        functools.partial(_diff_body, tk=tk, out_scale=out_scale),
        grid=(B, H, S // tq),
        in_specs=[smem, smem, smem,
                  pl.BlockSpec((1, tq, B_V), lambda b, h, i: (b, i, h)),
                  pl.BlockSpec((1, S, B_V), lambda b, h, i: (b, 0, H + h)),
                  pl.BlockSpec((1, S, B_V), lambda b, h, i: (b, 0, 2 * H + h)),
                  pl.BlockSpec((1, tq, tq), lambda b, h, i: (h, 0, 0)),
                  pl.BlockSpec((1, 2, nd, nd), lambda b, h, i: (h, 0, 0, 0))],
        out_specs=pl.BlockSpec((1, tq, B_V), lambda b, h, i: (b, i, h)),
        out_shape=jax.ShapeDtypeStruct((B, S, H * B_V), BF16),
        scratch_shapes=_attn_scratch(S // tq, 2 * tq, tq),
        compiler_params=_cparams("parallel", "parallel", "arbitrary"),
        name="diff_attn",
    )(jnp.full((1,), S // tk, jnp.int32), lam, far, zb, zb, zb, diag, corner)


def _pool_body(u_ref, w_ref, sc_ref, o_ref, pad_ref, *, chunk):
    g = pl.program_id(1)
    S = u_ref.shape[1]
    pad_ref[0:HALO, :] = jnp.zeros((HALO, LANES), F32)
    pad_ref[HALO + S:, :] = jnp.zeros((HALO, LANES), F32)
    pad_ref[HALO:HALO + S, :] = u_ref[0].astype(F32)
    w_mat, sc = w_ref[0], sc_ref[...]

    def pooled(win):
        def body(c, _):
            r0 = pl.multiple_of(c * chunk, chunk)
            tot = pad_ref[pl.ds(r0 + HALO - win // 2, chunk), :]
            for j in range(1 - win // 2, win // 2):
                tot = tot + pad_ref[pl.ds(r0 + HALO + j, chunk), :]
            t = r0 + lax.broadcasted_iota(jnp.int32, (chunk, 1), 0)
            cnt = jnp.clip(t - win // 2 + win, 0, S) - jnp.clip(t - win // 2, 0, S)
            d = tot / cnt.astype(F32) - pad_ref[pl.ds(r0 + HALO, chunk), :]
            y = jnp.dot(d.astype(BF16), w_mat, preferred_element_type=F32) * sc
            o_ref[0, pl.ds(r0, chunk), :] = y.astype(o_ref.dtype)
            return 0
        lax.fori_loop(0, S // chunk, body, 0)

    for gi, win in enumerate(POOL_WINDOWS):
        pl.when(g == gi)(functools.partial(pooled, win))


def _pool(zc, w_pool, pool_scale, chunk):
    B, S, _ = zc.shape
    blk = pl.BlockSpec((1, S, LANES), lambda b, g: (b, 0, g))
    return pl.pallas_call(
        functools.partial(_pool_body, chunk=chunk),
        grid=(B, C_GROUPS),
        in_specs=[blk, pl.BlockSpec((1, C_GROUP_W, C_GROUP_W), lambda b, g: (g, 0, 0)),
                  pl.BlockSpec((1, LANES), lambda b, g: (0, g))],
        out_specs=blk,
        out_shape=jax.ShapeDtypeStruct((B, S, BRANCH_W), BF16),
        scratch_shapes=[pltpu.VMEM((S + 2 * HALO, LANES), F32)],
        compiler_params=_cparams("parallel", "arbitrary"),
        name="pool_mixer",
    )(zc, w_pool, pool_scale)


def _gated_merge_body(h_ref, oa_ref, ob_ref, oc_ref, od_ref, wg_ref, bg_ref, wl_ref, m_ref):
    h = h_ref[...]
    merged = None
    for b, o_ref in enumerate((oa_ref, ob_ref, oc_ref, od_ref)):
        gate = jax.nn.sigmoid(jnp.dot(h, wg_ref[b], preferred_element_type=F32) + bg_ref[b:b + 1, :])
        term = gate * jnp.dot(o_ref[...], wl_ref[b], preferred_element_type=F32)
        merged = term if merged is None else merged + term
    m_ref[...] = merged.astype(BF16)


def _gated_merge(h, outs, wg, bg, wl, tm, tn):
    T, D = h.shape
    row = lambda w_: pl.BlockSpec((tm, w_), lambda i, j: (i, 0))
    return pl.pallas_call(
        _gated_merge_body,
        grid=(T // tm, D // tn),
        in_specs=[row(D)] + [row(BRANCH_W)] * N_BRANCH + [
            pl.BlockSpec((N_BRANCH, D, tn), lambda i, j: (0, 0, j)),
            pl.BlockSpec((N_BRANCH, tn), lambda i, j: (0, j)),
            pl.BlockSpec((N_BRANCH, BRANCH_W, tn), lambda i, j: (0, 0, j))],
        out_specs=pl.BlockSpec((tm, tn), lambda i, j: (i, j)),
        out_shape=jax.ShapeDtypeStruct((T, D), BF16),
        compiler_params=_cparams("parallel", "arbitrary"),
        name="gated_merge",
    )(h, *outs, wg, bg, wl)


ROUTE_ROWS = 8


def _out_proj_body(m_ref, x_ref, wo_ref, g2_ref, wr_ref, rb_ref, xn_ref, h2_ref, rt_ref):
    xn = x_ref[...] + jnp.dot(m_ref[...], wo_ref[...], preferred_element_type=F32)
    xn_ref[...] = xn
    h2 = _rms(xn, g2_ref[...])
    h2_ref[...] = h2.astype(BF16)
    h_hi = h2.astype(BF16)
    h_lo = (h2 - h_hi.astype(F32)).astype(BF16)
    hi = jnp.dot(h_hi, wr_ref[...], preferred_element_type=F32)
    logits = hi[:, :ROUTER_W] + hi[:, ROUTER_W:] + jnp.dot(h_lo, wr_ref[:, :ROUTER_W], preferred_element_type=F32)
    rt_ref[...] = _route_tile(logits + rb_ref[...])


def _out_proj(merged, x, wo, g2, wr, route_bias, tm):
    T, D = x.shape
    row = pl.BlockSpec((tm, D), lambda i: (i, 0))
    return pl.pallas_call(
        _out_proj_body,
        grid=(T // tm,),
        in_specs=[row, row, _resident((D, D)), _resident((1, D)), _resident((D, 2 * ROUTER_W)),
                  _resident((1, ROUTER_W))],
        out_specs=[row, row, pl.BlockSpec((ROUTE_ROWS, tm), lambda i: (0, i))],
        out_shape=[jax.ShapeDtypeStruct((T, D), F32), jax.ShapeDtypeStruct((T, D), BF16),
                   jax.ShapeDtypeStruct((ROUTE_ROWS, T), F32)],
        compiler_params=_cparams("parallel"),
        name="out_proj_route",
    )(merged, x, wo, g2, wr, route_bias)


def _route_tile(lg):
    lane = lax.broadcasted_iota(jnp.int32, lg.shape, 1).astype(F32)
    neg = -jnp.inf
    row_max = lambda v: jnp.max(v, axis=-1, keepdims=True)
    first_at = lambda v, m: jnp.min(jnp.where(v == m, lane, float(LANES)), axis=-1, keepdims=True)
    g_lg = jnp.where(lane < N_GROUPS, lg, neg)
    g_max = row_max(g_lg)
    g_top = first_at(g_lg, g_max)
    pg_top = 1.0 / jnp.sum(jnp.exp(g_lg - g_max), axis=-1, keepdims=True)
    e_lo = N_GROUPS + EXP_PER_GROUP * g_top
    e_lg = jnp.where(jnp.logical_and(lane >= e_lo, lane < e_lo + EXP_PER_GROUP), lg, neg)
    picks, maxes = [], []
    for _ in range(TOP_K):
        m = row_max(e_lg)
        i = first_at(e_lg, m)
        picks.append(i)
        maxes.append(m)
        e_lg = jnp.where(lane == i, neg, e_lg)
    w = [jnp.exp(m - maxes[0]) for m in maxes]
    w_sum = functools.reduce(lambda a, b: a + b, w)
    cols = [i - N_GROUPS for i in picks] + [pg_top * wk / w_sum for wk in w]
    tile = jnp.zeros(lg.shape, F32)
    for c, v in enumerate(cols):
        tile = jnp.where(lane == c, v, tile)
    return tile.T[:ROUTE_ROWS]


def _dispatch(routed, tb):
    T = routed.shape[1]
    A = T * TOP_K
    flat_e = routed[:TOP_K].astype(jnp.int32).T.reshape(A)
    flat_w = routed[TOP_K:2 * TOP_K].T.reshape(A)
    iota = jnp.arange(A, dtype=jnp.int32)
    se, order, sw = lax.sort((flat_e, iota, flat_w), num_keys=1, is_stable=True)
    experts = jnp.arange(N_EXPERTS, dtype=jnp.int32)
    counts = jnp.sum((flat_e[:, None] == experts[None, :]).astype(jnp.int32), axis=0)
    start = jnp.cumsum(counts) - counts
    padded = (counts + tb - 1) // tb * tb
    pend = jnp.cumsum(padded)
    pstart = pend - padded
    n_blocks = -(-A // tb) + N_EXPERTS
    P = n_blocks * tb
    blk_first = jnp.arange(n_blocks, dtype=jnp.int32) * tb
    blk_exp = jnp.minimum(jnp.sum((pend[None, :] <= blk_first[:, None]).astype(jnp.int32), axis=1), N_EXPERTS - 1)
    n_valid = (pend[-1] // tb).astype(jnp.int32).reshape(1)
    slot = jnp.arange(P, dtype=jnp.int32)
    before = (pend[None, :] <= slot[:, None]).astype(jnp.int32)
    shift = jnp.sum(before * (padded - counts)[None, :], axis=1)
    in_pad = jnp.any(jnp.logical_and(slot[:, None] >= (pstart + counts)[None, :], slot[:, None] < pend[None, :]),
                     axis=1)
    valid = jnp.logical_and(jnp.logical_not(in_pad), slot < pend[-1])
    src = jnp.where(valid, slot - shift, slot % A)
    slot_tok = order[src] // TOP_K
    slot_w = jnp.where(valid, sw[src], 0.0)
    dest = iota + jnp.sum((se[:, None] == experts[None, :]).astype(jnp.int32) * (pstart - start)[None, :], axis=1)
    _, slot_of = lax.sort((order, dest), num_keys=1)
    return slot_tok, slot_w, blk_exp, n_valid, slot_of.reshape(T, TOP_K).T


def _expert_body(be_ref, nv_ref, x_ref, sw_ref, wg_ref, wu_ref, wd_ref, o_ref, wg_s, wu_s, wd_s):
    i = pl.program_id(0)

    @pl.when(jnp.logical_or(i == 0, be_ref[i] != be_ref[jnp.maximum(i - 1, 0)]))
    def _():
        wg_s[...] = wg_ref[0, 0].astype(BF16)
        wu_s[...] = wu_ref[0, 0].astype(BF16)
        wd_s[...] = wd_ref[0, 0].astype(BF16)

    @pl.when(i < nv_ref[0])
    def _():
        x = x_ref[...]
        g = jnp.dot(x, wg_s[...], preferred_element_type=F32)
        u = jnp.dot(x, wu_s[...], preferred_element_type=F32)
        a = (g * jax.nn.sigmoid(g) * u).astype(BF16)
        y = jnp.dot(a, wd_s[...], preferred_element_type=F32)
        sw = sw_ref[...]
        o_ref[...] = jnp.concatenate([y[:, c * LANES:(c + 1) * LANES] * sw for c in range(y.shape[1] // LANES)],
                                     axis=1).astype(o_ref.dtype)

    @pl.when(i >= nv_ref[0])
    def _():
        o_ref[...] = jnp.zeros(o_ref.shape, o_ref.dtype)


def _expert_ffn(xs, slot_w, blk_exp, n_valid, wg, wu, wd, layer, tb):
    P, D = xs.shape
    grid_spec = pltpu.PrefetchScalarGridSpec(
        num_scalar_prefetch=2,
        grid=(P // tb,),
        in_specs=[pl.BlockSpec((tb, D), lambda i, be, nv: (i, 0)),
                  pl.BlockSpec((tb, LANES), lambda i, be, nv: (i, 0)),
                  pl.BlockSpec((1, 1, D, D_EXPERT), lambda i, be, nv: (layer, be[i], 0, 0)),
                  pl.BlockSpec((1, 1, D, D_EXPERT), lambda i, be, nv: (layer, be[i], 0, 0)),
                  pl.BlockSpec((1, 1, D_EXPERT, D), lambda i, be, nv: (layer, be[i], 0, 0))],
        out_specs=pl.BlockSpec((tb, D), lambda i, be, nv: (i, 0)),
        scratch_shapes=[pltpu.VMEM((D, D_EXPERT), BF16), pltpu.VMEM((D, D_EXPERT), BF16),
                        pltpu.VMEM((D_EXPERT, D), BF16)],
    )
    return pl.pallas_call(
        _expert_body,
        grid_spec=grid_spec,
        out_shape=jax.ShapeDtypeStruct((P, D), BF16),
        compiler_params=_cparams("arbitrary"),
        name="expert_ffn",
    )(blk_exp, n_valid, xs, jnp.broadcast_to(slot_w[:, None], (P, LANES)), wg, wu, wd)


def _final_norm_body(*refs):
    x_ref, y_refs, (g_ref, o_ref) = refs[0], refs[1:-2], refs[-2:]
    o_ref[...] = _rms(_residual_sum(x_ref, y_refs), g_ref[...])


def _final_norm(x, adds, g, row0, rows, tm):
    D = x.shape[1]
    first = row0 // tm
    src = pl.BlockSpec((tm, D), lambda i: (first + i, 0))
    return pl.pallas_call(
        _final_norm_body, grid=(rows // tm,),
        in_specs=[src] * (1 + len(adds)) + [_resident((1, D))],
        out_specs=pl.BlockSpec((tm, D), lambda i: (i, 0)),
        out_shape=jax.ShapeDtypeStruct((rows, D), F32), compiler_params=_cparams("parallel"), name="final_norm",
    )(x, *adds, g)


def _rope_tables(seq):
    def angles(pos, dim):
        inv = 1.0 / (ROPE_THETA ** (jnp.arange(0, dim, 2, dtype=F32) / dim))
        ang = pos.astype(F32)[:, None] * inv[None, :]
        return jnp.cos(ang), jnp.sin(ang)
    pos = jnp.arange(seq)
    c1, s1 = angles(pos, A_ROPE)
    cr, sr = angles(pos // GRID_W, D_HEAD // 2)
    cc, sc = angles(pos % GRID_W, D_HEAD // 2)
    pad1, pad0 = jnp.ones((seq, ROPE_PAD), F32), jnp.zeros((seq, ROPE_PAD), F32)
    return (jnp.concatenate([c1, c1, pad1], 1), jnp.concatenate([-s1, s1, pad0], 1),
            jnp.concatenate([cr, cr, cc, cc], 1), jnp.concatenate([-sr, sr, -sc, sc], 1))


def _layer_weights(l, w_in, w_uq, w_ukv, w_route_group, w_route_expert):
    D = D_MODEL
    k_r_end = A_Q_LORA + A_KV_LORA + A_ROPE
    win = jnp.concatenate([w_in[l][:, :k_r_end], jnp.zeros((D, ROPE_PAD), F32), w_in[l][:, k_r_end:]], 1)
    uq = w_uq[l].reshape(A_Q_LORA, A_HEADS, A_NOPE + A_ROPE)
    uq_rope = jnp.pad(uq[:, :, A_NOPE:], ((0, 0), (0, 0), (0, ROPE_PAD)))
    wuq = jnp.concatenate([uq[:, :, :A_NOPE].reshape(A_Q_LORA, -1), uq_rope.reshape(A_Q_LORA, -1)], 1)
    ukv = w_ukv[l].reshape(A_KV_LORA, A_HEADS, A_NOPE + A_V)
    wukv = jnp.concatenate([ukv[:, :, :A_NOPE].reshape(A_KV_LORA, -1), ukv[:, :, A_NOPE:].reshape(A_KV_LORA, -1)], 1)
    wr = jnp.concatenate([w_route_group[l], w_route_expert[l],
                          jnp.zeros((D, ROUTER_W - N_GROUPS - N_EXPERTS), F32)], 1)
    wr_hi = wr.astype(BF16)
    wr_lo = (wr - wr_hi.astype(F32)).astype(BF16)
    return win.astype(BF16), wuq.astype(BF16), wukv.astype(BF16), jnp.concatenate([wr_hi, wr_lo], 1)


def kernel(x_prompt, x_sample, norm1_g, w_in, q_norm_g, kv_norm_g, w_uq, w_ukv, lam_q1, lam_k1, lam_q2, lam_k2,
           rel_bias, w_pool, pool_scale, qk_norm_q, qk_norm_k, w_lift, w_gate, b_gate, w_out, norm2_g,
           w_route_group, b_route_group, w_route_expert, b_route_expert, w_exp_gate, w_exp_up, w_exp_down,
           final_g):
    assert x_prompt.shape[1:] == x_sample.shape[1:], "both request groups must share (seq, d_model)"
    n_prompt, S, D = x_prompt.shape
    B = n_prompt + x_sample.shape[0]
    T = B * S
    t = _tiles(S, T)
    x = None
    depth = w_in.shape[0]
    tabs = _rope_tables(S)
    diag, corner, far = _diff_bias_tables(rel_bias, t["tq"])
    row2 = lambda v: v.reshape(1, -1).astype(F32)

    moe_out = ()
    for l in range(depth):
        lam_init = 0.8 - 0.6 * math.exp(-0.3 * l)
        lam = (jnp.exp(jnp.sum(lam_q1[l] * lam_k1[l])) - jnp.exp(jnp.sum(lam_q2[l] * lam_k2[l])) + lam_init)
        win, wuq, wukv, wr = _layer_weights(l, w_in, w_uq, w_ukv, w_route_group, w_route_expert)

        xs = (x_prompt.reshape(-1, D), x_sample.reshape(-1, D)) if l == 0 else (x,)
        prep = (row2(q_norm_g[l]), row2(kv_norm_g[l]), wuq, wukv, row2(qk_norm_q[l]), row2(qk_norm_k[l])) + tabs
        x, h, zb, zc, qa, ka, va, qd, kd, vd = _norm_inproj(xs, moe_out, row2(norm1_g[l]), win, prep, S, t["tm_in"])
        seq3 = lambda a: a.reshape(B, S, a.shape[-1])
        rep = D_HEADS // D_KV_HEADS
        o_a = _flash(seq3(qa), seq3(ka), seq3(va), kv_heads=A_HEADS, stack=1, q_w=A_QK_W, k_col0=0, v_col0=0,
                     tq=t["rows"], ts=t["tq"], tk=t["tk"])
        o_b = _diff_attn(seq3(zb), lam.reshape(1).astype(F32), diag, corner, far, tq=t["tq"], tk=t["tk"],
                         out_scale=1.0 - lam_init)
        o_c = _pool(seq3(zc), w_pool[l].astype(BF16), row2(pool_scale[l]), t["pool_chunk"])
        o_d = _flash(seq3(qd), seq3(kd), seq3(vd), kv_heads=D_KV_HEADS, stack=rep, q_w=D_HEAD, k_col0=0,
                     v_col0=0, tq=t["rows"] // rep, ts=t["tq"], tk=t["tk"])
        outs = [o.reshape(T, BRANCH_W) for o in (o_a, o_b, o_c, o_d)]
        merged = _gated_merge(h, outs, w_gate[l].astype(BF16), b_gate[l].astype(F32), w_lift[l].astype(BF16),
                              t["tm_merge"], t["tn"])
        route_bias = jnp.concatenate([b_route_group[l], b_route_expert[l],
                                      jnp.zeros((ROUTER_W - N_GROUPS - N_EXPERTS,), F32)]).reshape(1, ROUTER_W)
        x, h2, routed = _out_proj(merged, x, w_out[l].astype(BF16), row2(norm2_g[l]), wr, route_bias.astype(F32),
                                  t["tm"])
        slot_tok, slot_w, blk_exp, n_valid, slot_of = _dispatch(routed, t["tb"])
        ys = _expert_ffn(h2[slot_tok], slot_w, blk_exp, n_valid, w_exp_gate, w_exp_up, w_exp_down, l, t["tb"])
        moe_out = tuple(ys[slot_of[k]] for k in range(TOP_K))

    t_prompt = n_prompt * S
    y_prompt = _final_norm(x, moe_out, row2(final_g), 0, t_prompt, t["tm"]).reshape(n_prompt, S, D)
    y_sample = _final_norm(x, moe_out, row2(final_g), t_prompt, T - t_prompt, t["tm"]).reshape(B - n_prompt, S, D)
    return y_prompt, y_sample
```

```python
import functools
import math

import jax
import jax.numpy as jnp
from jax import lax
from jax.experimental import pallas as pl
from jax.experimental.pallas import tpu as pltpu

F32 = jnp.float32
BF16 = jnp.bfloat16
LOG2E = 1.4426950408889634

D_MODEL = 2048
GRID_W = 64
BRANCH_W = 512
N_BRANCH = 4
ROPE_THETA = 10000.0
EPS = 1e-6
A_HEADS, A_Q_LORA, A_KV_LORA, A_NOPE, A_ROPE, A_V = 4, 384, 128, 128, 64, 128
B_HEADS, B_QK, B_V = 4, 64, 128
REL_BUCKETS, REL_MAX_DIST = 32, 128
POOL_WINDOWS = (2, 4, 8, 16)
C_GROUPS, C_GROUP_W = 4, 128
D_HEADS, D_KV_HEADS, D_HEAD = 4, 2, 128
N_GROUPS, EXP_PER_GROUP, N_EXPERTS, TOP_K, D_EXPERT = 4, 8, 32, 2, 512

LANES = 128
ROPE_PAD = LANES - A_ROPE
ZA_W = A_Q_LORA + A_KV_LORA + LANES
ZB_W = 3 * B_HEADS * B_V
ZC_W = BRANCH_W
ZD_W = (D_HEADS + 2 * D_KV_HEADS) * D_HEAD
A_QK_W = 2 * LANES
ROUTER_W = LANES
HALO = 16
VMEM_LIMIT = 56 * 2 ** 20


def _tiles(seq, tokens):
    def pick(n, pref):
        t = min(pref, n)
        while n % t:
            t //= 2
        return t
    return dict(
        tm=pick(seq, 512),
        tm_in=pick(seq, 256),
        tq=pick(seq, 512),
        rows=pick(seq, 1024),
        tk=pick(seq, 8192),
        tm_merge=pick(tokens, 1024),
        tn=512,
        tb=512,
        pool_chunk=pick(seq, 1024),
    )


def _cparams(*sem):
    return pltpu.CompilerParams(dimension_semantics=sem, vmem_limit_bytes=VMEM_LIMIT)


def _resident(shape):
    zeros = (0,) * len(shape)
    return pl.BlockSpec(shape, lambda *_: zeros, pipeline_mode=pl.Buffered(1))


def _rms(x, g=None):
    y = x * lax.rsqrt(jnp.mean(x * x, axis=-1, keepdims=True) + EPS)
    return y if g is None else y * g


def _rope_lanes(x, c, s):
    lane = lax.broadcasted_iota(jnp.int32, x.shape, 1)
    first_half = (lane & 32) == 0
    partner = jnp.where(first_half, pltpu.roll(x, 96, 1), pltpu.roll(x, 32, 1))
    return x * c + partner * s


def _residual_sum(x_ref, y_refs):
    x = x_ref[...]
    for y_ref in y_refs:
        x = x + y_ref[...].astype(F32)
    return x


N_PREP_PARAMS = 10
PREP_OUT_W = (A_HEADS * A_QK_W, A_HEADS * A_QK_W, A_HEADS * A_V, D_HEADS * D_HEAD, D_KV_HEADS * D_HEAD,
              D_KV_HEADS * D_HEAD)


def _norm_inproj_body(*refs, n_x, n_add, first_rows):
    x_refs, y_refs = refs[:n_x], refs[n_x:n_x + n_add]
    g_ref, w_ref = refs[n_x + n_add:n_x + n_add + 2]
    prep_refs = refs[n_x + n_add + 2:n_x + n_add + 2 + N_PREP_PARAMS]
    outs = refs[n_x + n_add + 2 + N_PREP_PARAMS:]
    if n_x == 2:
        x = jnp.where(pl.program_id(0) < first_rows, x_refs[0][...], x_refs[1][...])
    else:
        x = _residual_sum(x_refs[0], y_refs)
    if n_x == 2 or n_add:
        outs[0][...] = x
        outs = outs[1:]
    h_ref, zb_ref, zc_ref = outs[:3]
    xg = x * g_ref[...]
    r = lax.rsqrt(jnp.mean(x * x, axis=-1, keepdims=True) + EPS)
    xg16 = xg.astype(BF16)
    proj = lambda lo, w_: jnp.dot(xg16, w_ref[:, lo:lo + w_], preferred_element_type=F32) * r
    h_ref[...] = (xg * r).astype(BF16)
    za, zd = proj(0, ZA_W), proj(ZA_W + ZB_W + ZC_W, ZD_W)
    _mixer_prep(za, zd, prep_refs, outs[3:])
    zb_ref[...] = proj(ZA_W, ZB_W).astype(BF16)
    zc_ref[...] = proj(ZA_W + ZB_W, ZC_W).astype(BF16)


def _norm_inproj(xs, adds, g, w, prep, seq, tm):
    D = xs[0].shape[1]
    T = sum(x.shape[0] for x in xs)
    per_seq = seq // tm
    row = lambda w_: pl.BlockSpec((tm, w_), lambda i: (i, 0))
    if len(xs) == 2:
        na = xs[0].shape[0] // tm
        x_specs = [pl.BlockSpec((tm, D), lambda i: (jnp.minimum(i, na - 1), 0)),
                   pl.BlockSpec((tm, D), lambda i: (jnp.maximum(i - na, 0), 0))]
    else:
        na, x_specs = None, [row(D)]
    new_x = len(xs) == 2 or bool(adds)
    sum_spec, sum_shape = ([row(D)], [jax.ShapeDtypeStruct((T, D), F32)]) if new_x else ([], [])
    tab = pl.BlockSpec((tm, LANES), lambda i: (i % per_seq, 0))
    out_w = (D, ZB_W, ZC_W) + PREP_OUT_W
    outs = pl.pallas_call(
        functools.partial(_norm_inproj_body, n_x=len(xs), n_add=len(adds), first_rows=na),
        grid=(T // tm,),
        in_specs=x_specs + [row(D)] * len(adds) + [_resident((1, D)), _resident(w.shape)]
        + [_resident(p.shape) for p in prep[:N_PREP_PARAMS - 4]] + [tab] * 4,
        out_specs=sum_spec + [row(w_) for w_ in out_w],
        out_shape=sum_shape + [jax.ShapeDtypeStruct((T, w_), BF16) for w_ in out_w],
        compiler_params=_cparams("parallel"),
        name="norm_inproj",
    )(*xs, *adds, g, w, *prep)
    return list(outs) if new_x else [xs[0]] + list(outs)


def _mixer_prep(za, zd, prep_refs, out_refs):
    gq_ref, gkv_ref, wuq_ref, wukv_ref, gdq_ref, gdk_ref, ca_ref, sa_ref, cd_ref, sd_ref = prep_refs
    qa_ref, ka_ref, va_ref, qd_ref, kd_ref, vd_ref = out_refs
    a_scale = (A_NOPE + A_ROPE) ** -0.5 * LOG2E
    d_scale = D_HEAD ** -0.5 * LOG2E
    ca, sa, cd, sd = ca_ref[...], sa_ref[...], cd_ref[...], sd_ref[...]
    cq = _rms(za[:, :A_Q_LORA], gq_ref[...]).astype(BF16)
    q = jnp.dot(cq, wuq_ref[...], preferred_element_type=F32)
    ckv = _rms(za[:, A_Q_LORA:A_Q_LORA + A_KV_LORA], gkv_ref[...]).astype(BF16)
    kv = jnp.dot(ckv, wukv_ref[...], preferred_element_type=F32)
    k_rope = _rope_lanes(za[:, A_Q_LORA + A_KV_LORA:], ca, sa).astype(BF16)
    nope_w = A_HEADS * A_NOPE
    for h in range(A_HEADS):
        lo = h * A_QK_W
        qa_ref[:, lo:lo + LANES] = (q[:, h * LANES:(h + 1) * LANES] * a_scale).astype(BF16)
        q_rope = _rope_lanes(q[:, nope_w + h * LANES:nope_w + (h + 1) * LANES], ca, sa)
        qa_ref[:, lo + LANES:lo + 2 * LANES] = (q_rope * a_scale).astype(BF16)
        ka_ref[:, lo:lo + LANES] = kv[:, h * LANES:(h + 1) * LANES].astype(BF16)
        ka_ref[:, lo + LANES:lo + 2 * LANES] = k_rope
    va_ref[...] = kv[:, nope_w:].astype(BF16)
    for h in range(D_HEADS):
        xh = _rms(zd[:, h * D_HEAD:(h + 1) * D_HEAD], gdq_ref[...])
        qd_ref[:, h * D_HEAD:(h + 1) * D_HEAD] = (_rope_lanes(xh, cd, sd) * d_scale).astype(BF16)
    k_off = D_HEADS * D_HEAD
    for h in range(D_KV_HEADS):
        xh = _rms(zd[:, k_off + h * D_HEAD:k_off + (h + 1) * D_HEAD], gdk_ref[...])
        kd_ref[:, h * D_HEAD:(h + 1) * D_HEAD] = _rope_lanes(xh, cd, sd).astype(BF16)
    vd_ref[...] = zd[:, k_off + D_KV_HEADS * D_HEAD:].astype(BF16)


def _qk(q, k):
    return lax.dot_general(q, k, (((1,), (1,)), ((), ())), preferred_element_type=F32)


def _lane_fold(fn, acc, x):
    for c in range(x.shape[1] // LANES):
        acc = fn(acc, x[:, c * LANES:(c + 1) * LANES])
    return acc


def _lane_blocks(x):
    return [x[:, i * LANES:(i + 1) * LANES] for i in range(x.shape[1] // LANES)]


def _score_pass(q, k_ref, s_ref, m_ref, tk, n_steps, sub_max=None):
    tq, ts = s_ref.shape[1:]
    sub = tk // ts
    m_ref[...] = jnp.full((tq, LANES), -jnp.inf, F32)

    def step(j, _):
        ks = pl.multiple_of(j * tk, tk)
        s = _qk(q, k_ref[0, pl.ds(ks, tk), :])
        mrun = None
        for c in range(sub):
            jj = j * sub + c
            sc = s[:, c * ts:(c + 1) * ts]
            s_ref[jj] = sc
            blocks = _lane_blocks(sc)
            m_sub = functools.reduce(jnp.maximum, blocks) if sub_max is None else sub_max(jj, blocks)
            mrun = m_sub if mrun is None else jnp.maximum(mrun, m_sub)
        m_ref[...] = jnp.maximum(m_ref[...], mrun)
        return 0

    lax.fori_loop(0, n_steps, step, 0)
    return m_ref[...]


def _value_pass(v_ref, s_ref, a_ref, tk, n_steps, m_of):
    tq, ts = s_ref.shape[1:]
    sub = tk // ts
    dv = v_ref.shape[2]
    ones = jnp.ones((tk, LANES), BF16)
    a_ref[...] = jnp.zeros(a_ref.shape, F32)

    def step(j, _):
        ks = pl.multiple_of(j * tk, tk)
        p = []
        for c in range(sub):
            jj = j * sub + c
            m = m_of(jj)
            s = s_ref[jj]
            p += [jnp.exp2((s[:, i * LANES:(i + 1) * LANES] - m).astype(BF16)) for i in range(ts // LANES)]
        v_ext = jnp.concatenate([v_ref[0, pl.ds(ks, tk), :], ones], axis=1)
        a_ref[...] += jnp.dot(jnp.concatenate(p, axis=1), v_ext, preferred_element_type=F32)
        return 0

    lax.fori_loop(0, n_steps, step, 0)
    return a_ref[:, :dv] / a_ref[:, dv:]


def _row_max(mrun):
    return jnp.broadcast_to(jnp.max(mrun, axis=-1, keepdims=True), mrun.shape)


def _attn_scratch(n_sub, rows, ts):
    return [pltpu.VMEM((n_sub, rows, ts), F32), pltpu.VMEM((rows, LANES), F32), pltpu.VMEM((rows, 2 * LANES), F32)]


def _flash_body(n_ref, q_ref, k_ref, v_ref, o_ref, s_ref, m_ref, a_ref, *, tk, stack):
    tq = q_ref.shape[1]
    q_w = q_ref.shape[2] // stack
    q = jnp.concatenate([q_ref[0, :, c * q_w:(c + 1) * q_w] for c in range(stack)], axis=0)
    m = _row_max(_score_pass(q, k_ref, s_ref, m_ref, tk, n_ref[0]))
    o = _value_pass(v_ref, s_ref, a_ref, tk, n_ref[0], lambda jj: m).astype(o_ref.dtype)
    o_ref[0] = jnp.concatenate([o[c * tq:(c + 1) * tq] for c in range(stack)], axis=1)


def _flash(q, k, v, *, kv_heads, stack, q_w, k_col0, v_col0, tq, ts, tk):
    B, S, _ = q.shape
    dv = LANES
    return pl.pallas_call(
        functools.partial(_flash_body, tk=tk, stack=stack),
        grid=(B, kv_heads, S // tq),
        in_specs=[pl.BlockSpec(memory_space=pltpu.SMEM),
                  pl.BlockSpec((1, tq, stack * q_w), lambda b, h, i: (b, i, h)),
                  pl.BlockSpec((1, S, q_w), lambda b, h, i: (b, 0, k_col0 + h)),
                  pl.BlockSpec((1, S, dv), lambda b, h, i: (b, 0, v_col0 + h))],
        out_specs=pl.BlockSpec((1, tq, stack * dv), lambda b, h, i: (b, i, h)),
        out_shape=jax.ShapeDtypeStruct((B, S, kv_heads * stack * dv), BF16),
        scratch_shapes=_attn_scratch(S // ts, stack * tq, ts),
        compiler_params=_cparams("parallel", "parallel", "arbitrary"),
        name="flash_attn",
    )(jnp.full((1,), S // tk, jnp.int32), q, k, v)


def _diff_body(n_ref, lam_ref, cb_ref, q_ref, k_ref, v_ref, diag_ref, corner_ref, o_ref, s_ref, m_ref, a_ref, *,
               tk, out_scale):
    h, qi = pl.program_id(1), pl.program_id(2)
    n_sub = s_ref.shape[0]
    tq = q_ref.shape[1]
    nd = REL_MAX_DIST
    qf = q_ref[0].astype(F32) * (B_QK ** -0.5 * LOG2E)
    lane = lax.broadcasted_iota(jnp.int32, qf.shape, 1)
    c_lo, c_hi = cb_ref[0, h], cb_ref[1, h]
    side_shift = lambda jj: jnp.where(jj < qi, c_lo, jnp.where(jj > qi, c_hi, 0.0))
    tile_max = lambda s: _lane_fold(jnp.maximum, s[:, :LANES], s[:, LANES:])

    q = jnp.concatenate([jnp.where(lane < B_QK, qf, 0.0), jnp.where(lane >= B_QK, qf, 0.0)], axis=0).astype(BF16)
    row = lax.broadcasted_iota(jnp.int32, (2 * tq, LANES), 0) % tq
    pen_lo = jnp.where(row < nd, -jnp.inf, 0.0)
    pen_hi = jnp.where(row >= tq - nd, -jnp.inf, 0.0)
    zero = jnp.zeros_like(pen_lo)

    def sub_max(jj, blocks):
        blocks = list(blocks)
        blocks[-1] = blocks[-1] + jnp.where(jj == qi - 1, pen_lo, zero)
        blocks[0] = blocks[0] + jnp.where(jj == qi + 1, pen_hi, zero)
        shift = jnp.where(jj < qi, c_lo, jnp.where(jj > qi, c_hi, -jnp.inf))
        return functools.reduce(jnp.maximum, blocks) + shift

    mrun = _score_pass(q, k_ref, s_ref, m_ref, tk, n_ref[0], sub_max=sub_max)

    diag = diag_ref[0]
    s = s_ref[qi] + jnp.concatenate([diag, diag], axis=0)
    s_ref[qi] = s
    mrun = jnp.maximum(mrun, tile_max(s))

    def neighbour(jj, rows0, cols0, corner, c):
        def fix(mrun):
            for r in (rows0, tq + rows0):
                fixed = s_ref[jj, r:r + nd, cols0:cols0 + nd] + corner
                s_ref[jj, r:r + nd, cols0:cols0 + nd] = fixed
                pieces = [mrun[:r], jnp.maximum(mrun[r:r + nd], fixed + c), mrun[r + nd:]]
                mrun = jnp.concatenate([p for p in pieces if p.shape[0]], axis=0)
            return mrun
        return fix

    mrun = lax.cond(qi >= 1, neighbour(jnp.maximum(qi - 1, 0), 0, tq - nd, corner_ref[0, 0], c_lo),
                    lambda m_: m_, mrun)
    mrun = lax.cond(qi + 1 < n_sub, neighbour(jnp.minimum(qi + 1, n_sub - 1), tq - nd, 0, corner_ref[0, 1], c_hi),
                    lambda m_: m_, mrun)
    m = _row_max(mrun)
    o = _value_pass(v_ref, s_ref, a_ref, tk, n_ref[0], lambda jj: m - side_shift(jj))
    o = o[:tq] - lam_ref[0] * o[tq:]
    o_ref[0] = (_rms(o) * out_scale).astype(o_ref.dtype)


def _rel_bucket(rel):
    half = REL_BUCKETS // 2
    max_exact = half // 2
    ret = (rel > 0).astype(jnp.int32) * half
    n = jnp.abs(rel)
    large = max_exact + (jnp.log(jnp.maximum(n, 1).astype(F32) / max_exact)
                         / math.log(REL_MAX_DIST / max_exact) * (half - max_exact)).astype(jnp.int32)
    large = jnp.minimum(large, half - 1)
    return ret + jnp.where(n < max_exact, n, large)


def _diff_bias_tables(rel_bias, tq):
    nd = REL_MAX_DIST
    rb = rel_bias.astype(F32) * LOG2E
    far = jnp.stack([rb[REL_BUCKETS // 2 - 1], rb[REL_BUCKETS - 1]])

    def table(rel):
        bucket = _rel_bucket(rel)
        out = jnp.zeros((B_HEADS,) + rel.shape, F32)
        for b in range(REL_BUCKETS):
            out = jnp.where(bucket[None] == b, rb[b].reshape((B_HEADS,) + (1,) * rel.ndim), out)
        return out

    a, e = jnp.arange(tq), jnp.arange(nd)
    diag = table(a[None, :] - a[:, None])
    lo = table((tq - nd + e[None, :]) - tq - e[:, None]) - far[0][:, None, None]
    hi = table(e[None, :] + tq - (tq - nd + e[:, None])) - far[1][:, None, None]
    return diag, jnp.stack([lo, hi], axis=1), far


def _diff_attn(zb, lam, diag, corner, far, *, tq, tk, out_scale):
    B, S, _ = zb.shape
    nd = REL_MAX_DIST
    assert tq >= nd, "far key chunks must lie beyond the last distinct relative bucket"
    assert nd == LANES, "the neighbours' near corner is handled as one lane block"
    H = B_HEADS
    smem = pl.BlockSpec(memory_space=pltpu.SMEM)
    return pl.pallas_call(
        functools.partial(_diff_body, tk=tk, out_scale=out_scale),
        grid=(B, H, S // tq),
        in_specs=[smem, smem, smem,
                  pl.BlockSpec((1, tq, B_V), lambda b, h, i: (b, i, h)),
                  pl.BlockSpec((1, S, B_V), lambda b, h, i: (b, 0, H + h)),
                  pl.BlockSpec((1, S, B_V), lambda b, h, i: (b, 0, 2 * H + h)),
                  pl.BlockSpec((1, tq, tq), lambda b, h, i: (h, 0, 0)),
                  pl.BlockSpec((1, 2, nd, nd), lambda b, h, i: (h, 0, 0, 0))],
        out_specs=pl.BlockSpec((1, tq, B_V), lambda b, h, i: (b, i, h)),
        out_shape=jax.ShapeDtypeStruct((B, S, H * B_V), BF16),
        scratch_shapes=_attn_scratch(S // tq, 2 * tq, tq),
        compiler_params=_cparams("parallel", "parallel", "arbitrary"),
        name="diff_attn",
    )(jnp.full((1,), S // tk, jnp.int32), lam, far, zb, zb, zb, diag, corner)


def _pool_body(u_ref, w_ref, sc_ref, o_ref, pad_ref, *, chunk):
    g = pl.program_id(1)
    S = u_ref.shape[1]
    pad_ref[0:HALO, :] = jnp.zeros((HALO, LANES), F32)
    pad_ref[HALO + S:, :] = jnp.zeros((HALO, LANES), F32)
    pad_ref[HALO:HALO + S, :] = u_ref[0].astype(F32)
    w_mat, sc = w_ref[0], sc_ref[...]

    def pooled(win):
        def body(c, _):
            r0 = pl.multiple_of(c * chunk, chunk)
            tot = pad_ref[pl.ds(r0 + HALO - win // 2, chunk), :]
            for j in range(1 - win // 2, win // 2):
                tot = tot + pad_ref[pl.ds(r0 + HALO + j, chunk), :]
            t = r0 + lax.broadcasted_iota(jnp.int32, (chunk, 1), 0)
            cnt = jnp.clip(t - win // 2 + win, 0, S) - jnp.clip(t - win // 2, 0, S)
            d = tot / cnt.astype(F32) - pad_ref[pl.ds(r0 + HALO, chunk), :]
            y = jnp.dot(d.astype(BF16), w_mat, preferred_element_type=F32) * sc
            o_ref[0, pl.ds(r0, chunk), :] = y.astype(o_ref.dtype)
            return 0
        lax.fori_loop(0, S // chunk, body, 0)

    for gi, win in enumerate(POOL_WINDOWS):
        pl.when(g == gi)(functools.partial(pooled, win))


def _pool(zc, w_pool, pool_scale, chunk):
    B, S, _ = zc.shape
    blk = pl.BlockSpec((1, S, LANES), lambda b, g: (b, 0, g))
    return pl.pallas_call(
        functools.partial(_pool_body, chunk=chunk),
        grid=(B, C_GROUPS),
        in_specs=[blk, pl.BlockSpec((1, C_GROUP_W, C_GROUP_W), lambda b, g: (g, 0, 0)),
                  pl.BlockSpec((1, LANES), lambda b, g: (0, g))],
        out_specs=blk,
        out_shape=jax.ShapeDtypeStruct((B, S, BRANCH_W), BF16),
        scratch_shapes=[pltpu.VMEM((S + 2 * HALO, LANES), F32)],
        compiler_params=_cparams("parallel", "arbitrary"),
        name="pool_mixer",
    )(zc, w_pool, pool_scale)


def _gated_merge_body(h_ref, oa_ref, ob_ref, oc_ref, od_ref, wg_ref, bg_ref, wl_ref, m_ref):
    h = h_ref[...]
    merged = None
    for b, o_ref in enumerate((oa_ref, ob_ref, oc_ref, od_ref)):
        gate = jax.nn.sigmoid(jnp.dot(h, wg_ref[b], preferred_element_type=F32) + bg_ref[b:b + 1, :])
        term = gate * jnp.dot(o_ref[...], wl_ref[b], preferred_element_type=F32)
        merged = term if merged is None else merged + term
    m_ref[...] = merged.astype(BF16)


def _gated_merge(h, outs, wg, bg, wl, tm, tn):
    T, D = h.shape
    row = lambda w_: pl.BlockSpec((tm, w_), lambda i, j: (i, 0))
    return pl.pallas_call(
        _gated_merge_body,
        grid=(T // tm, D // tn),
        in_specs=[row(D)] + [row(BRANCH_W)] * N_BRANCH + [
            pl.BlockSpec((N_BRANCH, D, tn), lambda i, j: (0, 0, j)),
            pl.BlockSpec((N_BRANCH, tn), lambda i, j: (0, j)),
            pl.BlockSpec((N_BRANCH, BRANCH_W, tn), lambda i, j: (0, 0, j))],
        out_specs=pl.BlockSpec((tm, tn), lambda i, j: (i, j)),
        out_shape=jax.ShapeDtypeStruct((T, D), BF16),
        compiler_params=_cparams("parallel", "arbitrary"),
        name="gated_merge",
    )(h, *outs, wg, bg, wl)


ROUTE_ROWS = 8


def _out_proj_body(m_ref, x_ref, wo_ref, g2_ref, wr_ref, rb_ref, xn_ref, h2_ref, rt_ref):
    xn = x_ref[...] + jnp.dot(m_ref[...], wo_ref[...], preferred_element_type=F32)
    xn_ref[...] = xn
    h2 = _rms(xn, g2_ref[...])
    h2_ref[...] = h2.astype(BF16)
    h_hi = h2.astype(BF16)
    h_lo = (h2 - h_hi.astype(F32)).astype(BF16)
    hi = jnp.dot(h_hi, wr_ref[...], preferred_element_type=F32)
    logits = hi[:, :ROUTER_W] + hi[:, ROUTER_W:] + jnp.dot(h_lo, wr_ref[:, :ROUTER_W], preferred_element_type=F32)
    rt_ref[...] = _route_tile(logits + rb_ref[...])


def _out_proj(merged, x, wo, g2, wr, route_bias, tm):
    T, D = x.shape
    row = pl.BlockSpec((tm, D), lambda i: (i, 0))
    return pl.pallas_call(
        _out_proj_body,
        grid=(T // tm,),
        in_specs=[row, row, _resident((D, D)), _resident((1, D)), _resident((D, 2 * ROUTER_W)),
                  _resident((1, ROUTER_W))],
        out_specs=[row, row, pl.BlockSpec((ROUTE_ROWS, tm), lambda i: (0, i))],
        out_shape=[jax.ShapeDtypeStruct((T, D), F32), jax.ShapeDtypeStruct((T, D), BF16),
                   jax.ShapeDtypeStruct((ROUTE_ROWS, T), F32)],
        compiler_params=_cparams("parallel"),
        name="out_proj_route",
    )(merged, x, wo, g2, wr, route_bias)


def _route_tile(lg):
    lane = lax.broadcasted_iota(jnp.int32, lg.shape, 1).astype(F32)
    neg = -jnp.inf
    row_max = lambda v: jnp.max(v, axis=-1, keepdims=True)
    first_at = lambda v, m: jnp.min(jnp.where(v == m, lane, float(LANES)), axis=-1, keepdims=True)
    g_lg = jnp.where(lane < N_GROUPS, lg, neg)
    g_max = row_max(g_lg)
    g_top = first_at(g_lg, g_max)
    pg_top = 1.0 / jnp.sum(jnp.exp(g_lg - g_max), axis=-1, keepdims=True)
    e_lo = N_GROUPS + EXP_PER_GROUP * g_top
    e_lg = jnp.where(jnp.logical_and(lane >= e_lo, lane < e_lo + EXP_PER_GROUP), lg, neg)
    picks, maxes = [], []
    for _ in range(TOP_K):
        m = row_max(e_lg)
        i = first_at(e_lg, m)
        picks.append(i)
        maxes.append(m)
        e_lg = jnp.where(lane == i, neg, e_lg)
    w = [jnp.exp(m - maxes[0]) for m in maxes]
    w_sum = functools.reduce(lambda a, b: a + b, w)
    cols = [i - N_GROUPS for i in picks] + [pg_top * wk / w_sum for wk in w]
    tile = jnp.zeros(lg.shape, F32)
    for c, v in enumerate(cols):
        tile = jnp.where(lane == c, v, tile)
    return tile.T[:ROUTE_ROWS]


def _dispatch(routed, tb):
    T = routed.shape[1]
    A = T * TOP_K
    flat_e = routed[:TOP_K].astype(jnp.int32).T.reshape(A)
    flat_w = routed[TOP_K:2 * TOP_K].T.reshape(A)
    iota = jnp.arange(A, dtype=jnp.int32)
    se, order, sw = lax.sort((flat_e, iota, flat_w), num_keys=1, is_stable=True)
    experts = jnp.arange(N_EXPERTS, dtype=jnp.int32)
    counts = jnp.sum((flat_e[:, None] == experts[None, :]).astype(jnp.int32), axis=0)
    start = jnp.cumsum(counts) - counts
    padded = (counts + tb - 1) // tb * tb
    pend = jnp.cumsum(padded)
    pstart = pend - padded
    n_blocks = -(-A // tb) + N_EXPERTS
    P = n_blocks * tb
    blk_first = jnp.arange(n_blocks, dtype=jnp.int32) * tb
    blk_exp = jnp.minimum(jnp.sum((pend[None, :] <= blk_first[:, None]).astype(jnp.int32), axis=1), N_EXPERTS - 1)
    n_valid = (pend[-1] // tb).astype(jnp.int32).reshape(1)
    slot = jnp.arange(P, dtype=jnp.int32)
    before = (pend[None, :] <= slot[:, None]).astype(jnp.int32)
    shift = jnp.sum(before * (padded - counts)[None, :], axis=1)
    in_pad = jnp.any(jnp.logical_and(slot[:, None] >= (pstart + counts)[None, :], slot[:, None] < pend[None, :]),
                     axis=1)
    valid = jnp.logical_and(jnp.logical_not(in_pad), slot < pend[-1])
    src = jnp.where(valid, slot - shift, slot % A)
    slot_tok = order[src] // TOP_K
    slot_w = jnp.where(valid, sw[src], 0.0)
    dest = iota + jnp.sum((se[:, None] == experts[None, :]).astype(jnp.int32) * (pstart - start)[None, :], axis=1)
    _, slot_of = lax.sort((order, dest), num_keys=1)
    return slot_tok, slot_w, blk_exp, n_valid, slot_of.reshape(T, TOP_K).T


def _expert_body(be_ref, nv_ref, x_ref, sw_ref, wg_ref, wu_ref, wd_ref, o_ref, wg_s, wu_s, wd_s):
    i = pl.program_id(0)

    @pl.when(jnp.logical_or(i == 0, be_ref[i] != be_ref[jnp.maximum(i - 1, 0)]))
    def _():
        wg_s[...] = wg_ref[0, 0].astype(BF16)
        wu_s[...] = wu_ref[0, 0].astype(BF16)
        wd_s[...] = wd_ref[0, 0].astype(BF16)

    @pl.when(i < nv_ref[0])
    def _():
        x = x_ref[...]
        g = jnp.dot(x, wg_s[...], preferred_element_type=F32)
        u = jnp.dot(x, wu_s[...], preferred_element_type=F32)
        a = g * jax.nn.sigmoid(g) * u
        sw = sw_ref[...]
        a = jnp.concatenate([a[:, c * LANES:(c + 1) * LANES] * sw for c in range(a.shape[1] // LANES)], axis=1)
        o_ref[...] = jnp.dot(a.astype(BF16), wd_s[...], preferred_element_type=F32).astype(o_ref.dtype)

    @pl.when(i >= nv_ref[0])
    def _():
        o_ref[...] = jnp.zeros(o_ref.shape, o_ref.dtype)


def _expert_ffn(xs, slot_w, blk_exp, n_valid, wg, wu, wd, layer, tb):
    P, D = xs.shape
    grid_spec = pltpu.PrefetchScalarGridSpec(
        num_scalar_prefetch=2,
        grid=(P // tb,),
        in_specs=[pl.BlockSpec((tb, D), lambda i, be, nv: (i, 0)),
                  pl.BlockSpec((tb, LANES), lambda i, be, nv: (i, 0)),
                  pl.BlockSpec((1, 1, D, D_EXPERT), lambda i, be, nv: (layer, be[i], 0, 0)),
                  pl.BlockSpec((1, 1, D, D_EXPERT), lambda i, be, nv: (layer, be[i], 0, 0)),
                  pl.BlockSpec((1, 1, D_EXPERT, D), lambda i, be, nv: (layer, be[i], 0, 0))],
        out_specs=pl.BlockSpec((tb, D), lambda i, be, nv: (i, 0)),
        scratch_shapes=[pltpu.VMEM((D, D_EXPERT), BF16), pltpu.VMEM((D, D_EXPERT), BF16),
                        pltpu.VMEM((D_EXPERT, D), BF16)],
    )
    return pl.pallas_call(
        _expert_body,
        grid_spec=grid_spec,
        out_shape=jax.ShapeDtypeStruct((P, D), BF16),
        compiler_params=_cparams("arbitrary"),
        name="expert_ffn",
    )(blk_exp, n_valid, xs, jnp.broadcast_to(slot_w[:, None], (P, LANES)), wg, wu, wd)


def _final_norm_body(*refs):
    x_ref, y_refs, (g_ref, o_ref) = refs[0], refs[1:-2], refs[-2:]
    o_ref[...] = _rms(_residual_sum(x_ref, y_refs), g_ref[...])


def _final_norm(x, adds, g, row0, rows, tm):
    D = x.shape[1]
    first = row0 // tm
    src = pl.BlockSpec((tm, D), lambda i: (first + i, 0))
    return pl.pallas_call(
        _final_norm_body, grid=(rows // tm,),
        in_specs=[src] * (1 + len(adds)) + [_resident((1, D))],
        out_specs=pl.BlockSpec((tm, D), lambda i: (i, 0)),
        out_shape=jax.ShapeDtypeStruct((rows, D), F32), compiler_params=_cparams("parallel"), name="final_norm",
    )(x, *adds, g)


def _rope_tables(seq):
    def angles(pos, dim):
        inv = 1.0 / (ROPE_THETA ** (jnp.arange(0, dim, 2, dtype=F32) / dim))
        ang = pos.astype(F32)[:, None] * inv[None, :]
        return jnp.cos(ang), jnp.sin(ang)
    pos = jnp.arange(seq)
    c1, s1 = angles(pos, A_ROPE)
    cr, sr = angles(pos // GRID_W, D_HEAD // 2)
    cc, sc = angles(pos % GRID_W, D_HEAD // 2)
    pad1, pad0 = jnp.ones((seq, ROPE_PAD), F32), jnp.zeros((seq, ROPE_PAD), F32)
    return (jnp.concatenate([c1, c1, pad1], 1), jnp.concatenate([-s1, s1, pad0], 1),
            jnp.concatenate([cr, cr, cc, cc], 1), jnp.concatenate([-sr, sr, -sc, sc], 1))


def _layer_weights(l, w_in, w_uq, w_ukv, w_route_group, w_route_expert):
    D = D_MODEL
    k_r_end = A_Q_LORA + A_KV_LORA + A_ROPE
    win = jnp.concatenate([w_in[l][:, :k_r_end], jnp.zeros((D, ROPE_PAD), F32), w_in[l][:, k_r_end:]], 1)
    uq = w_uq[l].reshape(A_Q_LORA, A_HEADS, A_NOPE + A_ROPE)
    uq_rope = jnp.pad(uq[:, :, A_NOPE:], ((0, 0), (0, 0), (0, ROPE_PAD)))
    wuq = jnp.concatenate([uq[:, :, :A_NOPE].reshape(A_Q_LORA, -1), uq_rope.reshape(A_Q_LORA, -1)], 1)
    ukv = w_ukv[l].reshape(A_KV_LORA, A_HEADS, A_NOPE + A_V)
    wukv = jnp.concatenate([ukv[:, :, :A_NOPE].reshape(A_KV_LORA, -1), ukv[:, :, A_NOPE:].reshape(A_KV_LORA, -1)], 1)
    wr = jnp.concatenate([w_route_group[l], w_route_expert[l],
                          jnp.zeros((D, ROUTER_W - N_GROUPS - N_EXPERTS), F32)], 1)
    wr_hi = wr.astype(BF16)
    wr_lo = (wr - wr_hi.astype(F32)).astype(BF16)
    return win.astype(BF16), wuq.astype(BF16), wukv.astype(BF16), jnp.concatenate([wr_hi, wr_lo], 1)


def kernel(x_prompt, x_sample, norm1_g, w_in, q_norm_g, kv_norm_g, w_uq, w_ukv, lam_q1, lam_k1, lam_q2, lam_k2,
           rel_bias, w_pool, pool_scale, qk_norm_q, qk_norm_k, w_lift, w_gate, b_gate, w_out, norm2_g,
           w_route_group, b_route_group, w_route_expert, b_route_expert, w_exp_gate, w_exp_up, w_exp_down,
           final_g):
    assert x_prompt.shape[1:] == x_sample.shape[1:], "both request groups must share (seq, d_model)"
    n_prompt, S, D = x_prompt.shape
    B = n_prompt + x_sample.shape[0]
    T = B * S
    t = _tiles(S, T)
    x = None
    depth = w_in.shape[0]
    tabs = _rope_tables(S)
    diag, corner, far = _diff_bias_tables(rel_bias, t["tq"])
    row2 = lambda v: v.reshape(1, -1).astype(F32)

    moe_out = ()
    for l in range(depth):
        lam_init = 0.8 - 0.6 * math.exp(-0.3 * l)
        lam = (jnp.exp(jnp.sum(lam_q1[l] * lam_k1[l])) - jnp.exp(jnp.sum(lam_q2[l] * lam_k2[l])) + lam_init)
        win, wuq, wukv, wr = _layer_weights(l, w_in, w_uq, w_ukv, w_route_group, w_route_expert)

        xs = (x_prompt.reshape(-1, D), x_sample.reshape(-1, D)) if l == 0 else (x,)
        prep = (row2(q_norm_g[l]), row2(kv_norm_g[l]), wuq, wukv, row2(qk_norm_q[l]), row2(qk_norm_k[l])) + tabs
        x, h, zb, zc, qa, ka, va, qd, kd, vd = _norm_inproj(xs, moe_out, row2(norm1_g[l]), win, prep, S, t["tm_in"])
        seq3 = lambda a: a.reshape(B, S, a.shape[-1])
        rep = D_HEADS // D_KV_HEADS
        o_a = _flash(seq3(qa), seq3(ka), seq3(va), kv_heads=A_HEADS, stack=1, q_w=A_QK_W, k_col0=0, v_col0=0,
                     tq=t["rows"], ts=t["tq"], tk=t["tk"])
        o_b = _diff_attn(seq3(zb), lam.reshape(1).astype(F32), diag, corner, far, tq=t["tq"], tk=t["tk"],
                         out_scale=1.0 - lam_init)
        o_c = _pool(seq3(zc), w_pool[l].astype(BF16), row2(pool_scale[l]), t["pool_chunk"])
        o_d = _flash(seq3(qd), seq3(kd), seq3(vd), kv_heads=D_KV_HEADS, stack=rep, q_w=D_HEAD, k_col0=0,
                     v_col0=0, tq=t["rows"] // rep, ts=t["tq"], tk=t["tk"])
        outs = [o.reshape(T, BRANCH_W) for o in (o_a, o_b, o_c, o_d)]
        merged = _gated_merge(h, outs, w_gate[l].astype(BF16), b_gate[l].astype(F32), w_lift[l].astype(BF16),
                              t["tm_merge"], t["tn"])
        route_bias = jnp.concatenate([b_route_group[l], b_route_expert[l],
                                      jnp.zeros((ROUTER_W - N_GROUPS - N_EXPERTS,), F32)]).reshape(1, ROUTER_W)
        x, h2, routed = _out_proj(merged, x, w_out[l].astype(BF16), row2(norm2_g[l]), wr, route_bias.astype(F32),
                                  t["tm"])
        slot_tok, slot_w, blk_exp, n_valid, slot_of = _dispatch(routed, t["tb"])
        ys = _expert_ffn(h2[slot_tok], slot_w, blk_exp, n_valid, w_exp_gate, w_exp_up, w_exp_down, l, t["tb"])
        moe_out = tuple(ys[slot_of[k]] for k in range(TOP_K))

    t_prompt = n_prompt * S
    y_prompt = _final_norm(x, moe_out, row2(final_g), 0, t_prompt, t["tm"]).reshape(n_prompt, S, D)
    y_sample = _final_norm(x, moe_out, row2(final_g), t_prompt, T - t_prompt, t["tm"]).reshape(B - n_prompt, S, D)
    return y_prompt, y_sample
```

```python
import functools
import math

import jax
import jax.numpy as jnp
from jax import lax
from jax.experimental import pallas as pl
from jax.experimental.pallas import tpu as pltpu

F32 = jnp.float32
BF16 = jnp.bfloat16
LOG2E = 1.4426950408889634

D_MODEL = 2048
GRID_W = 64
BRANCH_W = 512
N_BRANCH = 4
ROPE_THETA = 10000.0
EPS = 1e-6
A_HEADS, A_Q_LORA, A_KV_LORA, A_NOPE, A_ROPE, A_V = 4, 384, 128, 128, 64, 128
B_HEADS, B_QK, B_V = 4, 64, 128
REL_BUCKETS, REL_MAX_DIST = 32, 128
POOL_WINDOWS = (2, 4, 8, 16)
C_GROUPS, C_GROUP_W = 4, 128
D_HEADS, D_KV_HEADS, D_HEAD = 4, 2, 128
N_GROUPS, EXP_PER_GROUP, N_EXPERTS, TOP_K, D_EXPERT = 4, 8, 32, 2, 512

LANES = 128
ROPE_PAD = LANES - A_ROPE
ZA_W = A_Q_LORA + A_KV_LORA + LANES
ZB_W = 3 * B_HEADS * B_V
ZC_W = BRANCH_W
ZD_W = (D_HEADS + 2 * D_KV_HEADS) * D_HEAD
A_QK_W = 2 * LANES
ROUTER_W = LANES
HALO = 16
VMEM_LIMIT = 56 * 2 ** 20


def _tiles(seq, tokens):
    def pick(n, pref):
        t = min(pref, n)
        while n % t:
            t //= 2
        return t
    return dict(
        tm=pick(seq, 512),
        tm_in=pick(seq, 256),
        tq=pick(seq, 512),
        rows=pick(seq, 1024),
        tk=pick(seq, 8192),
        tm_merge=pick(tokens, 1024),
        tn=512,
        tb=512,
        pool_chunk=pick(seq, 1024),
    )


def _cparams(*sem):
    return pltpu.CompilerParams(dimension_semantics=sem, vmem_limit_bytes=VMEM_LIMIT)


def _resident(shape):
    zeros = (0,) * len(shape)
    return pl.BlockSpec(shape, lambda *_: zeros, pipeline_mode=pl.Buffered(1))


def _rms(x, g=None):
    y = x * lax.rsqrt(jnp.mean(x * x, axis=-1, keepdims=True) + EPS)
    return y if g is None else y * g


def _rope_lanes(x, c, s):
    lane = lax.broadcasted_iota(jnp.int32, x.shape, 1)
    first_half = (lane & 32) == 0
    partner = jnp.where(first_half, pltpu.roll(x, 96, 1), pltpu.roll(x, 32, 1))
    return x * c + partner * s


def _residual_sum(x_ref, y_refs):
    x = x_ref[...]
    for y_ref in y_refs:
        x = x + y_ref[...].astype(F32)
    return x


N_PREP_PARAMS = 10
PREP_OUT_W = (A_HEADS * A_QK_W, A_HEADS * A_QK_W, A_HEADS * A_V, D_HEADS * D_HEAD, D_KV_HEADS * D_HEAD,
              D_KV_HEADS * D_HEAD)


def _norm_inproj_body(*refs, n_x, n_add, first_rows):
    x_refs, y_refs = refs[:n_x], refs[n_x:n_x + n_add]
    g_ref, w_ref = refs[n_x + n_add:n_x + n_add + 2]
    prep_refs = refs[n_x + n_add + 2:n_x + n_add + 2 + N_PREP_PARAMS]
    outs = refs[n_x + n_add + 2 + N_PREP_PARAMS:]
    if n_x == 2:
        x = jnp.where(pl.program_id(0) < first_rows, x_refs[0][...], x_refs[1][...])
    else:
        x = _residual_sum(x_refs[0], y_refs)
    if n_x == 2 or n_add:
        outs[0][...] = x
        outs = outs[1:]
    h_ref, zb_ref, zc_ref = outs[:3]
    xg = x * g_ref[...]
    r = lax.rsqrt(jnp.mean(x * x, axis=-1, keepdims=True) + EPS)
    xg16 = xg.astype(BF16)
    proj = lambda lo, w_: jnp.dot(xg16, w_ref[:, lo:lo + w_], preferred_element_type=F32) * r
    h_ref[...] = (xg * r).astype(BF16)
    za, zd = proj(0, ZA_W), proj(ZA_W + ZB_W + ZC_W, ZD_W)
    _mixer_prep(za, zd, prep_refs, outs[3:])
    zb_ref[...] = proj(ZA_W, ZB_W).astype(BF16)
    zc_ref[...] = proj(ZA_W + ZB_W, ZC_W).astype(BF16)


def _norm_inproj(xs, adds, g, w, prep, seq, tm):
    D = xs[0].shape[1]
    T = sum(x.shape[0] for x in xs)
    per_seq = seq // tm
    row = lambda w_: pl.BlockSpec((tm, w_), lambda i: (i, 0))
    if len(xs) == 2:
        na = xs[0].shape[0] // tm
        x_specs = [pl.BlockSpec((tm, D), lambda i: (jnp.minimum(i, na - 1), 0)),
                   pl.BlockSpec((tm, D), lambda i: (jnp.maximum(i - na, 0), 0))]
    else:
        na, x_specs = None, [row(D)]
    new_x = len(xs) == 2 or adds is not None
    sum_spec, sum_shape = ([row(D)], [jax.ShapeDtypeStruct((T, D), F32)]) if new_x else ([], [])
    tab = pl.BlockSpec((tm, LANES), lambda i: (i % per_seq, 0))
    out_w = (D, ZB_W, ZC_W) + PREP_OUT_W
    add_specs, add_args = _stacked_adds(adds, T, D, tm, 0)
    outs = pl.pallas_call(
        functools.partial(_norm_inproj_body, n_x=len(xs), n_add=len(add_args), first_rows=na),
        grid=(T // tm,),
        in_specs=x_specs + add_specs + [_resident((1, D)), _resident(w.shape)]
        + [_resident(p.shape) for p in prep[:N_PREP_PARAMS - 4]] + [tab] * 4,
        out_specs=sum_spec + [row(w_) for w_ in out_w],
        out_shape=sum_shape + [jax.ShapeDtypeStruct((T, w_), BF16) for w_ in out_w],
        compiler_params=_cparams("parallel"),
        name="norm_inproj",
    )(*xs, *add_args, g, w, *prep)
    return list(outs) if new_x else [xs[0]] + list(outs)


def _stacked_adds(adds, T, D, tm, first):
    if adds is None:
        return [], []
    n = adds.shape[0] // T
    specs = [pl.BlockSpec((tm, D), lambda i, k=k: (first + i + k * (T // tm), 0)) for k in range(n)]
    return specs, [adds] * n


def _mixer_prep(za, zd, prep_refs, out_refs):
    gq_ref, gkv_ref, wuq_ref, wukv_ref, gdq_ref, gdk_ref, ca_ref, sa_ref, cd_ref, sd_ref = prep_refs
    qa_ref, ka_ref, va_ref, qd_ref, kd_ref, vd_ref = out_refs
    a_scale = (A_NOPE + A_ROPE) ** -0.5 * LOG2E
    d_scale = D_HEAD ** -0.5 * LOG2E
    ca, sa, cd, sd = ca_ref[...], sa_ref[...], cd_ref[...], sd_ref[...]
    cq = _rms(za[:, :A_Q_LORA], gq_ref[...]).astype(BF16)
    q = jnp.dot(cq, wuq_ref[...], preferred_element_type=F32)
    ckv = _rms(za[:, A_Q_LORA:A_Q_LORA + A_KV_LORA], gkv_ref[...]).astype(BF16)
    kv = jnp.dot(ckv, wukv_ref[...], preferred_element_type=F32)
    k_rope = _rope_lanes(za[:, A_Q_LORA + A_KV_LORA:], ca, sa).astype(BF16)
    nope_w = A_HEADS * A_NOPE
    for h in range(A_HEADS):
        lo = h * A_QK_W
        qa_ref[:, lo:lo + LANES] = (q[:, h * LANES:(h + 1) * LANES] * a_scale).astype(BF16)
        q_rope = _rope_lanes(q[:, nope_w + h * LANES:nope_w + (h + 1) * LANES], ca, sa)
        qa_ref[:, lo + LANES:lo + 2 * LANES] = (q_rope * a_scale).astype(BF16)
        ka_ref[:, lo:lo + LANES] = kv[:, h * LANES:(h + 1) * LANES].astype(BF16)
        ka_ref[:, lo + LANES:lo + 2 * LANES] = k_rope
    va_ref[...] = kv[:, nope_w:].astype(BF16)
    for h in range(D_HEADS):
        xh = _rms(zd[:, h * D_HEAD:(h + 1) * D_HEAD], gdq_ref[...])
        qd_ref[:, h * D_HEAD:(h + 1) * D_HEAD] = (_rope_lanes(xh, cd, sd) * d_scale).astype(BF16)
    k_off = D_HEADS * D_HEAD
    for h in range(D_KV_HEADS):
        xh = _rms(zd[:, k_off + h * D_HEAD:k_off + (h + 1) * D_HEAD], gdk_ref[...])
        kd_ref[:, h * D_HEAD:(h + 1) * D_HEAD] = _rope_lanes(xh, cd, sd).astype(BF16)
    vd_ref[...] = zd[:, k_off + D_KV_HEADS * D_HEAD:].astype(BF16)


def _qk(q, k):
    return lax.dot_general(q, k, (((1,), (1,)), ((), ())), preferred_element_type=F32)


def _lane_fold(fn, acc, x):
    for c in range(x.shape[1] // LANES):
        acc = fn(acc, x[:, c * LANES:(c + 1) * LANES])
    return acc


def _lane_blocks(x):
    return [x[:, i * LANES:(i + 1) * LANES] for i in range(x.shape[1] // LANES)]


def _score_pass(q, k_ref, s_ref, m_ref, tk, n_steps, sub_max=None):
    tq, ts = s_ref.shape[1:]
    sub = tk // ts
    m_ref[...] = jnp.full((tq, LANES), -jnp.inf, F32)

    def step(j, _):
        ks = pl.multiple_of(j * tk, tk)
        s = _qk(q, k_ref[0, pl.ds(ks, tk), :])
        mrun = None
        for c in range(sub):
            jj = j * sub + c
            sc = s[:, c * ts:(c + 1) * ts]
            s_ref[jj] = sc
            blocks = _lane_blocks(sc)
            m_sub = functools.reduce(jnp.maximum, blocks) if sub_max is None else sub_max(jj, blocks)
            mrun = m_sub if mrun is None else jnp.maximum(mrun, m_sub)
        m_ref[...] = jnp.maximum(m_ref[...], mrun)
        return 0

    lax.fori_loop(0, n_steps, step, 0)
    return m_ref[...]


def _value_pass(v_ref, s_ref, a_ref, tk, n_steps, m_of):
    tq, ts = s_ref.shape[1:]
    sub = tk // ts
    dv = v_ref.shape[2]
    ones = jnp.ones((tk, LANES), BF16)
    a_ref[...] = jnp.zeros(a_ref.shape, F32)

    def step(j, _):
        ks = pl.multiple_of(j * tk, tk)
        p = []
        for c in range(sub):
            jj = j * sub + c
            m = m_of(jj)
            s = s_ref[jj]
            p += [jnp.exp2((s[:, i * LANES:(i + 1) * LANES] - m).astype(BF16)) for i in range(ts // LANES)]
        v_ext = jnp.concatenate([v_ref[0, pl.ds(ks, tk), :], ones], axis=1)
        a_ref[...] += jnp.dot(jnp.concatenate(p, axis=1), v_ext, preferred_element_type=F32)
        return 0

    lax.fori_loop(0, n_steps, step, 0)
    return a_ref[:, :dv] / a_ref[:, dv:]


def _row_max(mrun):
    return jnp.broadcast_to(jnp.max(mrun, axis=-1, keepdims=True), mrun.shape)


def _attn_scratch(n_sub, rows, ts):
    return [pltpu.VMEM((n_sub, rows, ts), F32), pltpu.VMEM((rows, LANES), F32), pltpu.VMEM((rows, 2 * LANES), F32)]


def _flash_body(n_ref, q_ref, k_ref, v_ref, o_ref, s_ref, m_ref, a_ref, *, tk, stack):
    tq = q_ref.shape[1]
    q_w = q_ref.shape[2] // stack
    q = jnp.concatenate([q_ref[0, :, c * q_w:(c + 1) * q_w] for c in range(stack)], axis=0)
    m = _row_max(_score_pass(q, k_ref, s_ref, m_ref, tk, n_ref[0]))
    o = _value_pass(v_ref, s_ref, a_ref, tk, n_ref[0], lambda jj: m).astype(o_ref.dtype)
    o_ref[0] = jnp.concatenate([o[c * tq:(c + 1) * tq] for c in range(stack)], axis=1)


def _flash(q, k, v, *, kv_heads, stack, q_w, k_col0, v_col0, tq, ts, tk):
    B, S, _ = q.shape
    dv = LANES
    return pl.pallas_call(
        functools.partial(_flash_body, tk=tk, stack=stack),
        grid=(B, kv_heads, S // tq),
        in_specs=[pl.BlockSpec(memory_space=pltpu.SMEM),
                  pl.BlockSpec((1, tq, stack * q_w), lambda b, h, i: (b, i, h)),
                  pl.BlockSpec((1, S, q_w), lambda b, h, i: (b, 0, k_col0 + h)),
                  pl.BlockSpec((1, S, dv), lambda b, h, i: (b, 0, v_col0 + h))],
        out_specs=pl.BlockSpec((1, tq, stack * dv), lambda b, h, i: (b, i, h)),
        out_shape=jax.ShapeDtypeStruct((B, S, kv_heads * stack * dv), BF16),
        scratch_shapes=_attn_scratch(S // ts, stack * tq, ts),
        compiler_params=_cparams("parallel", "parallel", "arbitrary"),
        name="flash_attn",
    )(jnp.full((1,), S // tk, jnp.int32), q, k, v)


def _diff_body(n_ref, lam_ref, cb_ref, q_ref, k_ref, v_ref, diag_ref, corner_ref, o_ref, s_ref, m_ref, a_ref, *,
               tk, out_scale):
    h, qi = pl.program_id(1), pl.program_id(2)
    n_sub = s_ref.shape[0]
    tq = q_ref.shape[1]
    nd = REL_MAX_DIST
    qf = q_ref[0].astype(F32) * (B_QK ** -0.5 * LOG2E)
    lane = lax.broadcasted_iota(jnp.int32, qf.shape, 1)
    c_lo, c_hi = cb_ref[0, h], cb_ref[1, h]
    side_shift = lambda jj: jnp.where(jj < qi, c_lo, jnp.where(jj > qi, c_hi, 0.0))
    tile_max = lambda s: _lane_fold(jnp.maximum, s[:, :LANES], s[:, LANES:])

    q = jnp.concatenate([jnp.where(lane < B_QK, qf, 0.0), jnp.where(lane >= B_QK, qf, 0.0)], axis=0).astype(BF16)
    row = lax.broadcasted_iota(jnp.int32, (2 * tq, LANES), 0) % tq
    pen_lo = jnp.where(row < nd, -jnp.inf, 0.0)
    pen_hi = jnp.where(row >= tq - nd, -jnp.inf, 0.0)
    zero = jnp.zeros_like(pen_lo)

    def sub_max(jj, blocks):
        blocks = list(blocks)
        blocks[-1] = blocks[-1] + jnp.where(jj == qi - 1, pen_lo, zero)
        blocks[0] = blocks[0] + jnp.where(jj == qi + 1, pen_hi, zero)
        shift = jnp.where(jj < qi, c_lo, jnp.where(jj > qi, c_hi, -jnp.inf))
        return functools.reduce(jnp.maximum, blocks) + shift

    mrun = _score_pass(q, k_ref, s_ref, m_ref, tk, n_ref[0], sub_max=sub_max)

    diag = diag_ref[0]
    s = s_ref[qi] + jnp.concatenate([diag, diag], axis=0)
    s_ref[qi] = s
    mrun = jnp.maximum(mrun, tile_max(s))

    def neighbour(jj, rows0, cols0, corner, c):
        def fix(mrun):
            for r in (rows0, tq + rows0):
                fixed = s_ref[jj, r:r + nd, cols0:cols0 + nd] + corner
                s_ref[jj, r:r + nd, cols0:cols0 + nd] = fixed
                pieces = [mrun[:r], jnp.maximum(mrun[r:r + nd], fixed + c), mrun[r + nd:]]
                mrun = jnp.concatenate([p for p in pieces if p.shape[0]], axis=0)
            return mrun
        return fix

    mrun = lax.cond(qi >= 1, neighbour(jnp.maximum(qi - 1, 0), 0, tq - nd, corner_ref[0, 0], c_lo),
                    lambda m_: m_, mrun)
    mrun = lax.cond(qi + 1 < n_sub, neighbour(jnp.minimum(qi + 1, n_sub - 1), tq - nd, 0, corner_ref[0, 1], c_hi),
                    lambda m_: m_, mrun)
    m = _row_max(mrun)
    o = _value_pass(v_ref, s_ref, a_ref, tk, n_ref[0], lambda jj: m - side_shift(jj))
    o = o[:tq] - lam_ref[0] * o[tq:]
    o_ref[0] = (_rms(o) * out_scale).astype(o_ref.dtype)


def _rel_bucket(rel):
    half = REL_BUCKETS // 2
    max_exact = half // 2
    ret = (rel > 0).astype(jnp.int32) * half
    n = jnp.abs(rel)
    large = max_exact + (jnp.log(jnp.maximum(n, 1).astype(F32) / max_exact)
                         / math.log(REL_MAX_DIST / max_exact) * (half - max_exact)).astype(jnp.int32)
    large = jnp.minimum(large, half - 1)
    return ret + jnp.where(n < max_exact, n, large)


def _diff_bias_tables(rel_bias, tq):
    nd = REL_MAX_DIST
    rb = rel_bias.astype(F32) * LOG2E
    far = jnp.stack([rb[REL_BUCKETS // 2 - 1], rb[REL_BUCKETS - 1]])

    def table(rel):
        bucket = _rel_bucket(rel)
        out = jnp.zeros((B_HEADS,) + rel.shape, F32)
        for b in range(REL_BUCKETS):
            out = jnp.where(bucket[None] == b, rb[b].reshape((B_HEADS,) + (1,) * rel.ndim), out)
        return out

    a, e = jnp.arange(tq), jnp.arange(nd)
    diag = table(a[None, :] - a[:, None])
    lo = table((tq - nd + e[None, :]) - tq - e[:, None]) - far[0][:, None, None]
    hi = table(e[None, :] + tq - (tq - nd + e[:, None])) - far[1][:, None, None]
    return diag, jnp.stack([lo, hi], axis=1), far


def _diff_attn(zb, lam, diag, corner, far, *, tq, tk, out_scale):
    B, S, _ = zb.shape
    nd = REL_MAX_DIST
    assert tq >= nd, "far key chunks must lie beyond the last distinct relative bucket"
    assert nd == LANES, "the neighbours' near corner is handled as one lane block"
    H = B_HEADS
    smem = pl.BlockSpec(memory_space=pltpu.SMEM)
    return pl.pallas_call(
        functools.partial(_diff_body, tk=tk, out_scale=out_scale),
        grid=(B, H, S // tq),
        in_specs=[smem, smem, smem,
                  pl.BlockSpec((1, tq, B_V), lambda b, h, i: (b, i, h)),
                  pl.BlockSpec((1, S, B_V), lambda b, h, i: (b, 0, H + h)),
                  pl.BlockSpec((1, S, B_V), lambda b, h, i: (b, 0, 2 * H + h)),
                  pl.BlockSpec((1, tq, tq), lambda b, h, i: (h, 0, 0)),
                  pl.BlockSpec((1, 2, nd, nd), lambda b, h, i: (h, 0, 0, 0))],
        out_specs=pl.BlockSpec((1, tq, B_V), lambda b, h, i: (b, i, h)),
        out_shape=jax.ShapeDtypeStruct((B, S, H * B_V), BF16),
        scratch_shapes=_attn_scratch(S // tq, 2 * tq, tq),
        compiler_params=_cparams("parallel", "parallel", "arbitrary"),
        name="diff_attn",
    )(jnp.full((1,), S // tk, jnp.int32), lam, far, zb, zb, zb, diag, corner)


def _pool_body(u_ref, w_ref, sc_ref, o_ref, pad_ref, *, chunk):
    g = pl.program_id(1)
    S = u_ref.shape[1]
    pad_ref[0:HALO, :] = jnp.zeros((HALO, LANES), F32)
    pad_ref[HALO + S:, :] = jnp.zeros((HALO, LANES), F32)
    pad_ref[HALO:HALO + S, :] = u_ref[0].astype(F32)
    w_mat, sc = w_ref[0], sc_ref[...]

    def pooled(win):
        def body(c, _):
            r0 = pl.multiple_of(c * chunk, chunk)
            tot = pad_ref[pl.ds(r0 + HALO - win // 2, chunk), :]
            for j in range(1 - win // 2, win // 2):
                tot = tot + pad_ref[pl.ds(r0 + HALO + j, chunk), :]
            t = r0 + lax.broadcasted_iota(jnp.int32, (chunk, 1), 0)
            cnt = jnp.clip(t - win // 2 + win, 0, S) - jnp.clip(t - win // 2, 0, S)
            d = tot / cnt.astype(F32) - pad_ref[pl.ds(r0 + HALO, chunk), :]
            y = jnp.dot(d.astype(BF16), w_mat, preferred_element_type=F32) * sc
            o_ref[0, pl.ds(r0, chunk), :] = y.astype(o_ref.dtype)
            return 0
        lax.fori_loop(0, S // chunk, body, 0)

    for gi, win in enumerate(POOL_WINDOWS):
        pl.when(g == gi)(functools.partial(pooled, win))


def _pool(zc, w_pool, pool_scale, chunk):
    B, S, _ = zc.shape
    blk = pl.BlockSpec((1, S, LANES), lambda b, g: (b, 0, g))
    return pl.pallas_call(
        functools.partial(_pool_body, chunk=chunk),
        grid=(B, C_GROUPS),
        in_specs=[blk, pl.BlockSpec((1, C_GROUP_W, C_GROUP_W), lambda b, g: (g, 0, 0)),
                  pl.BlockSpec((1, LANES), lambda b, g: (0, g))],
        out_specs=blk,
        out_shape=jax.ShapeDtypeStruct((B, S, BRANCH_W), BF16),
        scratch_shapes=[pltpu.VMEM((S + 2 * HALO, LANES), F32)],
        compiler_params=_cparams("parallel", "arbitrary"),
        name="pool_mixer",
    )(zc, w_pool, pool_scale)


def _gated_merge_body(h_ref, oa_ref, ob_ref, oc_ref, od_ref, wg_ref, bg_ref, wl_ref, m_ref):
    h = h_ref[...]
    merged = None
    for b, o_ref in enumerate((oa_ref, ob_ref, oc_ref, od_ref)):
        gate = jax.nn.sigmoid(jnp.dot(h, wg_ref[b], preferred_element_type=F32) + bg_ref[b:b + 1, :])
        term = gate * jnp.dot(o_ref[...], wl_ref[b], preferred_element_type=F32)
        merged = term if merged is None else merged + term
    m_ref[...] = merged.astype(BF16)


def _gated_merge(h, outs, wg, bg, wl, tm, tn):
    T, D = h.shape
    row = lambda w_: pl.BlockSpec((tm, w_), lambda i, j: (i, 0))
    return pl.pallas_call(
        _gated_merge_body,
        grid=(T // tm, D // tn),
        in_specs=[row(D)] + [row(BRANCH_W)] * N_BRANCH + [
            pl.BlockSpec((N_BRANCH, D, tn), lambda i, j: (0, 0, j)),
            pl.BlockSpec((N_BRANCH, tn), lambda i, j: (0, j)),
            pl.BlockSpec((N_BRANCH, BRANCH_W, tn), lambda i, j: (0, 0, j))],
        out_specs=pl.BlockSpec((tm, tn), lambda i, j: (i, j)),
        out_shape=jax.ShapeDtypeStruct((T, D), BF16),
        compiler_params=_cparams("parallel", "arbitrary"),
        name="gated_merge",
    )(h, *outs, wg, bg, wl)


ROUTE_ROWS = 8


def _out_proj_body(m_ref, x_ref, wo_ref, g2_ref, wr_ref, rb_ref, xn_ref, h2_ref, rt_ref):
    xn = x_ref[...] + jnp.dot(m_ref[...], wo_ref[...], preferred_element_type=F32)
    xn_ref[...] = xn
    h2 = _rms(xn, g2_ref[...])
    h2_ref[...] = h2.astype(BF16)
    h_hi = h2.astype(BF16)
    h_lo = (h2 - h_hi.astype(F32)).astype(BF16)
    hi = jnp.dot(h_hi, wr_ref[...], preferred_element_type=F32)
    logits = hi[:, :ROUTER_W] + hi[:, ROUTER_W:] + jnp.dot(h_lo, wr_ref[:, :ROUTER_W], preferred_element_type=F32)
    rt_ref[...] = _route_tile(logits + rb_ref[...])


def _out_proj(merged, x, wo, g2, wr, route_bias, tm):
    T, D = x.shape
    row = pl.BlockSpec((tm, D), lambda i: (i, 0))
    return pl.pallas_call(
        _out_proj_body,
        grid=(T // tm,),
        in_specs=[row, row, _resident((D, D)), _resident((1, D)), _resident((D, 2 * ROUTER_W)),
                  _resident((1, ROUTER_W))],
        out_specs=[row, row, pl.BlockSpec((ROUTE_ROWS, tm), lambda i: (0, i))],
        out_shape=[jax.ShapeDtypeStruct((T, D), F32), jax.ShapeDtypeStruct((T, D), BF16),
                   jax.ShapeDtypeStruct((ROUTE_ROWS, T), F32)],
        compiler_params=_cparams("parallel"),
        name="out_proj_route",
    )(merged, x, wo, g2, wr, route_bias)


def _route_tile(lg):
    lane = lax.broadcasted_iota(jnp.int32, lg.shape, 1).astype(F32)
    neg = -jnp.inf
    row_max = lambda v: jnp.max(v, axis=-1, keepdims=True)
    first_at = lambda v, m: jnp.min(jnp.where(v == m, lane, float(LANES)), axis=-1, keepdims=True)
    g_lg = jnp.where(lane < N_GROUPS, lg, neg)
    g_max = row_max(g_lg)
    g_top = first_at(g_lg, g_max)
    pg_top = 1.0 / jnp.sum(jnp.exp(g_lg - g_max), axis=-1, keepdims=True)
    e_lo = N_GROUPS + EXP_PER_GROUP * g_top
    e_lg = jnp.where(jnp.logical_and(lane >= e_lo, lane < e_lo + EXP_PER_GROUP), lg, neg)
    picks, maxes = [], []
    for _ in range(TOP_K):
        m = row_max(e_lg)
        i = first_at(e_lg, m)
        picks.append(i)
        maxes.append(m)
        e_lg = jnp.where(lane == i, neg, e_lg)
    w = [jnp.exp(m - maxes[0]) for m in maxes]
    w_sum = functools.reduce(lambda a, b: a + b, w)
    cols = [i - N_GROUPS for i in picks] + [pg_top * wk / w_sum for wk in w]
    tile = jnp.zeros(lg.shape, F32)
    for c, v in enumerate(cols):
        tile = jnp.where(lane == c, v, tile)
    return tile.T[:ROUTE_ROWS]


def _dispatch(routed, tb):
    T = routed.shape[1]
    A = T * TOP_K
    flat_e = routed[:TOP_K].astype(jnp.int32).T.reshape(A)
    flat_w = routed[TOP_K:2 * TOP_K].T.reshape(A)
    iota = jnp.arange(A, dtype=jnp.int32)
    se, order, sw = lax.sort((flat_e, iota, flat_w), num_keys=1, is_stable=True)
    experts = jnp.arange(N_EXPERTS, dtype=jnp.int32)
    counts = jnp.sum((flat_e[:, None] == experts[None, :]).astype(jnp.int32), axis=0)
    start = jnp.cumsum(counts) - counts
    padded = (counts + tb - 1) // tb * tb
    pend = jnp.cumsum(padded)
    pstart = pend - padded
    n_blocks = -(-A // tb) + N_EXPERTS
    P = n_blocks * tb
    blk_first = jnp.arange(n_blocks, dtype=jnp.int32) * tb
    blk_exp = jnp.minimum(jnp.sum((pend[None, :] <= blk_first[:, None]).astype(jnp.int32), axis=1), N_EXPERTS - 1)
    n_valid = (pend[-1] // tb).astype(jnp.int32).reshape(1)
    slot = jnp.arange(P, dtype=jnp.int32)
    before = (pend[None, :] <= slot[:, None]).astype(jnp.int32)
    shift = jnp.sum(before * (padded - counts)[None, :], axis=1)
    in_pad = jnp.any(jnp.logical_and(slot[:, None] >= (pstart + counts)[None, :], slot[:, None] < pend[None, :]),
                     axis=1)
    valid = jnp.logical_and(jnp.logical_not(in_pad), slot < pend[-1])
    src = jnp.where(valid, slot - shift, slot % A)
    slot_tok = order[src] // TOP_K
    slot_w = jnp.where(valid, sw[src], 0.0)
    dest = iota + jnp.sum((se[:, None] == experts[None, :]).astype(jnp.int32) * (pstart - start)[None, :], axis=1)
    _, slot_of = lax.sort((order, dest), num_keys=1)
    return slot_tok, slot_w, blk_exp, n_valid, slot_of.reshape(T, TOP_K).T


def _expert_body(be_ref, nv_ref, x_ref, sw_ref, wg_ref, wu_ref, wd_ref, o_ref, wg_s, wu_s, wd_s):
    i = pl.program_id(0)

    @pl.when(jnp.logical_or(i == 0, be_ref[i] != be_ref[jnp.maximum(i - 1, 0)]))
    def _():
        wg_s[...] = wg_ref[0, 0].astype(BF16)
        wu_s[...] = wu_ref[0, 0].astype(BF16)
        wd_s[...] = wd_ref[0, 0].astype(BF16)

    @pl.when(i < nv_ref[0])
    def _():
        x = x_ref[...]
        g = jnp.dot(x, wg_s[...], preferred_element_type=F32)
        u = jnp.dot(x, wu_s[...], preferred_element_type=F32)
        a = g * jax.nn.sigmoid(g) * u
        sw = sw_ref[...]
        a = jnp.concatenate([a[:, c * LANES:(c + 1) * LANES] * sw for c in range(a.shape[1] // LANES)], axis=1)
        o_ref[...] = jnp.dot(a.astype(BF16), wd_s[...], preferred_element_type=F32).astype(o_ref.dtype)

    @pl.when(i >= nv_ref[0])
    def _():
        o_ref[...] = jnp.zeros(o_ref.shape, o_ref.dtype)


def _expert_ffn(xs, slot_w, blk_exp, n_valid, wg, wu, wd, layer, tb):
    P, D = xs.shape
    grid_spec = pltpu.PrefetchScalarGridSpec(
        num_scalar_prefetch=2,
        grid=(P // tb,),
        in_specs=[pl.BlockSpec((tb, D), lambda i, be, nv: (i, 0)),
                  pl.BlockSpec((tb, LANES), lambda i, be, nv: (i, 0)),
                  pl.BlockSpec((1, 1, D, D_EXPERT), lambda i, be, nv: (layer, be[i], 0, 0)),
                  pl.BlockSpec((1, 1, D, D_EXPERT), lambda i, be, nv: (layer, be[i], 0, 0)),
                  pl.BlockSpec((1, 1, D_EXPERT, D), lambda i, be, nv: (layer, be[i], 0, 0))],
        out_specs=pl.BlockSpec((tb, D), lambda i, be, nv: (i, 0)),
        scratch_shapes=[pltpu.VMEM((D, D_EXPERT), BF16), pltpu.VMEM((D, D_EXPERT), BF16),
                        pltpu.VMEM((D_EXPERT, D), BF16)],
    )
    return pl.pallas_call(
        _expert_body,
        grid_spec=grid_spec,
        out_shape=jax.ShapeDtypeStruct((P, D), BF16),
        compiler_params=_cparams("arbitrary"),
        name="expert_ffn",
    )(blk_exp, n_valid, xs, jnp.broadcast_to(slot_w[:, None], (P, LANES)), wg, wu, wd)


def _final_norm_body(*refs):
    x_ref, y_refs, (g_ref, o_ref) = refs[0], refs[1:-2], refs[-2:]
    o_ref[...] = _rms(_residual_sum(x_ref, y_refs), g_ref[...])


def _final_norm(x, adds, g, row0, rows, tm):
    T, D = x.shape
    first = row0 // tm
    add_specs, add_args = _stacked_adds(adds, T, D, tm, first)
    return pl.pallas_call(
        _final_norm_body, grid=(rows // tm,),
        in_specs=[pl.BlockSpec((tm, D), lambda i: (first + i, 0))] + add_specs + [_resident((1, D))],
        out_specs=pl.BlockSpec((tm, D), lambda i: (i, 0)),
        out_shape=jax.ShapeDtypeStruct((rows, D), F32), compiler_params=_cparams("parallel"), name="final_norm",
    )(x, *add_args, g)


def _rope_tables(seq):
    def angles(pos, dim):
        inv = 1.0 / (ROPE_THETA ** (jnp.arange(0, dim, 2, dtype=F32) / dim))
        ang = pos.astype(F32)[:, None] * inv[None, :]
        return jnp.cos(ang), jnp.sin(ang)
    pos = jnp.arange(seq)
    c1, s1 = angles(pos, A_ROPE)
    cr, sr = angles(pos // GRID_W, D_HEAD // 2)
    cc, sc = angles(pos % GRID_W, D_HEAD // 2)
    pad1, pad0 = jnp.ones((seq, ROPE_PAD), F32), jnp.zeros((seq, ROPE_PAD), F32)
    return (jnp.concatenate([c1, c1, pad1], 1), jnp.concatenate([-s1, s1, pad0], 1),
            jnp.concatenate([cr, cr, cc, cc], 1), jnp.concatenate([-sr, sr, -sc, sc], 1))


def _layer_weights(l, w_in, w_uq, w_ukv, w_route_group, w_route_expert):
    D = D_MODEL
    k_r_end = A_Q_LORA + A_KV_LORA + A_ROPE
    win = jnp.concatenate([w_in[l][:, :k_r_end], jnp.zeros((D, ROPE_PAD), F32), w_in[l][:, k_r_end:]], 1)
    uq = w_uq[l].reshape(A_Q_LORA, A_HEADS, A_NOPE + A_ROPE)
    uq_rope = jnp.pad(uq[:, :, A_NOPE:], ((0, 0), (0, 0), (0, ROPE_PAD)))
    wuq = jnp.concatenate([uq[:, :, :A_NOPE].reshape(A_Q_LORA, -1), uq_rope.reshape(A_Q_LORA, -1)], 1)
    ukv = w_ukv[l].reshape(A_KV_LORA, A_HEADS, A_NOPE + A_V)
    wukv = jnp.concatenate([ukv[:, :, :A_NOPE].reshape(A_KV_LORA, -1), ukv[:, :, A_NOPE:].reshape(A_KV_LORA, -1)], 1)
    wr = jnp.concatenate([w_route_group[l], w_route_expert[l],
                          jnp.zeros((D, ROUTER_W - N_GROUPS - N_EXPERTS), F32)], 1)
    wr_hi = wr.astype(BF16)
    wr_lo = (wr - wr_hi.astype(F32)).astype(BF16)
    return win.astype(BF16), wuq.astype(BF16), wukv.astype(BF16), jnp.concatenate([wr_hi, wr_lo], 1)


def kernel(x_prompt, x_sample, norm1_g, w_in, q_norm_g, kv_norm_g, w_uq, w_ukv, lam_q1, lam_k1, lam_q2, lam_k2,
           rel_bias, w_pool, pool_scale, qk_norm_q, qk_norm_k, w_lift, w_gate, b_gate, w_out, norm2_g,
           w_route_group, b_route_group, w_route_expert, b_route_expert, w_exp_gate, w_exp_up, w_exp_down,
           final_g):
    assert x_prompt.shape[1:] == x_sample.shape[1:], "both request groups must share (seq, d_model)"
    n_prompt, S, D = x_prompt.shape
    B = n_prompt + x_sample.shape[0]
    T = B * S
    t = _tiles(S, T)
    x = None
    depth = w_in.shape[0]
    tabs = _rope_tables(S)
    diag, corner, far = _diff_bias_tables(rel_bias, t["tq"])
    row2 = lambda v: v.reshape(1, -1).astype(F32)

    moe_out = None
    for l in range(depth):
        lam_init = 0.8 - 0.6 * math.exp(-0.3 * l)
        lam = (jnp.exp(jnp.sum(lam_q1[l] * lam_k1[l])) - jnp.exp(jnp.sum(lam_q2[l] * lam_k2[l])) + lam_init)
        win, wuq, wukv, wr = _layer_weights(l, w_in, w_uq, w_ukv, w_route_group, w_route_expert)

        xs = (x_prompt.reshape(-1, D), x_sample.reshape(-1, D)) if l == 0 else (x,)
        prep = (row2(q_norm_g[l]), row2(kv_norm_g[l]), wuq, wukv, row2(qk_norm_q[l]), row2(qk_norm_k[l])) + tabs
        x, h, zb, zc, qa, ka, va, qd, kd, vd = _norm_inproj(xs, moe_out, row2(norm1_g[l]), win, prep, S, t["tm_in"])
        seq3 = lambda a: a.reshape(B, S, a.shape[-1])
        rep = D_HEADS // D_KV_HEADS
        o_a = _flash(seq3(qa), seq3(ka), seq3(va), kv_heads=A_HEADS, stack=1, q_w=A_QK_W, k_col0=0, v_col0=0,
                     tq=t["rows"], ts=t["tq"], tk=t["tk"])
        o_b = _diff_attn(seq3(zb), lam.reshape(1).astype(F32), diag, corner, far, tq=t["tq"], tk=t["tk"],
                         out_scale=1.0 - lam_init)
        o_c = _pool(seq3(zc), w_pool[l].astype(BF16), row2(pool_scale[l]), t["pool_chunk"])
        o_d = _flash(seq3(qd), seq3(kd), seq3(vd), kv_heads=D_KV_HEADS, stack=rep, q_w=D_HEAD, k_col0=0,
                     v_col0=0, tq=t["rows"] // rep, ts=t["tq"], tk=t["tk"])
        outs = [o.reshape(T, BRANCH_W) for o in (o_a, o_b, o_c, o_d)]
        merged = _gated_merge(h, outs, w_gate[l].astype(BF16), b_gate[l].astype(F32), w_lift[l].astype(BF16),
                              t["tm_merge"], t["tn"])
        route_bias = jnp.concatenate([b_route_group[l], b_route_expert[l],
                                      jnp.zeros((ROUTER_W - N_GROUPS - N_EXPERTS,), F32)]).reshape(1, ROUTER_W)
        x, h2, routed = _out_proj(merged, x, w_out[l].astype(BF16), row2(norm2_g[l]), wr, route_bias.astype(F32),
                                  t["tm"])
        slot_tok, slot_w, blk_exp, n_valid, slot_of = _dispatch(routed, t["tb"])
        ys = _expert_ffn(h2[slot_tok], slot_w, blk_exp, n_valid, w_exp_gate, w_exp_up, w_exp_down, l, t["tb"])
        moe_out = ys[slot_of.reshape(-1)]

    t_prompt = n_prompt * S
    y_prompt = _final_norm(x, moe_out, row2(final_g), 0, t_prompt, t["tm"]).reshape(n_prompt, S, D)
    y_sample = _final_norm(x, moe_out, row2(final_g), t_prompt, T - t_prompt, t["tm"]).reshape(B - n_prompt, S, D)
    return y_prompt, y_sample
```

```python
import functools
import math

import jax
import jax.numpy as jnp
from jax import lax
from jax.experimental import pallas as pl
from jax.experimental.pallas import tpu as pltpu

F32 = jnp.float32
BF16 = jnp.bfloat16
LOG2E = 1.4426950408889634

D_MODEL = 2048
GRID_W = 64
BRANCH_W = 512
N_BRANCH = 4
ROPE_THETA = 10000.0
EPS = 1e-6
A_HEADS, A_Q_LORA, A_KV_LORA, A_NOPE, A_ROPE, A_V = 4, 384, 128, 128, 64, 128
B_HEADS, B_QK, B_V = 4, 64, 128
REL_BUCKETS, REL_MAX_DIST = 32, 128
POOL_WINDOWS = (2, 4, 8, 16)
C_GROUPS, C_GROUP_W = 4, 128
D_HEADS, D_KV_HEADS, D_HEAD = 4, 2, 128
N_GROUPS, EXP_PER_GROUP, N_EXPERTS, TOP_K, D_EXPERT = 4, 8, 32, 2, 512

LANES = 128
ROPE_PAD = LANES - A_ROPE
ZA_W = A_Q_LORA + A_KV_LORA + LANES
ZB_W = 3 * B_HEADS * B_V
ZC_W = BRANCH_W
ZD_W = (D_HEADS + 2 * D_KV_HEADS) * D_HEAD
A_QK_W = 2 * LANES
ROUTER_W = LANES
HALO = 16
VMEM_LIMIT = 56 * 2 ** 20


def _tiles(seq, tokens):
    def pick(n, pref):
        t = min(pref, n)
        while n % t:
            t //= 2
        return t
    return dict(
        tm=pick(seq, 512),
        tm_in=pick(seq, 256),
        tq=pick(seq, 512),
        rows=pick(seq, 1024),
        tk=pick(seq, 8192),
        tm_merge=pick(tokens, 1024),
        tn=512,
        tb=512,
        pool_chunk=pick(seq, 1024),
    )


def _cparams(*sem):
    return pltpu.CompilerParams(dimension_semantics=sem, vmem_limit_bytes=VMEM_LIMIT)


def _resident(shape):
    zeros = (0,) * len(shape)
    return pl.BlockSpec(shape, lambda *_: zeros, pipeline_mode=pl.Buffered(1))


def _rms(x, g=None):
    y = x * lax.rsqrt(jnp.mean(x * x, axis=-1, keepdims=True) + EPS)
    return y if g is None else y * g


def _rope_lanes(x, c, s):
    lane = lax.broadcasted_iota(jnp.int32, x.shape, 1)
    first_half = (lane & 32) == 0
    partner = jnp.where(first_half, pltpu.roll(x, 96, 1), pltpu.roll(x, 32, 1))
    return x * c + partner * s


def _residual_sum(x_ref, y_refs):
    x = x_ref[...]
    for y_ref in y_refs:
        x = x + y_ref[...].astype(F32)
    return x


N_PREP_PARAMS = 10
PREP_OUT = ((None, A_HEADS * A_QK_W), (A_HEADS, A_QK_W), (A_HEADS, A_V), (None, D_HEADS * D_HEAD),
            (D_KV_HEADS, D_HEAD), (D_KV_HEADS, D_HEAD), (B_HEADS, B_V), (B_HEADS, B_V))


def _norm_inproj_body(*refs, n_x, n_add, first_rows):
    x_refs, y_refs = refs[:n_x], refs[n_x:n_x + n_add]
    g_ref, w_ref = refs[n_x + n_add:n_x + n_add + 2]
    prep_refs = refs[n_x + n_add + 2:n_x + n_add + 2 + N_PREP_PARAMS]
    outs = refs[n_x + n_add + 2 + N_PREP_PARAMS:]
    if n_x == 2:
        x = jnp.where(pl.program_id(0) < first_rows, x_refs[0][...], x_refs[1][...])
    else:
        x = _residual_sum(x_refs[0], y_refs)
    if n_x == 2 or n_add:
        outs[0][...] = x
        outs = outs[1:]
    h_ref, zb_ref, zc_ref = outs[:3]
    xg = x * g_ref[...]
    r = lax.rsqrt(jnp.mean(x * x, axis=-1, keepdims=True) + EPS)
    xg16 = xg.astype(BF16)
    proj = lambda lo, w_: jnp.dot(xg16, w_ref[:, lo:lo + w_], preferred_element_type=F32) * r
    h_ref[...] = (xg * r).astype(BF16)
    za, zd = proj(0, ZA_W), proj(ZA_W + ZB_W + ZC_W, ZD_W)
    _mixer_prep(za, zd, prep_refs, outs[3:9])
    zb = proj(ZA_W, ZB_W).astype(BF16)
    zb_ref[...] = zb
    kb_ref, vb_ref = outs[9:]
    for hd in range(B_HEADS):
        kb_ref[hd] = zb[:, (B_HEADS + hd) * B_V:(B_HEADS + hd + 1) * B_V]
        vb_ref[hd] = zb[:, (2 * B_HEADS + hd) * B_V:(2 * B_HEADS + hd + 1) * B_V]
    zc_ref[...] = proj(ZA_W + ZB_W, ZC_W).astype(BF16)


def _norm_inproj(xs, adds, g, w, prep, seq, tm):
    D = xs[0].shape[1]
    T = sum(x.shape[0] for x in xs)
    per_seq = seq // tm
    row = lambda w_: pl.BlockSpec((tm, w_), lambda i: (i, 0))
    if len(xs) == 2:
        na = xs[0].shape[0] // tm
        x_specs = [pl.BlockSpec((tm, D), lambda i: (jnp.minimum(i, na - 1), 0)),
                   pl.BlockSpec((tm, D), lambda i: (jnp.maximum(i - na, 0), 0))]
    else:
        na, x_specs = None, [row(D)]
    new_x = len(xs) == 2 or adds is not None
    sum_spec, sum_shape = ([row(D)], [jax.ShapeDtypeStruct((T, D), F32)]) if new_x else ([], [])
    tab = pl.BlockSpec((tm, LANES), lambda i: (i % per_seq, 0))
    outs_hw = ((None, D), (None, ZB_W), (None, ZC_W)) + PREP_OUT
    out_spec = lambda hw: row(hw[1]) if hw[0] is None else pl.BlockSpec((hw[0], tm, hw[1]), lambda i: (0, i, 0))
    out_sds = lambda hw: jax.ShapeDtypeStruct((T, hw[1]) if hw[0] is None else (hw[0], T, hw[1]), BF16)
    add_specs, add_args = _stacked_adds(adds, T, D, tm, 0)
    outs = pl.pallas_call(
        functools.partial(_norm_inproj_body, n_x=len(xs), n_add=len(add_args), first_rows=na),
        grid=(T // tm,),
        in_specs=x_specs + add_specs + [_resident((1, D)), _resident(w.shape)]
        + [_resident(p.shape) for p in prep[:N_PREP_PARAMS - 4]] + [tab] * 4,
        out_specs=sum_spec + [out_spec(hw) for hw in outs_hw],
        out_shape=sum_shape + [out_sds(hw) for hw in outs_hw],
        compiler_params=_cparams("parallel"),
        name="norm_inproj",
    )(*xs, *add_args, g, w, *prep)
    return list(outs) if new_x else [xs[0]] + list(outs)


def _stacked_adds(adds, T, D, tm, first):
    if adds is None:
        return [], []
    n = adds.shape[0] // T
    specs = [pl.BlockSpec((tm, D), lambda i, k=k: (first + i + k * (T // tm), 0)) for k in range(n)]
    return specs, [adds] * n


def _mixer_prep(za, zd, prep_refs, out_refs):
    gq_ref, gkv_ref, wuq_ref, wukv_ref, gdq_ref, gdk_ref, ca_ref, sa_ref, cd_ref, sd_ref = prep_refs
    qa_ref, ka_ref, va_ref, qd_ref, kd_ref, vd_ref = out_refs
    a_scale = (A_NOPE + A_ROPE) ** -0.5 * LOG2E
    d_scale = D_HEAD ** -0.5 * LOG2E
    ca, sa, cd, sd = ca_ref[...], sa_ref[...], cd_ref[...], sd_ref[...]
    cq = _rms(za[:, :A_Q_LORA], gq_ref[...]).astype(BF16)
    q = jnp.dot(cq, wuq_ref[...], preferred_element_type=F32)
    ckv = _rms(za[:, A_Q_LORA:A_Q_LORA + A_KV_LORA], gkv_ref[...]).astype(BF16)
    kv = jnp.dot(ckv, wukv_ref[...], preferred_element_type=F32)
    k_rope = _rope_lanes(za[:, A_Q_LORA + A_KV_LORA:], ca, sa).astype(BF16)
    nope_w = A_HEADS * A_NOPE
    for h in range(A_HEADS):
        lo = h * A_QK_W
        qa_ref[:, lo:lo + LANES] = (q[:, h * LANES:(h + 1) * LANES] * a_scale).astype(BF16)
        q_rope = _rope_lanes(q[:, nope_w + h * LANES:nope_w + (h + 1) * LANES], ca, sa)
        qa_ref[:, lo + LANES:lo + 2 * LANES] = (q_rope * a_scale).astype(BF16)
        ka_ref[h, :, :LANES] = kv[:, h * LANES:(h + 1) * LANES].astype(BF16)
        ka_ref[h, :, LANES:] = k_rope
        va_ref[h] = kv[:, nope_w + h * A_V:nope_w + (h + 1) * A_V].astype(BF16)
    for h in range(D_HEADS):
        xh = _rms(zd[:, h * D_HEAD:(h + 1) * D_HEAD], gdq_ref[...])
        qd_ref[:, h * D_HEAD:(h + 1) * D_HEAD] = (_rope_lanes(xh, cd, sd) * d_scale).astype(BF16)
    k_off = D_HEADS * D_HEAD
    for h in range(D_KV_HEADS):
        xh = _rms(zd[:, k_off + h * D_HEAD:k_off + (h + 1) * D_HEAD], gdk_ref[...])
        kd_ref[h] = _rope_lanes(xh, cd, sd).astype(BF16)
        v_lo = k_off + (D_KV_HEADS + h) * D_HEAD
        vd_ref[h] = zd[:, v_lo:v_lo + D_HEAD].astype(BF16)


def _qk(q, k):
    return lax.dot_general(q, k, (((1,), (1,)), ((), ())), preferred_element_type=F32)


def _lane_fold(fn, acc, x):
    for c in range(x.shape[1] // LANES):
        acc = fn(acc, x[:, c * LANES:(c + 1) * LANES])
    return acc


def _lane_blocks(x):
    return [x[:, i * LANES:(i + 1) * LANES] for i in range(x.shape[1] // LANES)]


def _score_pass(q, k_ref, s_ref, m_ref, tk, n_steps, sub_max=None):
    tq, ts = s_ref.shape[1:]
    sub = tk // ts
    m_ref[...] = jnp.full((tq, LANES), -jnp.inf, F32)

    def step(j, _):
        ks = pl.multiple_of(j * tk, tk)
        s = _qk(q, k_ref[0, pl.ds(ks, tk), :])
        mrun = None
        for c in range(sub):
            jj = j * sub + c
            sc = s[:, c * ts:(c + 1) * ts]
            s_ref[jj] = sc
            blocks = _lane_blocks(sc)
            m_sub = functools.reduce(jnp.maximum, blocks) if sub_max is None else sub_max(jj, blocks)
            mrun = m_sub if mrun is None else jnp.maximum(mrun, m_sub)
        m_ref[...] = jnp.maximum(m_ref[...], mrun)
        return 0

    lax.fori_loop(0, n_steps, step, 0)
    return m_ref[...]


def _value_pass(v_ref, s_ref, a_ref, tk, n_steps, m_of):
    tq, ts = s_ref.shape[1:]
    sub = tk // ts
    dv = v_ref.shape[2]
    ones = jnp.ones((tk, LANES), BF16)
    a_ref[...] = jnp.zeros(a_ref.shape, F32)

    def step(j, _):
        ks = pl.multiple_of(j * tk, tk)
        p = []
        for c in range(sub):
            jj = j * sub + c
            m = m_of(jj)
            s = s_ref[jj]
            p += [jnp.exp2((s[:, i * LANES:(i + 1) * LANES] - m).astype(BF16)) for i in range(ts // LANES)]
        v_ext = jnp.concatenate([v_ref[0, pl.ds(ks, tk), :], ones], axis=1)
        a_ref[...] += jnp.dot(jnp.concatenate(p, axis=1), v_ext, preferred_element_type=F32)
        return 0

    lax.fori_loop(0, n_steps, step, 0)
    return a_ref[:, :dv] / a_ref[:, dv:]


def _row_max(mrun):
    return jnp.broadcast_to(jnp.max(mrun, axis=-1, keepdims=True), mrun.shape)


def _attn_scratch(n_sub, rows, ts):
    return [pltpu.VMEM((n_sub, rows, ts), F32), pltpu.VMEM((rows, LANES), F32), pltpu.VMEM((rows, 2 * LANES), F32)]


def _flash_body(n_ref, q_ref, k_ref, v_ref, o_ref, s_ref, m_ref, a_ref, *, tk, stack):
    tq = q_ref.shape[1]
    q_w = q_ref.shape[2] // stack
    q = jnp.concatenate([q_ref[0, :, c * q_w:(c + 1) * q_w] for c in range(stack)], axis=0)
    m = _row_max(_score_pass(q, k_ref, s_ref, m_ref, tk, n_ref[0]))
    o = _value_pass(v_ref, s_ref, a_ref, tk, n_ref[0], lambda jj: m).astype(o_ref.dtype)
    o_ref[0] = jnp.concatenate([o[c * tq:(c + 1) * tq] for c in range(stack)], axis=1)


def _flash(q, k, v, *, kv_heads, stack, q_w, tq, ts, tk):
    B, S, _ = q.shape
    dv = LANES
    return pl.pallas_call(
        functools.partial(_flash_body, tk=tk, stack=stack),
        grid=(B, kv_heads, S // tq),
        in_specs=[pl.BlockSpec(memory_space=pltpu.SMEM),
                  pl.BlockSpec((1, tq, stack * q_w), lambda b, h, i: (b, i, h)),
                  pl.BlockSpec((1, S, q_w), lambda b, h, i: (h * B + b, 0, 0)),
                  pl.BlockSpec((1, S, dv), lambda b, h, i: (h * B + b, 0, 0))],
        out_specs=pl.BlockSpec((1, tq, stack * dv), lambda b, h, i: (b, i, h)),
        out_shape=jax.ShapeDtypeStruct((B, S, kv_heads * stack * dv), BF16),
        scratch_shapes=_attn_scratch(S // ts, stack * tq, ts),
        compiler_params=_cparams("parallel", "parallel", "arbitrary"),
        name="flash_attn",
    )(jnp.full((1,), S // tk, jnp.int32), q, k, v)


def _diff_body(n_ref, lam_ref, cb_ref, q_ref, k_ref, v_ref, diag_ref, corner_ref, o_ref, s_ref, m_ref, a_ref, *,
               tk, out_scale):
    h, qi = pl.program_id(1), pl.program_id(2)
    n_sub = s_ref.shape[0]
    tq = q_ref.shape[1]
    nd = REL_MAX_DIST
    qf = q_ref[0].astype(F32) * (B_QK ** -0.5 * LOG2E)
    lane = lax.broadcasted_iota(jnp.int32, qf.shape, 1)
    c_lo, c_hi = cb_ref[0, h], cb_ref[1, h]
    side_shift = lambda jj: jnp.where(jj < qi, c_lo, jnp.where(jj > qi, c_hi, 0.0))
    tile_max = lambda s: _lane_fold(jnp.maximum, s[:, :LANES], s[:, LANES:])

    q = jnp.concatenate([jnp.where(lane < B_QK, qf, 0.0), jnp.where(lane >= B_QK, qf, 0.0)], axis=0).astype(BF16)
    row = lax.broadcasted_iota(jnp.int32, (2 * tq, LANES), 0) % tq
    pen_lo = jnp.where(row < nd, -jnp.inf, 0.0)
    pen_hi = jnp.where(row >= tq - nd, -jnp.inf, 0.0)
    zero = jnp.zeros_like(pen_lo)

    def sub_max(jj, blocks):
        blocks = list(blocks)
        blocks[-1] = blocks[-1] + jnp.where(jj == qi - 1, pen_lo, zero)
        blocks[0] = blocks[0] + jnp.where(jj == qi + 1, pen_hi, zero)
        shift = jnp.where(jj < qi, c_lo, jnp.where(jj > qi, c_hi, -jnp.inf))
        return functools.reduce(jnp.maximum, blocks) + shift

    mrun = _score_pass(q, k_ref, s_ref, m_ref, tk, n_ref[0], sub_max=sub_max)

    diag = diag_ref[0]
    s = s_ref[qi] + jnp.concatenate([diag, diag], axis=0)
    s_ref[qi] = s
    mrun = jnp.maximum(mrun, tile_max(s))

    def neighbour(jj, rows0, cols0, corner, c):
        def fix(mrun):
            for r in (rows0, tq + rows0):
                fixed = s_ref[jj, r:r + nd, cols0:cols0 + nd] + corner
                s_ref[jj, r:r + nd, cols0:cols0 + nd] = fixed
                pieces = [mrun[:r], jnp.maximum(mrun[r:r + nd], fixed + c), mrun[r + nd:]]
                mrun = jnp.concatenate([p for p in pieces if p.shape[0]], axis=0)
            return mrun
        return fix

    mrun = lax.cond(qi >= 1, neighbour(jnp.maximum(qi - 1, 0), 0, tq - nd, corner_ref[0, 0], c_lo),
                    lambda m_: m_, mrun)
    mrun = lax.cond(qi + 1 < n_sub, neighbour(jnp.minimum(qi + 1, n_sub - 1), tq - nd, 0, corner_ref[0, 1], c_hi),
                    lambda m_: m_, mrun)
    m = _row_max(mrun)
    o = _value_pass(v_ref, s_ref, a_ref, tk, n_ref[0], lambda jj: m - side_shift(jj))
    o = o[:tq] - lam_ref[0] * o[tq:]
    o_ref[0] = (_rms(o) * out_scale).astype(o_ref.dtype)


def _rel_bucket(rel):
    half = REL_BUCKETS // 2
    max_exact = half // 2
    ret = (rel > 0).astype(jnp.int32) * half
    n = jnp.abs(rel)
    large = max_exact + (jnp.log(jnp.maximum(n, 1).astype(F32) / max_exact)
                         / math.log(REL_MAX_DIST / max_exact) * (half - max_exact)).astype(jnp.int32)
    large = jnp.minimum(large, half - 1)
    return ret + jnp.where(n < max_exact, n, large)


def _diff_bias_tables(rel_bias, tq):
    nd = REL_MAX_DIST
    rb = rel_bias.astype(F32) * LOG2E
    far = jnp.stack([rb[REL_BUCKETS // 2 - 1], rb[REL_BUCKETS - 1]])

    def table(rel):
        bucket = _rel_bucket(rel)
        out = jnp.zeros((B_HEADS,) + rel.shape, F32)
        for b in range(REL_BUCKETS):
            out = jnp.where(bucket[None] == b, rb[b].reshape((B_HEADS,) + (1,) * rel.ndim), out)
        return out

    a, e = jnp.arange(tq), jnp.arange(nd)
    diag = table(a[None, :] - a[:, None])
    lo = table((tq - nd + e[None, :]) - tq - e[:, None]) - far[0][:, None, None]
    hi = table(e[None, :] + tq - (tq - nd + e[:, None])) - far[1][:, None, None]
    return diag, jnp.stack([lo, hi], axis=1), far


def _diff_attn(zb, kb, vb, lam, diag, corner, far, *, tq, tk, out_scale):
    B, S, _ = zb.shape
    nd = REL_MAX_DIST
    assert tq >= nd, "far key chunks must lie beyond the last distinct relative bucket"
    assert nd == LANES, "the neighbours' near corner is handled as one lane block"
    H = B_HEADS
    smem = pl.BlockSpec(memory_space=pltpu.SMEM)
    return pl.pallas_call(
        functools.partial(_diff_body, tk=tk, out_scale=out_scale),
        grid=(B, H, S // tq),
        in_specs=[smem, smem, smem,
                  pl.BlockSpec((1, tq, B_V), lambda b, h, i: (b, i, h)),
                  pl.BlockSpec((1, S, B_V), lambda b, h, i: (h * B + b, 0, 0)),
                  pl.BlockSpec((1, S, B_V), lambda b, h, i: (h * B + b, 0, 0)),
                  pl.BlockSpec((1, tq, tq), lambda b, h, i: (h, 0, 0)),
                  pl.BlockSpec((1, 2, nd, nd), lambda b, h, i: (h, 0, 0, 0))],
        out_specs=pl.BlockSpec((1, tq, B_V), lambda b, h, i: (b, i, h)),
        out_shape=jax.ShapeDtypeStruct((B, S, H * B_V), BF16),
        scratch_shapes=_attn_scratch(S // tq, 2 * tq, tq),
        compiler_params=_cparams("parallel", "parallel", "arbitrary"),
        name="diff_attn",
    )(jnp.full((1,), S // tk, jnp.int32), lam, far, zb, kb, vb, diag, corner)


def _pool_body(u_ref, w_ref, sc_ref, o_ref, pad_ref, *, chunk):
    g = pl.program_id(1)
    S = u_ref.shape[1]
    pad_ref[0:HALO, :] = jnp.zeros((HALO, LANES), F32)
    pad_ref[HALO + S:, :] = jnp.zeros((HALO, LANES), F32)
    pad_ref[HALO:HALO + S, :] = u_ref[0].astype(F32)
    w_mat, sc = w_ref[0], sc_ref[...]

    def pooled(win):
        def body(c, _):
            r0 = pl.multiple_of(c * chunk, chunk)
            tot = pad_ref[pl.ds(r0 + HALO - win // 2, chunk), :]
            for j in range(1 - win // 2, win // 2):
                tot = tot + pad_ref[pl.ds(r0 + HALO + j, chunk), :]
            t = r0 + lax.broadcasted_iota(jnp.int32, (chunk, 1), 0)
            cnt = jnp.clip(t - win // 2 + win, 0, S) - jnp.clip(t - win // 2, 0, S)
            d = tot / cnt.astype(F32) - pad_ref[pl.ds(r0 + HALO, chunk), :]
            y = jnp.dot(d.astype(BF16), w_mat, preferred_element_type=F32) * sc
            o_ref[0, pl.ds(r0, chunk), :] = y.astype(o_ref.dtype)
            return 0
        lax.fori_loop(0, S // chunk, body, 0)

    for gi, win in enumerate(POOL_WINDOWS):
        pl.when(g == gi)(functools.partial(pooled, win))


def _pool(zc, w_pool, pool_scale, chunk):
    B, S, _ = zc.shape
    blk = pl.BlockSpec((1, S, LANES), lambda b, g: (b, 0, g))
    return pl.pallas_call(
        functools.partial(_pool_body, chunk=chunk),
        grid=(B, C_GROUPS),
        in_specs=[blk, pl.BlockSpec((1, C_GROUP_W, C_GROUP_W), lambda b, g: (g, 0, 0)),
                  pl.BlockSpec((1, LANES), lambda b, g: (0, g))],
        out_specs=blk,
        out_shape=jax.ShapeDtypeStruct((B, S, BRANCH_W), BF16),
        scratch_shapes=[pltpu.VMEM((S + 2 * HALO, LANES), F32)],
        compiler_params=_cparams("parallel", "arbitrary"),
        name="pool_mixer",
    )(zc, w_pool, pool_scale)


def _gated_merge_body(h_ref, oa_ref, ob_ref, oc_ref, od_ref, wg_ref, bg_ref, wl_ref, m_ref):
    h = h_ref[...]
    merged = None
    for b, o_ref in enumerate((oa_ref, ob_ref, oc_ref, od_ref)):
        gate = jax.nn.sigmoid(jnp.dot(h, wg_ref[b], preferred_element_type=F32) + bg_ref[b:b + 1, :])
        term = gate * jnp.dot(o_ref[...], wl_ref[b], preferred_element_type=F32)
        merged = term if merged is None else merged + term
    m_ref[...] = merged.astype(BF16)


def _gated_merge(h, outs, wg, bg, wl, tm, tn):
    T, D = h.shape
    row = lambda w_: pl.BlockSpec((tm, w_), lambda i, j: (i, 0))
    return pl.pallas_call(
        _gated_merge_body,
        grid=(T // tm, D // tn),
        in_specs=[row(D)] + [row(BRANCH_W)] * N_BRANCH + [
            pl.BlockSpec((N_BRANCH, D, tn), lambda i, j: (0, 0, j)),
            pl.BlockSpec((N_BRANCH, tn), lambda i, j: (0, j)),
            pl.BlockSpec((N_BRANCH, BRANCH_W, tn), lambda i, j: (0, 0, j))],
        out_specs=pl.BlockSpec((tm, tn), lambda i, j: (i, j)),
        out_shape=jax.ShapeDtypeStruct((T, D), BF16),
        compiler_params=_cparams("parallel", "arbitrary"),
        name="gated_merge",
    )(h, *outs, wg, bg, wl)


ROUTE_ROWS = 8


def _out_proj_body(m_ref, x_ref, wo_ref, g2_ref, wr_ref, rb_ref, xn_ref, h2_ref, rt_ref):
    xn = x_ref[...] + jnp.dot(m_ref[...], wo_ref[...], preferred_element_type=F32)
    xn_ref[...] = xn
    h2 = _rms(xn, g2_ref[...])
    h2_ref[...] = h2.astype(BF16)
    h_hi = h2.astype(BF16)
    h_lo = (h2 - h_hi.astype(F32)).astype(BF16)
    hi = jnp.dot(h_hi, wr_ref[...], preferred_element_type=F32)
    logits = hi[:, :ROUTER_W] + hi[:, ROUTER_W:] + jnp.dot(h_lo, wr_ref[:, :ROUTER_W], preferred_element_type=F32)
    rt_ref[...] = _route_tile(logits + rb_ref[...])


def _out_proj(merged, x, wo, g2, wr, route_bias, tm):
    T, D = x.shape
    row = pl.BlockSpec((tm, D), lambda i: (i, 0))
    return pl.pallas_call(
        _out_proj_body,
        grid=(T // tm,),
        in_specs=[row, row, _resident((D, D)), _resident((1, D)), _resident((D, 2 * ROUTER_W)),
                  _resident((1, ROUTER_W))],
        out_specs=[row, row, pl.BlockSpec((ROUTE_ROWS, tm), lambda i: (0, i))],
        out_shape=[jax.ShapeDtypeStruct((T, D), F32), jax.ShapeDtypeStruct((T, D), BF16),
                   jax.ShapeDtypeStruct((ROUTE_ROWS, T), F32)],
        compiler_params=_cparams("parallel"),
        name="out_proj_route",
    )(merged, x, wo, g2, wr, route_bias)


def _route_tile(lg):
    lane = lax.broadcasted_iota(jnp.int32, lg.shape, 1).astype(F32)
    neg = -jnp.inf
    row_max = lambda v: jnp.max(v, axis=-1, keepdims=True)
    first_at = lambda v, m: jnp.min(jnp.where(v == m, lane, float(LANES)), axis=-1, keepdims=True)
    g_lg = jnp.where(lane < N_GROUPS, lg, neg)
    g_max = row_max(g_lg)
    g_top = first_at(g_lg, g_max)
    pg_top = 1.0 / jnp.sum(jnp.exp(g_lg - g_max), axis=-1, keepdims=True)
    e_lo = N_GROUPS + EXP_PER_GROUP * g_top
    e_lg = jnp.where(jnp.logical_and(lane >= e_lo, lane < e_lo + EXP_PER_GROUP), lg, neg)
    picks, maxes = [], []
    for _ in range(TOP_K):
        m = row_max(e_lg)
        i = first_at(e_lg, m)
        picks.append(i)
        maxes.append(m)
        e_lg = jnp.where(lane == i, neg, e_lg)
    w = [jnp.exp(m - maxes[0]) for m in maxes]
    w_sum = functools.reduce(lambda a, b: a + b, w)
    cols = [i - N_GROUPS for i in picks] + [pg_top * wk / w_sum for wk in w]
    tile = jnp.zeros(lg.shape, F32)
    for c, v in enumerate(cols):
        tile = jnp.where(lane == c, v, tile)
    return tile.T[:ROUTE_ROWS]


def _dispatch(routed, tb):
    T = routed.shape[1]
    A = T * TOP_K
    flat_e = routed[:TOP_K].astype(jnp.int32).T.reshape(A)
    flat_w = routed[TOP_K:2 * TOP_K].T.reshape(A)
    iota = jnp.arange(A, dtype=jnp.int32)
    se, order, sw = lax.sort((flat_e, iota, flat_w), num_keys=1, is_stable=True)
    experts = jnp.arange(N_EXPERTS, dtype=jnp.int32)
    counts = jnp.sum((flat_e[:, None] == experts[None, :]).astype(jnp.int32), axis=0)
    start = jnp.cumsum(counts) - counts
    padded = (counts + tb - 1) // tb * tb
    pend = jnp.cumsum(padded)
    pstart = pend - padded
    n_blocks = -(-A // tb) + N_EXPERTS
    P = n_blocks * tb
    blk_first = jnp.arange(n_blocks, dtype=jnp.int32) * tb
    blk_exp = jnp.minimum(jnp.sum((pend[None, :] <= blk_first[:, None]).astype(jnp.int32), axis=1), N_EXPERTS - 1)
    n_valid = (pend[-1] // tb).astype(jnp.int32).reshape(1)
    slot = jnp.arange(P, dtype=jnp.int32)
    before = (pend[None, :] <= slot[:, None]).astype(jnp.int32)
    shift = jnp.sum(before * (padded - counts)[None, :], axis=1)
    in_pad = jnp.any(jnp.logical_and(slot[:, None] >= (pstart + counts)[None, :], slot[:, None] < pend[None, :]),
                     axis=1)
    valid = jnp.logical_and(jnp.logical_not(in_pad), slot < pend[-1])
    src = jnp.where(valid, slot - shift, slot % A)
    slot_tok = order[src] // TOP_K
    slot_w = jnp.where(valid, sw[src], 0.0)
    dest = iota + jnp.sum((se[:, None] == experts[None, :]).astype(jnp.int32) * (pstart - start)[None, :], axis=1)
    _, slot_of = lax.sort((order, dest), num_keys=1)
    return slot_tok, slot_w, blk_exp, n_valid, slot_of.reshape(T, TOP_K).T


def _expert_body(be_ref, nv_ref, x_ref, sw_ref, wg_ref, wu_ref, wd_ref, o_ref, wg_s, wu_s, wd_s):
    i = pl.program_id(0)

    @pl.when(jnp.logical_or(i == 0, be_ref[i] != be_ref[jnp.maximum(i - 1, 0)]))
    def _():
        wg_s[...] = wg_ref[0, 0].astype(BF16)
        wu_s[...] = wu_ref[0, 0].astype(BF16)
        wd_s[...] = wd_ref[0, 0].astype(BF16)

    @pl.when(i < nv_ref[0])
    def _():
        x = x_ref[...]
        g = jnp.dot(x, wg_s[...], preferred_element_type=F32)
        u = jnp.dot(x, wu_s[...], preferred_element_type=F32)
        a = g * jax.nn.sigmoid(g) * u
        sw = sw_ref[...]
        a = jnp.concatenate([a[:, c * LANES:(c + 1) * LANES] * sw for c in range(a.shape[1] // LANES)], axis=1)
        o_ref[...] = jnp.dot(a.astype(BF16), wd_s[...], preferred_element_type=F32).astype(o_ref.dtype)

    @pl.when(i >= nv_ref[0])
    def _():
        o_ref[...] = jnp.zeros(o_ref.shape, o_ref.dtype)


def _expert_ffn(xs, slot_w, blk_exp, n_valid, wg, wu, wd, layer, tb):
    P, D = xs.shape
    grid_spec = pltpu.PrefetchScalarGridSpec(
        num_scalar_prefetch=2,
        grid=(P // tb,),
        in_specs=[pl.BlockSpec((tb, D), lambda i, be, nv: (i, 0)),
                  pl.BlockSpec((tb, LANES), lambda i, be, nv: (i, 0)),
                  pl.BlockSpec((1, 1, D, D_EXPERT), lambda i, be, nv: (layer, be[i], 0, 0)),
                  pl.BlockSpec((1, 1, D, D_EXPERT), lambda i, be, nv: (layer, be[i], 0, 0)),
                  pl.BlockSpec((1, 1, D_EXPERT, D), lambda i, be, nv: (layer, be[i], 0, 0))],
        out_specs=pl.BlockSpec((tb, D), lambda i, be, nv: (i, 0)),
        scratch_shapes=[pltpu.VMEM((D, D_EXPERT), BF16), pltpu.VMEM((D, D_EXPERT), BF16),
                        pltpu.VMEM((D_EXPERT, D), BF16)],
    )
    return pl.pallas_call(
        _expert_body,
        grid_spec=grid_spec,
        out_shape=jax.ShapeDtypeStruct((P, D), BF16),
        compiler_params=_cparams("arbitrary"),
        name="expert_ffn",
    )(blk_exp, n_valid, xs, jnp.broadcast_to(slot_w[:, None], (P, LANES)), wg, wu, wd)


def _final_norm_body(*refs):
    x_ref, y_refs, (g_ref, o_ref) = refs[0], refs[1:-2], refs[-2:]
    o_ref[...] = _rms(_residual_sum(x_ref, y_refs), g_ref[...])


def _final_norm(x, adds, g, row0, rows, tm):
    T, D = x.shape
    first = row0 // tm
    add_specs, add_args = _stacked_adds(adds, T, D, tm, first)
    return pl.pallas_call(
        _final_norm_body, grid=(rows // tm,),
        in_specs=[pl.BlockSpec((tm, D), lambda i: (first + i, 0))] + add_specs + [_resident((1, D))],
        out_specs=pl.BlockSpec((tm, D), lambda i: (i, 0)),
        out_shape=jax.ShapeDtypeStruct((rows, D), F32), compiler_params=_cparams("parallel"), name="final_norm",
    )(x, *add_args, g)


def _rope_tables(seq):
    def angles(pos, dim):
        inv = 1.0 / (ROPE_THETA ** (jnp.arange(0, dim, 2, dtype=F32) / dim))
        ang = pos.astype(F32)[:, None] * inv[None, :]
        return jnp.cos(ang), jnp.sin(ang)
    pos = jnp.arange(seq)
    c1, s1 = angles(pos, A_ROPE)
    cr, sr = angles(pos // GRID_W, D_HEAD // 2)
    cc, sc = angles(pos % GRID_W, D_HEAD // 2)
    pad1, pad0 = jnp.ones((seq, ROPE_PAD), F32), jnp.zeros((seq, ROPE_PAD), F32)
    return (jnp.concatenate([c1, c1, pad1], 1), jnp.concatenate([-s1, s1, pad0], 1),
            jnp.concatenate([cr, cr, cc, cc], 1), jnp.concatenate([-sr, sr, -sc, sc], 1))


def _layer_weights(l, w_in, w_uq, w_ukv, w_route_group, w_route_expert):
    D = D_MODEL
    k_r_end = A_Q_LORA + A_KV_LORA + A_ROPE
    win = jnp.concatenate([w_in[l][:, :k_r_end], jnp.zeros((D, ROPE_PAD), F32), w_in[l][:, k_r_end:]], 1)
    uq = w_uq[l].reshape(A_Q_LORA, A_HEADS, A_NOPE + A_ROPE)
    uq_rope = jnp.pad(uq[:, :, A_NOPE:], ((0, 0), (0, 0), (0, ROPE_PAD)))
    wuq = jnp.concatenate([uq[:, :, :A_NOPE].reshape(A_Q_LORA, -1), uq_rope.reshape(A_Q_LORA, -1)], 1)
    ukv = w_ukv[l].reshape(A_KV_LORA, A_HEADS, A_NOPE + A_V)
    wukv = jnp.concatenate([ukv[:, :, :A_NOPE].reshape(A_KV_LORA, -1), ukv[:, :, A_NOPE:].reshape(A_KV_LORA, -1)], 1)
    wr = jnp.concatenate([w_route_group[l], w_route_expert[l],
                          jnp.zeros((D, ROUTER_W - N_GROUPS - N_EXPERTS), F32)], 1)
    wr_hi = wr.astype(BF16)
    wr_lo = (wr - wr_hi.astype(F32)).astype(BF16)
    return win.astype(BF16), wuq.astype(BF16), wukv.astype(BF16), jnp.concatenate([wr_hi, wr_lo], 1)


def kernel(x_prompt, x_sample, norm1_g, w_in, q_norm_g, kv_norm_g, w_uq, w_ukv, lam_q1, lam_k1, lam_q2, lam_k2,
           rel_bias, w_pool, pool_scale, qk_norm_q, qk_norm_k, w_lift, w_gate, b_gate, w_out, norm2_g,
           w_route_group, b_route_group, w_route_expert, b_route_expert, w_exp_gate, w_exp_up, w_exp_down,
           final_g):
    assert x_prompt.shape[1:] == x_sample.shape[1:], "both request groups must share (seq, d_model)"
    n_prompt, S, D = x_prompt.shape
    B = n_prompt + x_sample.shape[0]
    T = B * S
    t = _tiles(S, T)
    x = None
    depth = w_in.shape[0]
    tabs = _rope_tables(S)
    diag, corner, far = _diff_bias_tables(rel_bias, t["tq"])
    row2 = lambda v: v.reshape(1, -1).astype(F32)

    moe_out = None
    for l in range(depth):
        lam_init = 0.8 - 0.6 * math.exp(-0.3 * l)
        lam = (jnp.exp(jnp.sum(lam_q1[l] * lam_k1[l])) - jnp.exp(jnp.sum(lam_q2[l] * lam_k2[l])) + lam_init)
        win, wuq, wukv, wr = _layer_weights(l, w_in, w_uq, w_ukv, w_route_group, w_route_expert)

        xs = (x_prompt.reshape(-1, D), x_sample.reshape(-1, D)) if l == 0 else (x,)
        prep = (row2(q_norm_g[l]), row2(kv_norm_g[l]), wuq, wukv, row2(qk_norm_q[l]), row2(qk_norm_k[l])) + tabs
        x, h, zb, zc, qa, ka, va, qd, kd, vd, kb, vb = _norm_inproj(xs, moe_out, row2(norm1_g[l]), win, prep, S,
                                                                    t["tm_in"])
        seq3 = lambda a: a.reshape(-1, S, a.shape[-1])
        rep = D_HEADS // D_KV_HEADS
        o_a = _flash(seq3(qa), seq3(ka), seq3(va), kv_heads=A_HEADS, stack=1, q_w=A_QK_W,
                     tq=t["rows"], ts=t["tq"], tk=t["tk"])
        o_b = _diff_attn(seq3(zb), seq3(kb), seq3(vb), lam.reshape(1).astype(F32), diag, corner, far, tq=t["tq"],
                         tk=t["tk"], out_scale=1.0 - lam_init)
        o_c = _pool(seq3(zc), w_pool[l].astype(BF16), row2(pool_scale[l]), t["pool_chunk"])
        o_d = _flash(seq3(qd), seq3(kd), seq3(vd), kv_heads=D_KV_HEADS, stack=rep, q_w=D_HEAD,
                     tq=t["rows"] // rep, ts=t["tq"], tk=t["tk"])
        outs = [o.reshape(T, BRANCH_W) for o in (o_a, o_b, o_c, o_d)]
        merged = _gated_merge(h, outs, w_gate[l].astype(BF16), b_gate[l].astype(F32), w_lift[l].astype(BF16),
                              t["tm_merge"], t["tn"])
        route_bias = jnp.concatenate([b_route_group[l], b_route_expert[l],
                                      jnp.zeros((ROUTER_W - N_GROUPS - N_EXPERTS,), F32)]).reshape(1, ROUTER_W)
        x, h2, routed = _out_proj(merged, x, w_out[l].astype(BF16), row2(norm2_g[l]), wr, route_bias.astype(F32),
                                  t["tm"])
        slot_tok, slot_w, blk_exp, n_valid, slot_of = _dispatch(routed, t["tb"])
        ys = _expert_ffn(h2[slot_tok], slot_w, blk_exp, n_valid, w_exp_gate, w_exp_up, w_exp_down, l, t["tb"])
        moe_out = ys[slot_of.reshape(-1)]

    t_prompt = n_prompt * S
    y_prompt = _final_norm(x, moe_out, row2(final_g), 0, t_prompt, t["tm"]).reshape(n_prompt, S, D)
    y_sample = _final_norm(x, moe_out, row2(final_g), t_prompt, T - t_prompt, t["tm"]).reshape(B - n_prompt, S, D)
    return y_prompt, y_sample
```

```python
import functools
import math

import jax
import jax.numpy as jnp
from jax import lax
from jax.experimental import pallas as pl
from jax.experimental.pallas import tpu as pltpu

F32 = jnp.float32
BF16 = jnp.bfloat16
LOG2E = 1.4426950408889634

D_MODEL = 2048
GRID_W = 64
BRANCH_W = 512
N_BRANCH = 4
ROPE_THETA = 10000.0
EPS = 1e-6
A_HEADS, A_Q_LORA, A_KV_LORA, A_NOPE, A_ROPE, A_V = 4, 384, 128, 128, 64, 128
B_HEADS, B_QK, B_V = 4, 64, 128
REL_BUCKETS, REL_MAX_DIST = 32, 128
POOL_WINDOWS = (2, 4, 8, 16)
C_GROUPS, C_GROUP_W = 4, 128
D_HEADS, D_KV_HEADS, D_HEAD = 4, 2, 128
N_GROUPS, EXP_PER_GROUP, N_EXPERTS, TOP_K, D_EXPERT = 4, 8, 32, 2, 512

LANES = 128
ROPE_PAD = LANES - A_ROPE
ZA_W = A_Q_LORA + A_KV_LORA + LANES
ZB_W = 3 * B_HEADS * B_V
ZC_W = BRANCH_W
ZD_W = (D_HEADS + 2 * D_KV_HEADS) * D_HEAD
A_QK_W = 2 * LANES
ROUTER_W = LANES
HALO = 16
VMEM_LIMIT = 56 * 2 ** 20


def _tiles(seq, tokens):
    def pick(n, pref):
        t = min(pref, n)
        while n % t:
            t //= 2
        return t
    return dict(
        tm=pick(seq, 512),
        tm_in=pick(seq, 256),
        tq=pick(seq, 512),
        rows=pick(seq, 1024),
        tk=pick(seq, 8192),
        tm_merge=pick(tokens, 1024),
        tn=512,
        tb=512,
        pool_chunk=pick(seq, 1024),
    )


def _cparams(*sem):
    return pltpu.CompilerParams(dimension_semantics=sem, vmem_limit_bytes=VMEM_LIMIT)


def _resident(shape):
    zeros = (0,) * len(shape)
    return pl.BlockSpec(shape, lambda *_: zeros, pipeline_mode=pl.Buffered(1))


def _rms(x, g=None):
    y = x * lax.rsqrt(jnp.mean(x * x, axis=-1, keepdims=True) + EPS)
    return y if g is None else y * g


def _rope_lanes(x, c, s):
    lane = lax.broadcasted_iota(jnp.int32, x.shape, 1)
    first_half = (lane & 32) == 0
    partner = jnp.where(first_half, pltpu.roll(x, 96, 1), pltpu.roll(x, 32, 1))
    return x * c + partner * s


def _residual_sum(x_ref, y_refs):
    x = x_ref[...]
    for y_ref in y_refs:
        x = x + y_ref[...].astype(F32)
    return x


N_PREP_PARAMS = 10
PREP_OUT = ((None, A_HEADS * A_QK_W), (A_HEADS, A_QK_W), (A_HEADS, A_V), (None, D_HEADS * D_HEAD),
            (D_KV_HEADS, D_HEAD), (D_KV_HEADS, D_HEAD), (B_HEADS, B_V), (B_HEADS, B_V))


def _norm_inproj_body(*refs, n_x, n_add, first_rows):
    x_refs, y_refs = refs[:n_x], refs[n_x:n_x + n_add]
    g_ref, w_ref = refs[n_x + n_add:n_x + n_add + 2]
    prep_refs = refs[n_x + n_add + 2:n_x + n_add + 2 + N_PREP_PARAMS]
    outs = refs[n_x + n_add + 2 + N_PREP_PARAMS:]
    if n_x == 2:
        x = jnp.where(pl.program_id(0) < first_rows, x_refs[0][...], x_refs[1][...])
    else:
        x = _residual_sum(x_refs[0], y_refs)
    if n_x == 2 or n_add:
        outs[0][...] = x
        outs = outs[1:]
    h_ref, zb_ref, zc_ref = outs[:3]
    xg = x * g_ref[...]
    r = lax.rsqrt(jnp.mean(x * x, axis=-1, keepdims=True) + EPS)
    xg16 = xg.astype(BF16)
    proj = lambda lo, w_: jnp.dot(xg16, w_ref[:, lo:lo + w_], preferred_element_type=F32) * r
    h_ref[...] = (xg * r).astype(BF16)
    za, zd = proj(0, ZA_W), proj(ZA_W + ZB_W + ZC_W, ZD_W)
    _mixer_prep(za, zd, prep_refs, outs[3:9])
    zb = proj(ZA_W, ZB_W).astype(BF16)
    zb_ref[...] = zb
    kb_ref, vb_ref = outs[9:]
    for hd in range(B_HEADS):
        kb_ref[hd] = zb[:, (B_HEADS + hd) * B_V:(B_HEADS + hd + 1) * B_V]
        vb_ref[hd] = zb[:, (2 * B_HEADS + hd) * B_V:(2 * B_HEADS + hd + 1) * B_V]
    zc_ref[...] = proj(ZA_W + ZB_W, ZC_W).astype(BF16)


def _norm_inproj(xs, adds, g, w, prep, seq, tm):
    D = xs[0].shape[1]
    T = sum(x.shape[0] for x in xs)
    per_seq = seq // tm
    row = lambda w_: pl.BlockSpec((tm, w_), lambda i: (i, 0))
    if len(xs) == 2:
        na = xs[0].shape[0] // tm
        x_specs = [pl.BlockSpec((tm, D), lambda i: (jnp.minimum(i, na - 1), 0)),
                   pl.BlockSpec((tm, D), lambda i: (jnp.maximum(i - na, 0), 0))]
    else:
        na, x_specs = None, [row(D)]
    new_x = len(xs) == 2 or adds is not None
    sum_spec, sum_shape = ([row(D)], [jax.ShapeDtypeStruct((T, D), F32)]) if new_x else ([], [])
    tab = pl.BlockSpec((tm, LANES), lambda i: (i % per_seq, 0))
    outs_hw = ((None, D), (None, ZB_W), (None, ZC_W)) + PREP_OUT
    out_spec = lambda hw: row(hw[1]) if hw[0] is None else pl.BlockSpec((hw[0], tm, hw[1]), lambda i: (0, i, 0))
    out_sds = lambda hw: jax.ShapeDtypeStruct((T, hw[1]) if hw[0] is None else (hw[0], T, hw[1]), BF16)
    add_specs, add_args = _stacked_adds(adds, T, D, tm, 0)
    outs = pl.pallas_call(
        functools.partial(_norm_inproj_body, n_x=len(xs), n_add=len(add_args), first_rows=na),
        grid=(T // tm,),
        in_specs=x_specs + add_specs + [_resident((1, D)), _resident(w.shape)]
        + [_resident(p.shape) for p in prep[:N_PREP_PARAMS - 4]] + [tab] * 4,
        out_specs=sum_spec + [out_spec(hw) for hw in outs_hw],
        out_shape=sum_shape + [out_sds(hw) for hw in outs_hw],
        compiler_params=_cparams("parallel"),
        name="norm_inproj",
    )(*xs, *add_args, g, w, *prep)
    return list(outs) if new_x else [xs[0]] + list(outs)


def _stacked_adds(adds, T, D, tm, first):
    if adds is None:
        return [], []
    n = adds.shape[0] // T
    specs = [pl.BlockSpec((tm, D), lambda i, k=k: (first + i + k * (T // tm), 0)) for k in range(n)]
    return specs, [adds] * n


def _mixer_prep(za, zd, prep_refs, out_refs):
    gq_ref, gkv_ref, wuq_ref, wukv_ref, gdq_ref, gdk_ref, ca_ref, sa_ref, cd_ref, sd_ref = prep_refs
    qa_ref, ka_ref, va_ref, qd_ref, kd_ref, vd_ref = out_refs
    a_scale = (A_NOPE + A_ROPE) ** -0.5 * LOG2E
    d_scale = D_HEAD ** -0.5 * LOG2E
    ca, sa, cd, sd = ca_ref[...], sa_ref[...], cd_ref[...], sd_ref[...]
    cq = _rms(za[:, :A_Q_LORA], gq_ref[...]).astype(BF16)
    q = jnp.dot(cq, wuq_ref[...], preferred_element_type=F32)
    ckv = _rms(za[:, A_Q_LORA:A_Q_LORA + A_KV_LORA], gkv_ref[...]).astype(BF16)
    kv = jnp.dot(ckv, wukv_ref[...], preferred_element_type=F32)
    k_rope = _rope_lanes(za[:, A_Q_LORA + A_KV_LORA:], ca, sa).astype(BF16)
    nope_w = A_HEADS * A_NOPE
    for h in range(A_HEADS):
        lo = h * A_QK_W
        qa_ref[:, lo:lo + LANES] = (q[:, h * LANES:(h + 1) * LANES] * a_scale).astype(BF16)
        q_rope = _rope_lanes(q[:, nope_w + h * LANES:nope_w + (h + 1) * LANES], ca, sa)
        qa_ref[:, lo + LANES:lo + 2 * LANES] = (q_rope * a_scale).astype(BF16)
        ka_ref[h, :, :LANES] = kv[:, h * LANES:(h + 1) * LANES].astype(BF16)
        ka_ref[h, :, LANES:] = k_rope
        va_ref[h] = kv[:, nope_w + h * A_V:nope_w + (h + 1) * A_V].astype(BF16)
    for h in range(D_HEADS):
        xh = _rms(zd[:, h * D_HEAD:(h + 1) * D_HEAD], gdq_ref[...])
        qd_ref[:, h * D_HEAD:(h + 1) * D_HEAD] = (_rope_lanes(xh, cd, sd) * d_scale).astype(BF16)
    k_off = D_HEADS * D_HEAD
    for h in range(D_KV_HEADS):
        xh = _rms(zd[:, k_off + h * D_HEAD:k_off + (h + 1) * D_HEAD], gdk_ref[...])
        kd_ref[h] = _rope_lanes(xh, cd, sd).astype(BF16)
        v_lo = k_off + (D_KV_HEADS + h) * D_HEAD
        vd_ref[h] = zd[:, v_lo:v_lo + D_HEAD].astype(BF16)


def _qk(q, k):
    return lax.dot_general(q, k, (((1,), (1,)), ((), ())), preferred_element_type=F32)


def _lane_fold(fn, acc, x):
    for c in range(x.shape[1] // LANES):
        acc = fn(acc, x[:, c * LANES:(c + 1) * LANES])
    return acc


def _lane_blocks(x):
    return [x[:, i * LANES:(i + 1) * LANES] for i in range(x.shape[1] // LANES)]


def _score_pass(q, k_ref, s_ref, m_ref, tk, n_steps, sub_max=None):
    tq, ts = s_ref.shape[1:]
    sub = tk // ts
    m_ref[...] = jnp.full((tq, LANES), -jnp.inf, F32)

    def step(j, _):
        ks = pl.multiple_of(j * tk, tk)
        s = _qk(q, k_ref[0, pl.ds(ks, tk), :])
        mrun = None
        for c in range(sub):
            jj = j * sub + c
            sc = s[:, c * ts:(c + 1) * ts]
            s_ref[jj] = sc
            blocks = _lane_blocks(sc)
            m_sub = functools.reduce(jnp.maximum, blocks) if sub_max is None else sub_max(jj, blocks)
            mrun = m_sub if mrun is None else jnp.maximum(mrun, m_sub)
        m_ref[...] = jnp.maximum(m_ref[...], mrun)
        return 0

    lax.fori_loop(0, n_steps, step, 0)
    return m_ref[...]


def _value_pass(v_ref, s_ref, a_ref, tk, n_steps, m_of):
    tq, ts = s_ref.shape[1:]
    sub = tk // ts
    dv = v_ref.shape[2]
    ones = jnp.ones((tk, LANES), BF16)
    a_ref[...] = jnp.zeros(a_ref.shape, F32)

    def step(j, _):
        ks = pl.multiple_of(j * tk, tk)
        p = []
        for c in range(sub):
            jj = j * sub + c
            m = m_of(jj)
            s = s_ref[jj]
            p += [jnp.exp2((s[:, i * LANES:(i + 1) * LANES] - m).astype(BF16)) for i in range(ts // LANES)]
        v_ext = jnp.concatenate([v_ref[0, pl.ds(ks, tk), :], ones], axis=1)
        a_ref[...] += jnp.dot(jnp.concatenate(p, axis=1), v_ext, preferred_element_type=F32)
        return 0

    lax.fori_loop(0, n_steps, step, 0)
    return a_ref[:, :dv] / a_ref[:, dv:]


def _row_max(mrun):
    return jnp.broadcast_to(jnp.max(mrun, axis=-1, keepdims=True), mrun.shape)


def _attn_scratch(n_sub, rows, ts):
    return [pltpu.VMEM((n_sub, rows, ts), F32), pltpu.VMEM((rows, LANES), F32), pltpu.VMEM((rows, 2 * LANES), F32)]


def _flash_body(n_ref, q_ref, k_ref, v_ref, o_ref, s_ref, m_ref, a_ref, *, tk, stack):
    tq = q_ref.shape[1]
    q_w = q_ref.shape[2] // stack
    q = jnp.concatenate([q_ref[0, :, c * q_w:(c + 1) * q_w] for c in range(stack)], axis=0)
    m = _row_max(_score_pass(q, k_ref, s_ref, m_ref, tk, n_ref[0]))
    o = _value_pass(v_ref, s_ref, a_ref, tk, n_ref[0], lambda jj: m).astype(o_ref.dtype)
    o_ref[0] = jnp.concatenate([o[c * tq:(c + 1) * tq] for c in range(stack)], axis=1)


def _flash(q, k, v, *, kv_heads, stack, q_w, tq, ts, tk):
    B, S, _ = q.shape
    dv = LANES
    return pl.pallas_call(
        functools.partial(_flash_body, tk=tk, stack=stack),
        grid=(B, kv_heads, S // tq),
        in_specs=[pl.BlockSpec(memory_space=pltpu.SMEM),
                  pl.BlockSpec((1, tq, stack * q_w), lambda b, h, i: (b, i, h)),
                  pl.BlockSpec((1, S, q_w), lambda b, h, i: (h * B + b, 0, 0)),
                  pl.BlockSpec((1, S, dv), lambda b, h, i: (h * B + b, 0, 0))],
        out_specs=pl.BlockSpec((1, tq, stack * dv), lambda b, h, i: (b, i, h)),
        out_shape=jax.ShapeDtypeStruct((B, S, kv_heads * stack * dv), BF16),
        scratch_shapes=_attn_scratch(S // ts, stack * tq, ts),
        compiler_params=_cparams("parallel", "parallel", "arbitrary"),
        name="flash_attn",
    )(jnp.full((1,), S // tk, jnp.int32), q, k, v)


def _diff_body(n_ref, lam_ref, cb_ref, q_ref, k_ref, v_ref, diag_ref, corner_ref, o_ref, s_ref, m_ref, a_ref, *,
               tk, out_scale):
    h, qi = pl.program_id(1), pl.program_id(2)
    n_sub = s_ref.shape[0]
    tq = q_ref.shape[1]
    nd = REL_MAX_DIST
    qf = q_ref[0].astype(F32) * (B_QK ** -0.5 * LOG2E)
    lane = lax.broadcasted_iota(jnp.int32, qf.shape, 1)
    c_lo, c_hi = cb_ref[0, h], cb_ref[1, h]
    side_shift = lambda jj: jnp.where(jj < qi, c_lo, jnp.where(jj > qi, c_hi, 0.0))
    tile_max = lambda s: _lane_fold(jnp.maximum, s[:, :LANES], s[:, LANES:])

    q = jnp.concatenate([jnp.where(lane < B_QK, qf, 0.0), jnp.where(lane >= B_QK, qf, 0.0)], axis=0).astype(BF16)
    row = lax.broadcasted_iota(jnp.int32, (2 * tq, LANES), 0) % tq
    pen_lo = jnp.where(row < nd, -jnp.inf, 0.0)
    pen_hi = jnp.where(row >= tq - nd, -jnp.inf, 0.0)
    zero = jnp.zeros_like(pen_lo)

    def sub_max(jj, blocks):
        blocks = list(blocks)
        blocks[-1] = blocks[-1] + jnp.where(jj == qi - 1, pen_lo, zero)
        blocks[0] = blocks[0] + jnp.where(jj == qi + 1, pen_hi, zero)
        shift = jnp.where(jj < qi, c_lo, jnp.where(jj > qi, c_hi, -jnp.inf))
        return functools.reduce(jnp.maximum, blocks) + shift

    mrun = _score_pass(q, k_ref, s_ref, m_ref, tk, n_ref[0], sub_max=sub_max)

    diag = diag_ref[0]
    s = s_ref[qi] + jnp.concatenate([diag, diag], axis=0)
    s_ref[qi] = s
    mrun = jnp.maximum(mrun, tile_max(s))

    def neighbour(jj, rows0, cols0, corner, c):
        def fix(mrun):
            for r in (rows0, tq + rows0):
                fixed = s_ref[jj, r:r + nd, cols0:cols0 + nd] + corner
                s_ref[jj, r:r + nd, cols0:cols0 + nd] = fixed
                pieces = [mrun[:r], jnp.maximum(mrun[r:r + nd], fixed + c), mrun[r + nd:]]
                mrun = jnp.concatenate([p for p in pieces if p.shape[0]], axis=0)
            return mrun
        return fix

    mrun = lax.cond(qi >= 1, neighbour(jnp.maximum(qi - 1, 0), 0, tq - nd, corner_ref[0, 0], c_lo),
                    lambda m_: m_, mrun)
    mrun = lax.cond(qi + 1 < n_sub, neighbour(jnp.minimum(qi + 1, n_sub - 1), tq - nd, 0, corner_ref[0, 1], c_hi),
                    lambda m_: m_, mrun)
    m = _row_max(mrun)
    o = _value_pass(v_ref, s_ref, a_ref, tk, n_ref[0], lambda jj: m - side_shift(jj))
    o = o[:tq] - lam_ref[0] * o[tq:]
    o_ref[0] = (_rms(o) * out_scale).astype(o_ref.dtype)


def _rel_bucket(rel):
    half = REL_BUCKETS // 2
    max_exact = half // 2
    ret = (rel > 0).astype(jnp.int32) * half
    n = jnp.abs(rel)
    large = max_exact + (jnp.log(jnp.maximum(n, 1).astype(F32) / max_exact)
                         / math.log(REL_MAX_DIST / max_exact) * (half - max_exact)).astype(jnp.int32)
    large = jnp.minimum(large, half - 1)
    return ret + jnp.where(n < max_exact, n, large)


def _diff_bias_tables(rel_bias, tq):
    nd = REL_MAX_DIST
    rb = rel_bias.astype(F32) * LOG2E
    far = jnp.stack([rb[REL_BUCKETS // 2 - 1], rb[REL_BUCKETS - 1]])

    def table(rel):
        bucket = _rel_bucket(rel)
        out = jnp.zeros((B_HEADS,) + rel.shape, F32)
        for b in range(REL_BUCKETS):
            out = jnp.where(bucket[None] == b, rb[b].reshape((B_HEADS,) + (1,) * rel.ndim), out)
        return out

    a, e = jnp.arange(tq), jnp.arange(nd)
    diag = table(a[None, :] - a[:, None])
    lo = table((tq - nd + e[None, :]) - tq - e[:, None]) - far[0][:, None, None]
    hi = table(e[None, :] + tq - (tq - nd + e[:, None])) - far[1][:, None, None]
    return diag, jnp.stack([lo, hi], axis=1), far


def _diff_attn(zb, kb, vb, lam, diag, corner, far, *, tq, tk, out_scale):
    B, S, _ = zb.shape
    nd = REL_MAX_DIST
    assert tq >= nd, "far key chunks must lie beyond the last distinct relative bucket"
    assert nd == LANES, "the neighbours' near corner is handled as one lane block"
    H = B_HEADS
    smem = pl.BlockSpec(memory_space=pltpu.SMEM)
    return pl.pallas_call(
        functools.partial(_diff_body, tk=tk, out_scale=out_scale),
        grid=(B, H, S // tq),
        in_specs=[smem, smem, smem,
                  pl.BlockSpec((1, tq, B_V), lambda b, h, i: (b, i, h)),
                  pl.BlockSpec((1, S, B_V), lambda b, h, i: (h * B + b, 0, 0)),
                  pl.BlockSpec((1, S, B_V), lambda b, h, i: (h * B + b, 0, 0)),
                  pl.BlockSpec((1, tq, tq), lambda b, h, i: (h, 0, 0)),
                  pl.BlockSpec((1, 2, nd, nd), lambda b, h, i: (h, 0, 0, 0))],
        out_specs=pl.BlockSpec((1, tq, B_V), lambda b, h, i: (b, i, h)),
        out_shape=jax.ShapeDtypeStruct((B, S, H * B_V), BF16),
        scratch_shapes=_attn_scratch(S // tq, 2 * tq, tq),
        compiler_params=_cparams("parallel", "parallel", "arbitrary"),
        name="diff_attn",
    )(jnp.full((1,), S // tk, jnp.int32), lam, far, zb, kb, vb, diag, corner)


def _pool_body(u_ref, w_ref, sc_ref, o_ref, pad_ref, *, chunk):
    g = pl.program_id(1)
    S = u_ref.shape[1]
    pad_ref[0:HALO, :] = jnp.zeros((HALO, LANES), F32)
    pad_ref[HALO + S:, :] = jnp.zeros((HALO, LANES), F32)
    pad_ref[HALO:HALO + S, :] = u_ref[0].astype(F32)
    w_mat, sc = w_ref[0], sc_ref[...]

    def pooled(win):
        def body(c, _):
            r0 = pl.multiple_of(c * chunk, chunk)
            tot = pad_ref[pl.ds(r0 + HALO - win // 2, chunk), :]
            for j in range(1 - win // 2, win // 2):
                tot = tot + pad_ref[pl.ds(r0 + HALO + j, chunk), :]
            t = r0 + lax.broadcasted_iota(jnp.int32, (chunk, 1), 0)
            cnt = jnp.clip(t - win // 2 + win, 0, S) - jnp.clip(t - win // 2, 0, S)
            d = tot / cnt.astype(F32) - pad_ref[pl.ds(r0 + HALO, chunk), :]
            y = jnp.dot(d.astype(BF16), w_mat, preferred_element_type=F32) * sc
            o_ref[0, pl.ds(r0, chunk), :] = y.astype(o_ref.dtype)
            return 0
        lax.fori_loop(0, S // chunk, body, 0)

    for gi, win in enumerate(POOL_WINDOWS):
        pl.when(g == gi)(functools.partial(pooled, win))


def _pool(zc, w_pool, pool_scale, chunk):
    B, S, _ = zc.shape
    blk = pl.BlockSpec((1, S, LANES), lambda b, g: (b, 0, g))
    return pl.pallas_call(
        functools.partial(_pool_body, chunk=chunk),
        grid=(B, C_GROUPS),
        in_specs=[blk, pl.BlockSpec((1, C_GROUP_W, C_GROUP_W), lambda b, g: (g, 0, 0)),
                  pl.BlockSpec((1, LANES), lambda b, g: (0, g))],
        out_specs=blk,
        out_shape=jax.ShapeDtypeStruct((B, S, BRANCH_W), BF16),
        scratch_shapes=[pltpu.VMEM((S + 2 * HALO, LANES), F32)],
        compiler_params=_cparams("parallel", "arbitrary"),
        name="pool_mixer",
    )(zc, w_pool, pool_scale)


def _gated_merge_body(h_ref, oa_ref, ob_ref, oc_ref, od_ref, wg_ref, bg_ref, wl_ref, m_ref):
    h = h_ref[...]
    merged = None
    for b, o_ref in enumerate((oa_ref, ob_ref, oc_ref, od_ref)):
        gate = jax.nn.sigmoid(jnp.dot(h, wg_ref[b], preferred_element_type=F32) + bg_ref[b:b + 1, :])
        term = gate * jnp.dot(o_ref[...], wl_ref[b], preferred_element_type=F32)
        merged = term if merged is None else merged + term
    m_ref[...] = merged.astype(BF16)


def _gated_merge(h, outs, wg, bg, wl, tm, tn):
    T, D = h.shape
    row = lambda w_: pl.BlockSpec((tm, w_), lambda j, i: (i, 0))
    return pl.pallas_call(
        _gated_merge_body,
        grid=(D // tn, T // tm),
        in_specs=[row(D)] + [row(BRANCH_W)] * N_BRANCH + [
            pl.BlockSpec((N_BRANCH, D, tn), lambda j, i: (0, 0, j)),
            pl.BlockSpec((N_BRANCH, tn), lambda j, i: (0, j)),
            pl.BlockSpec((N_BRANCH, BRANCH_W, tn), lambda j, i: (0, 0, j))],
        out_specs=pl.BlockSpec((tm, tn), lambda j, i: (i, j)),
        out_shape=jax.ShapeDtypeStruct((T, D), BF16),
        compiler_params=_cparams("arbitrary", "parallel"),
        name="gated_merge",
    )(h, *outs, wg, bg, wl)


ROUTE_ROWS = 8


def _out_proj_body(m_ref, x_ref, wo_ref, g2_ref, wr_ref, rb_ref, xn_ref, h2_ref, rt_ref):
    xn = x_ref[...] + jnp.dot(m_ref[...], wo_ref[...], preferred_element_type=F32)
    xn_ref[...] = xn
    h2 = _rms(xn, g2_ref[...])
    h2_ref[...] = h2.astype(BF16)
    h_hi = h2.astype(BF16)
    h_lo = (h2 - h_hi.astype(F32)).astype(BF16)
    hi = jnp.dot(h_hi, wr_ref[...], preferred_element_type=F32)
    logits = hi[:, :ROUTER_W] + hi[:, ROUTER_W:] + jnp.dot(h_lo, wr_ref[:, :ROUTER_W], preferred_element_type=F32)
    rt_ref[...] = _route_tile(logits + rb_ref[...])


def _out_proj(merged, x, wo, g2, wr, route_bias, tm):
    T, D = x.shape
    row = pl.BlockSpec((tm, D), lambda i: (i, 0))
    return pl.pallas_call(
        _out_proj_body,
        grid=(T // tm,),
        in_specs=[row, row, _resident((D, D)), _resident((1, D)), _resident((D, 2 * ROUTER_W)),
                  _resident((1, ROUTER_W))],
        out_specs=[row, row, pl.BlockSpec((ROUTE_ROWS, tm), lambda i: (0, i))],
        out_shape=[jax.ShapeDtypeStruct((T, D), F32), jax.ShapeDtypeStruct((T, D), BF16),
                   jax.ShapeDtypeStruct((ROUTE_ROWS, T), F32)],
        compiler_params=_cparams("parallel"),
        name="out_proj_route",
    )(merged, x, wo, g2, wr, route_bias)


def _route_tile(lg):
    lane = lax.broadcasted_iota(jnp.int32, lg.shape, 1).astype(F32)
    neg = -jnp.inf
    row_max = lambda v: jnp.max(v, axis=-1, keepdims=True)
    first_at = lambda v, m: jnp.min(jnp.where(v == m, lane, float(LANES)), axis=-1, keepdims=True)
    g_lg = jnp.where(lane < N_GROUPS, lg, neg)
    g_max = row_max(g_lg)
    g_top = first_at(g_lg, g_max)
    pg_top = 1.0 / jnp.sum(jnp.exp(g_lg - g_max), axis=-1, keepdims=True)
    e_lo = N_GROUPS + EXP_PER_GROUP * g_top
    e_lg = jnp.where(jnp.logical_and(lane >= e_lo, lane < e_lo + EXP_PER_GROUP), lg, neg)
    picks, maxes = [], []
    for _ in range(TOP_K):
        m = row_max(e_lg)
        i = first_at(e_lg, m)
        picks.append(i)
        maxes.append(m)
        e_lg = jnp.where(lane == i, neg, e_lg)
    w = [jnp.exp(m - maxes[0]) for m in maxes]
    w_sum = functools.reduce(lambda a, b: a + b, w)
    cols = [i - N_GROUPS for i in picks] + [pg_top * wk / w_sum for wk in w]
    tile = jnp.zeros(lg.shape, F32)
    for c, v in enumerate(cols):
        tile = jnp.where(lane == c, v, tile)
    return tile.T[:ROUTE_ROWS]


def _dispatch(routed, tb):
    T = routed.shape[1]
    A = T * TOP_K
    flat_e = routed[:TOP_K].astype(jnp.int32).T.reshape(A)
    flat_w = routed[TOP_K:2 * TOP_K].T.reshape(A)
    iota = jnp.arange(A, dtype=jnp.int32)
    se, order, sw = lax.sort((flat_e, iota, flat_w), num_keys=1, is_stable=True)
    experts = jnp.arange(N_EXPERTS, dtype=jnp.int32)
    counts = jnp.sum((flat_e[:, None] == experts[None, :]).astype(jnp.int32), axis=0)
    start = jnp.cumsum(counts) - counts
    padded = (counts + tb - 1) // tb * tb
    pend = jnp.cumsum(padded)
    pstart = pend - padded
    n_blocks = -(-A // tb) + N_EXPERTS
    P = n_blocks * tb
    blk_first = jnp.arange(n_blocks, dtype=jnp.int32) * tb
    blk_exp = jnp.minimum(jnp.sum((pend[None, :] <= blk_first[:, None]).astype(jnp.int32), axis=1), N_EXPERTS - 1)
    n_valid = (pend[-1] // tb).astype(jnp.int32).reshape(1)
    slot = jnp.arange(P, dtype=jnp.int32)
    before = (pend[None, :] <= slot[:, None]).astype(jnp.int32)
    shift = jnp.sum(before * (padded - counts)[None, :], axis=1)
    in_pad = jnp.any(jnp.logical_and(slot[:, None] >= (pstart + counts)[None, :], slot[:, None] < pend[None, :]),
                     axis=1)
    valid = jnp.logical_and(jnp.logical_not(in_pad), slot < pend[-1])
    src = jnp.where(valid, slot - shift, slot % A)
    slot_tok = order[src] // TOP_K
    slot_w = jnp.where(valid, sw[src], 0.0)
    dest = iota + jnp.sum((se[:, None] == experts[None, :]).astype(jnp.int32) * (pstart - start)[None, :], axis=1)
    _, slot_of = lax.sort((order, dest), num_keys=1)
    return slot_tok, slot_w, blk_exp, n_valid, slot_of.reshape(T, TOP_K).T


def _expert_body(be_ref, nv_ref, x_ref, sw_ref, wg_ref, wu_ref, wd_ref, o_ref, wg_s, wu_s, wd_s):
    i = pl.program_id(0)

    @pl.when(jnp.logical_or(i == 0, be_ref[i] != be_ref[jnp.maximum(i - 1, 0)]))
    def _():
        wg_s[...] = wg_ref[0, 0].astype(BF16)
        wu_s[...] = wu_ref[0, 0].astype(BF16)
        wd_s[...] = wd_ref[0, 0].astype(BF16)

    @pl.when(i < nv_ref[0])
    def _():
        x = x_ref[...]
        g = jnp.dot(x, wg_s[...], preferred_element_type=F32)
        u = jnp.dot(x, wu_s[...], preferred_element_type=F32)
        a = g * jax.nn.sigmoid(g) * u
        sw = sw_ref[...]
        a = jnp.concatenate([a[:, c * LANES:(c + 1) * LANES] * sw for c in range(a.shape[1] // LANES)], axis=1)
        o_ref[...] = jnp.dot(a.astype(BF16), wd_s[...], preferred_element_type=F32).astype(o_ref.dtype)

    @pl.when(i >= nv_ref[0])
    def _():
        o_ref[...] = jnp.zeros(o_ref.shape, o_ref.dtype)


def _expert_ffn(xs, slot_w, blk_exp, n_valid, wg, wu, wd, layer, tb):
    P, D = xs.shape
    grid_spec = pltpu.PrefetchScalarGridSpec(
        num_scalar_prefetch=2,
        grid=(P // tb,),
        in_specs=[pl.BlockSpec((tb, D), lambda i, be, nv: (i, 0)),
                  pl.BlockSpec((tb, LANES), lambda i, be, nv: (i, 0)),
                  pl.BlockSpec((1, 1, D, D_EXPERT), lambda i, be, nv: (layer, be[i], 0, 0)),
                  pl.BlockSpec((1, 1, D, D_EXPERT), lambda i, be, nv: (layer, be[i], 0, 0)),
                  pl.BlockSpec((1, 1, D_EXPERT, D), lambda i, be, nv: (layer, be[i], 0, 0))],
        out_specs=pl.BlockSpec((tb, D), lambda i, be, nv: (i, 0)),
        scratch_shapes=[pltpu.VMEM((D, D_EXPERT), BF16), pltpu.VMEM((D, D_EXPERT), BF16),
                        pltpu.VMEM((D_EXPERT, D), BF16)],
    )
    return pl.pallas_call(
        _expert_body,
        grid_spec=grid_spec,
        out_shape=jax.ShapeDtypeStruct((P, D), BF16),
        compiler_params=_cparams("arbitrary"),
        name="expert_ffn",
    )(blk_exp, n_valid, xs, jnp.broadcast_to(slot_w[:, None], (P, LANES)), wg, wu, wd)


def _final_norm_body(*refs):
    x_ref, y_refs, (g_ref, o_ref) = refs[0], refs[1:-2], refs[-2:]
    o_ref[...] = _rms(_residual_sum(x_ref, y_refs), g_ref[...])


def _final_norm(x, adds, g, row0, rows, tm):
    T, D = x.shape
    first = row0 // tm
    add_specs, add_args = _stacked_adds(adds, T, D, tm, first)
    return pl.pallas_call(
        _final_norm_body, grid=(rows // tm,),
        in_specs=[pl.BlockSpec((tm, D), lambda i: (first + i, 0))] + add_specs + [_resident((1, D))],
        out_specs=pl.BlockSpec((tm, D), lambda i: (i, 0)),
        out_shape=jax.ShapeDtypeStruct((rows, D), F32), compiler_params=_cparams("parallel"), name="final_norm",
    )(x, *add_args, g)


def _rope_tables(seq):
    def angles(pos, dim):
        inv = 1.0 / (ROPE_THETA ** (jnp.arange(0, dim, 2, dtype=F32) / dim))
        ang = pos.astype(F32)[:, None] * inv[None, :]
        return jnp.cos(ang), jnp.sin(ang)
    pos = jnp.arange(seq)
    c1, s1 = angles(pos, A_ROPE)
    cr, sr = angles(pos // GRID_W, D_HEAD // 2)
    cc, sc = angles(pos % GRID_W, D_HEAD // 2)
    pad1, pad0 = jnp.ones((seq, ROPE_PAD), F32), jnp.zeros((seq, ROPE_PAD), F32)
    return (jnp.concatenate([c1, c1, pad1], 1), jnp.concatenate([-s1, s1, pad0], 1),
            jnp.concatenate([cr, cr, cc, cc], 1), jnp.concatenate([-sr, sr, -sc, sc], 1))


def _layer_weights(l, w_in, w_uq, w_ukv, w_route_group, w_route_expert):
    D = D_MODEL
    k_r_end = A_Q_LORA + A_KV_LORA + A_ROPE
    win = jnp.concatenate([w_in[l][:, :k_r_end], jnp.zeros((D, ROPE_PAD), F32), w_in[l][:, k_r_end:]], 1)
    uq = w_uq[l].reshape(A_Q_LORA, A_HEADS, A_NOPE + A_ROPE)
    uq_rope = jnp.pad(uq[:, :, A_NOPE:], ((0, 0), (0, 0), (0, ROPE_PAD)))
    wuq = jnp.concatenate([uq[:, :, :A_NOPE].reshape(A_Q_LORA, -1), uq_rope.reshape(A_Q_LORA, -1)], 1)
    ukv = w_ukv[l].reshape(A_KV_LORA, A_HEADS, A_NOPE + A_V)
    wukv = jnp.concatenate([ukv[:, :, :A_NOPE].reshape(A_KV_LORA, -1), ukv[:, :, A_NOPE:].reshape(A_KV_LORA, -1)], 1)
    wr = jnp.concatenate([w_route_group[l], w_route_expert[l],
                          jnp.zeros((D, ROUTER_W - N_GROUPS - N_EXPERTS), F32)], 1)
    wr_hi = wr.astype(BF16)
    wr_lo = (wr - wr_hi.astype(F32)).astype(BF16)
    return win.astype(BF16), wuq.astype(BF16), wukv.astype(BF16), jnp.concatenate([wr_hi, wr_lo], 1)


def kernel(x_prompt, x_sample, norm1_g, w_in, q_norm_g, kv_norm_g, w_uq, w_ukv, lam_q1, lam_k1, lam_q2, lam_k2,
           rel_bias, w_pool, pool_scale, qk_norm_q, qk_norm_k, w_lift, w_gate, b_gate, w_out, norm2_g,
           w_route_group, b_route_group, w_route_expert, b_route_expert, w_exp_gate, w_exp_up, w_exp_down,
           final_g):
    assert x_prompt.shape[1:] == x_sample.shape[1:], "both request groups must share (seq, d_model)"
    n_prompt, S, D = x_prompt.shape
    B = n_prompt + x_sample.shape[0]
    T = B * S
    t = _tiles(S, T)
    x = None
    depth = w_in.shape[0]
    tabs = _rope_tables(S)
    diag, corner, far = _diff_bias_tables(rel_bias, t["tq"])
    row2 = lambda v: v.reshape(1, -1).astype(F32)

    moe_out = None
    for l in range(depth):
        lam_init = 0.8 - 0.6 * math.exp(-0.3 * l)
        lam = (jnp.exp(jnp.sum(lam_q1[l] * lam_k1[l])) - jnp.exp(jnp.sum(lam_q2[l] * lam_k2[l])) + lam_init)
        win, wuq, wukv, wr = _layer_weights(l, w_in, w_uq, w_ukv, w_route_group, w_route_expert)

        xs = (x_prompt.reshape(-1, D), x_sample.reshape(-1, D)) if l == 0 else (x,)
        prep = (row2(q_norm_g[l]), row2(kv_norm_g[l]), wuq, wukv, row2(qk_norm_q[l]), row2(qk_norm_k[l])) + tabs
        x, h, zb, zc, qa, ka, va, qd, kd, vd, kb, vb = _norm_inproj(xs, moe_out, row2(norm1_g[l]), win, prep, S,
                                                                    t["tm_in"])
        seq3 = lambda a: a.reshape(-1, S, a.shape[-1])
        rep = D_HEADS // D_KV_HEADS
        o_a = _flash(seq3(qa), seq3(ka), seq3(va), kv_heads=A_HEADS, stack=1, q_w=A_QK_W,
                     tq=t["rows"], ts=t["tq"], tk=t["tk"])
        o_b = _diff_attn(seq3(zb), seq3(kb), seq3(vb), lam.reshape(1).astype(F32), diag, corner, far, tq=t["tq"],
                         tk=t["tk"], out_scale=1.0 - lam_init)
        o_c = _pool(seq3(zc), w_pool[l].astype(BF16), row2(pool_scale[l]), t["pool_chunk"])
        o_d = _flash(seq3(qd), seq3(kd), seq3(vd), kv_heads=D_KV_HEADS, stack=rep, q_w=D_HEAD,
                     tq=t["rows"] // rep, ts=t["tq"], tk=t["tk"])
        outs = [o.reshape(T, BRANCH_W) for o in (o_a, o_b, o_c, o_d)]
        merged = _gated_merge(h, outs, w_gate[l].astype(BF16), b_gate[l].astype(F32), w_lift[l].astype(BF16),
                              t["tm_merge"], t["tn"])
        route_bias = jnp.concatenate([b_route_group[l], b_route_expert[l],
                                      jnp.zeros((ROUTER_W - N_GROUPS - N_EXPERTS,), F32)]).reshape(1, ROUTER_W)
        x, h2, routed = _out_proj(merged, x, w_out[l].astype(BF16), row2(norm2_g[l]), wr, route_bias.astype(F32),
                                  t["tm"])
        slot_tok, slot_w, blk_exp, n_valid, slot_of = _dispatch(routed, t["tb"])
        ys = _expert_ffn(h2[slot_tok], slot_w, blk_exp, n_valid, w_exp_gate, w_exp_up, w_exp_down, l, t["tb"])
        moe_out = ys[slot_of.reshape(-1)]

    t_prompt = n_prompt * S
    y_prompt = _final_norm(x, moe_out, row2(final_g), 0, t_prompt, t["tm"]).reshape(n_prompt, S, D)
    y_sample = _final_norm(x, moe_out, row2(final_g), t_prompt, T - t_prompt, t["tm"]).reshape(B - n_prompt, S, D)
    return y_prompt, y_sample
```
